```python
import math
import jax, jax.numpy as jnp
from jax import lax
import numpy as np

D_MODEL = 2048
BATCH = 2
SEQ = 4096
DEPTH = 1

CHUNK = 64

D_MIX = D_MODEL
D_LRU = D_MIX // 2
D_CONV = D_MIX - D_LRU
LRU_HEADS = 16
LRU_HEAD_DIM = D_LRU // LRU_HEADS
LRU_C = 8.0
LRU_CONV_W = 4
CONF_GROUPS = 16
CONF_CONV_W = 31
N_GROUPS = 4
EXPERTS_PER_GROUP = 8
TOP_K = 2
D_EXPERT = D_MODEL // 4
LN_EPS = 1e-5
DN_ALPHA = (2 * DEPTH) ** 0.25
DN_BETA = (8 * DEPTH) ** -0.25

kernel_name = "hymba_rglru_conformer_hiermoe_deepnorm"


def layer_norm(x, g, b):
    xf = x.astype(jnp.float32)
    mu = jnp.mean(xf, axis=-1, keepdims=True)
    var = jnp.mean(jnp.square(xf - mu), axis=-1, keepdims=True)
    y = (xf - mu) * lax.rsqrt(var + LN_EPS) * g.astype(jnp.float32) + b.astype(jnp.float32)
    return y.astype(x.dtype)


def causal_depthwise_conv(x, w, b):
    k, c = w.shape
    y = lax.conv_general_dilated(
        x, w[:, None, :].astype(x.dtype), window_strides=(1,), padding=[(k - 1, 0)],
        dimension_numbers=("NWC", "WIO", "NWC"), feature_group_count=c)
    return y + b.astype(x.dtype)


def rg_lru(x, w_a, b_a, w_x, b_x, lam):
    bsz, t, c = x.shape
    xh = x.reshape(bsz, t, LRU_HEADS, LRU_HEAD_DIM)
    r = jax.nn.sigmoid(jnp.einsum("bthi,hij->bthj", xh, w_a).reshape(bsz, t, c) + b_a)
    i = jax.nn.sigmoid(jnp.einsum("bthi,hij->bthj", xh, w_x).reshape(bsz, t, c) + b_x)
    log_a = -LRU_C * r.astype(jnp.float32) * jax.nn.softplus(-lam.astype(jnp.float32))
    a = jnp.exp(log_a)
    u = jnp.sqrt(-jnp.expm1(2.0 * log_a)) * (i * x).astype(jnp.float32)

    def combine(left, right):
        a1, b1 = left
        a2, b2 = right
        return a1 * a2, a2 * b1 + b2

    _, h = lax.associative_scan(combine, (a, u), axis=1)
    return h.astype(x.dtype)


def hier_moe(x, wg, bg, we, be, w_gate, w_up, w_down):
    bsz, t, _ = x.shape
    lg = (jnp.einsum("btd,dg->btg", x, wg) + bg).astype(jnp.float32)
    pg = jax.nn.softmax(lg, axis=-1)
    pg_top, g_sel = lax.top_k(pg, 1)
    le = (jnp.einsum("btd,de->bte", x, we) + be).astype(jnp.float32)
    le = le.reshape(bsz, t, N_GROUPS, EXPERTS_PER_GROUP)
    le_sel = jnp.take_along_axis(le, g_sel[..., None], axis=2)[:, :, 0]
    top_v, top_i = lax.top_k(le_sel, TOP_K)
    q = jax.nn.softmax(top_v, axis=-1) * pg_top
    w_e = jnp.sum(jax.nn.one_hot(top_i, EXPERTS_PER_GROUP, dtype=jnp.float32) * q[..., None], axis=-2)
    w_full = jax.nn.one_hot(g_sel[..., 0], N_GROUPS, dtype=jnp.float32)[..., None] * w_e[:, :, None, :]
    w_full = w_full.astype(x.dtype)
    y = jnp.zeros_like(x)
    for g in range(N_GROUPS):
        h = jax.nn.silu(jnp.einsum("btd,edf->btef", x, w_gate[g])) * jnp.einsum("btd,edf->btef", x, w_up[g])
        y = y + jnp.einsum("btef,efd->btd", h * w_full[:, :, g, :, None], w_down[g])
    return y


def setup_inputs(seed: int = 0) -> dict:
    key = jax.random.key(seed)
    ks = jax.random.split(key, 32)
    f32 = jnp.float32
    L = DEPTH
    G, E, F = N_GROUPS, EXPERTS_PER_GROUP, D_EXPERT

    def nrm(k, shape, scale):
        return jax.random.normal(k, shape, f32) * scale

    u = jax.random.uniform(ks[10], (L, D_LRU), f32, 0.9, 0.999)
    s = u ** (1.0 / LRU_C)
    return {
        "x": nrm(ks[0], (BATCH, SEQ, D_MODEL), 1.0),
        "ln_in_g": 1.0 + nrm(ks[1], (D_MODEL,), 0.02),
        "ln_in_b": nrm(ks[2], (D_MODEL,), 0.02),
        "w_in": nrm(ks[3], (L, D_MODEL, 2 * D_LRU + 2 * D_CONV), D_MODEL ** -0.5),
        "lru_conv_w": nrm(ks[4], (L, LRU_CONV_W, D_LRU), LRU_CONV_W ** -0.5),
        "lru_conv_b": nrm(ks[5], (L, D_LRU), 0.02),
        "lru_w_a": nrm(ks[6], (L, LRU_HEADS, LRU_HEAD_DIM, LRU_HEAD_DIM), LRU_HEAD_DIM ** -0.5),
        "lru_b_a": nrm(ks[7], (L, D_LRU), 0.02),
        "lru_w_x": nrm(ks[8], (L, LRU_HEADS, LRU_HEAD_DIM, LRU_HEAD_DIM), LRU_HEAD_DIM ** -0.5),
        "lru_b_x": nrm(ks[9], (L, D_LRU), 0.02),
        "lru_lambda": jnp.log(s) - jnp.log1p(-s),
        "conf_conv_w": nrm(ks[11], (L, CONF_CONV_W, D_CONV), CONF_CONV_W ** -0.5),
        "conf_conv_b": nrm(ks[12], (L, D_CONV), 0.02),
        "conf_ln_g": 1.0 + nrm(ks[13], (L, D_CONV), 0.02),
        "conf_ln_b": nrm(ks[14], (L, D_CONV), 0.02),
        "w_out": nrm(ks[15], (L, D_MIX, D_MODEL), D_MIX ** -0.5 * DN_BETA),
        "ln1_g": 1.0 + nrm(ks[16], (L, D_MODEL), 0.02),
        "ln1_b": nrm(ks[17], (L, D_MODEL), 0.02),
        "router_group_w": nrm(ks[18], (L, D_MODEL, G), D_MODEL ** -0.5),
        "router_group_b": nrm(ks[19], (L, G), 0.01),
        "router_expert_w": nrm(ks[20], (L, D_MODEL, G * E), D_MODEL ** -0.5),
        "router_expert_b": nrm(ks[21], (L, G * E), 0.01),
        "exp_w_gate": nrm(ks[22], (L, G, E, D_MODEL, F), D_MODEL ** -0.5),
        "exp_w_up": nrm(ks[23], (L, G, E, D_MODEL, F), D_MODEL ** -0.5),
        "exp_w_down": nrm(ks[24], (L, G, E, F, D_MODEL), F ** -0.5 * DN_BETA),
        "ln2_g": 1.0 + nrm(ks[25], (L, D_MODEL), 0.02),
        "ln2_b": nrm(ks[26], (L, D_MODEL), 0.02),
    }


def reference(x, ln_in_g, ln_in_b, w_in, lru_conv_w, lru_conv_b, lru_w_a, lru_b_a,
              lru_w_x, lru_b_x, lru_lambda, conf_conv_w, conf_conv_b, conf_ln_g, conf_ln_b,
              w_out, ln1_g, ln1_b, router_group_w, router_group_b, router_expert_w,
              router_expert_b, exp_w_gate, exp_w_up, exp_w_down, ln2_g, ln2_b):
    h = layer_norm(x, ln_in_g, ln_in_b)
    for l in range(DEPTH):
        z = jnp.einsum("btd,dc->btc", h, w_in[l])
        lru_x = z[..., :D_LRU]
        lru_g = z[..., D_LRU:2 * D_LRU]
        conf_v = z[..., 2 * D_LRU:2 * D_LRU + D_CONV]
        conf_g = z[..., 2 * D_LRU + D_CONV:]
        a_in = causal_depthwise_conv(lru_x, lru_conv_w[l], lru_conv_b[l])
        a_out = jax.nn.gelu(lru_g, approximate=True) * rg_lru(
            a_in, lru_w_a[l], lru_b_a[l], lru_w_x[l], lru_b_x[l], lru_lambda[l])
        b_in = conf_v * jax.nn.sigmoid(conf_g)
        b_c = causal_depthwise_conv(b_in, conf_conv_w[l], conf_conv_b[l])
        b_out = jax.nn.silu(layer_norm(b_c, conf_ln_g[l], conf_ln_b[l]))
        mix = jnp.einsum("btc,cd->btd", jnp.concatenate([a_out, b_out], axis=-1), w_out[l])
        h = layer_norm(DN_ALPHA * h + mix, ln1_g[l], ln1_b[l])
        ffn = hier_moe(h, router_group_w[l], router_group_b[l], router_expert_w[l],
                       router_expert_b[l], exp_w_gate[l], exp_w_up[l], exp_w_down[l])
        h = layer_norm(DN_ALPHA * h + ffn, ln2_g[l], ln2_b[l])
    return h
```

```python
import functools
import math

import jax
import jax.numpy as jnp
from jax import lax
from jax.experimental import pallas as pl
from jax.experimental.pallas import tpu as pltpu

F32 = jnp.float32
BF16 = jnp.bfloat16

D_MODEL = 2048
D_LRU = 1024
D_CONV = 1024
LRU_HEADS = 16
LRU_HEAD_DIM = 64
LRU_C = 8.0
LRU_CONV_W = 4
CONF_CONV_W = 31
N_GROUPS = 4
EXPERTS_PER_GROUP = 8
N_EXPERTS = N_GROUPS * EXPERTS_PER_GROUP
D_EXPERT = 512
LN_EPS = 1e-5
DEPTH = 1
DN_ALPHA = (2 * DEPTH) ** 0.25

LANES = 128
SUBLANES = 8
VMEM_LIMIT = 56 * 1024 * 1024

TM_WIN = 1024
TN_WIN = 1024
TT_LRU = 512
CB_LRU = 256
TT_CONF = 256
CONF_HALO = 32
TM_OUT = 512
TM_ROUTE = 512
TM_EXP = 256
TM_COMB = 256
ROUTE_W = 128


def _ln_rows(x, g, b):
    mu = jnp.mean(x, axis=-1, keepdims=True)
    xc = x - mu
    var = jnp.mean(xc * xc, axis=-1, keepdims=True)
    return xc * lax.rsqrt(var + LN_EPS) * g + b


def _ln_win_kernel(x_ref, g_ref, b_ref, w_ref, z_ref, xn_ref):
    rows = 128

    @pl.when(pl.program_id(1) == 0)
    def _():
        def body(i, _):
            rs = pl.ds(pl.multiple_of(i * rows, rows), rows)
            xn_ref[rs, :] = _ln_rows(x_ref[rs, :], g_ref[...], b_ref[...]).astype(BF16)
            return 0
        lax.fori_loop(0, TM_WIN // rows, body, 0)

    z_ref[...] = jnp.dot(xn_ref[...], w_ref[...],
                         preferred_element_type=F32).astype(z_ref.dtype)


def _ln_win(x2, g, b, w_bf16):
    n = x2.shape[0]
    ncol = w_bf16.shape[1]
    return pl.pallas_call(
        _ln_win_kernel,
        grid=(n // TM_WIN, ncol // TN_WIN),
        in_specs=[
            pl.BlockSpec((TM_WIN, D_MODEL), lambda i, j: (i, 0)),
            pl.BlockSpec((1, D_MODEL), lambda i, j: (0, 0)),
            pl.BlockSpec((1, D_MODEL), lambda i, j: (0, 0)),
            pl.BlockSpec((D_MODEL, TN_WIN), lambda i, j: (0, j)),
        ],
        out_specs=pl.BlockSpec((TM_WIN, TN_WIN), lambda i, j: (i, j)),
        out_shape=jax.ShapeDtypeStruct((n, ncol), BF16),
        scratch_shapes=[pltpu.VMEM((TM_WIN, D_MODEL), BF16)],
        compiler_params=pltpu.CompilerParams(
            dimension_semantics=("arbitrary", "arbitrary"),
            vmem_limit_bytes=VMEM_LIMIT),
        name="ln_win",
    )(x2, g, b, w_bf16)


def _lru_kernel(zx_ref, zg_ref, cw_ref, cb_ref, wcat_ref, ba_ref, bx_ref, lam_ref,
                o_ref, xs_ref, hp_ref, a_ref, g_ref):
    t = pl.program_id(2)
    tt = TT_LRU

    @pl.when(t == 0)
    def _():
        xs_ref[0:SUBLANES, :] = jnp.zeros((SUBLANES, CB_LRU), F32)
        hp_ref[...] = jnp.zeros_like(hp_ref)

    @pl.when(t > 0)
    def _():
        xs_ref[0:SUBLANES, :] = xs_ref[tt:tt + SUBLANES, :]

    xs_ref[SUBLANES:SUBLANES + tt, :] = zx_ref[...].astype(F32)

    rows = 128
    for rb in range(tt // rows):
        acc = jnp.broadcast_to(cb_ref[...], (rows, CB_LRU))
        for k in range(LRU_CONV_W):
            off = rb * rows + SUBLANES - (LRU_CONV_W - 1) + k
            acc = acc + cw_ref[k:k + 1, :] * xs_ref[off:off + rows, :]
        a_ref[rb * rows:(rb + 1) * rows, :] = acc

    g_ref[...] = jnp.dot(a_ref[...].astype(BF16), wcat_ref[0], preferred_element_type=F32)

    lam = lam_ref[...]
    softplus_neg = jnp.maximum(-lam, 0.0) + jnp.log1p(jnp.exp(-jnp.abs(lam)))
    cvec = -LRU_C * softplus_neg
    ba = ba_ref[...]
    bx = bx_ref[...]
    blk = 64
    row_in_vreg = lax.broadcasted_iota(jnp.int32, (blk, CB_LRU), 0) & (SUBLANES - 1)

    def body(rb, h):
        rs = pl.ds(pl.multiple_of(rb * blk, blk), blk)
        a_in = a_ref[rs, :]
        r = jax.nn.sigmoid(g_ref[rs, 0:CB_LRU] + ba)
        i = jax.nn.sigmoid(g_ref[rs, CB_LRU:2 * CB_LRU] + bx)
        log_a = cvec * r
        a = jnp.exp(log_a)
        u = jnp.sqrt(-jnp.tanh(log_a) * (a * a + 1.0)) * (i * a_in)
        for s in (1, 2, 4):
            m = row_in_vreg >= s
            a_sh = jnp.where(m, pltpu.roll(a, s, 0), 1.0)
            u_sh = jnp.where(m, pltpu.roll(u, s, 0), 0.0)
            u = u + a * u_sh
            a = a * a_sh
        outs = []
        for gi in range(blk // SUBLANES):
            ag = a[gi * SUBLANES:(gi + 1) * SUBLANES, :]
            ug = u[gi * SUBLANES:(gi + 1) * SUBLANES, :]
            hg = ug + ag * h
            h = hg[SUBLANES - 1:SUBLANES, :]
            outs.append(hg)
        hblk = jnp.concatenate(outs, axis=0)
        gl = zg_ref[rs, :].astype(F32)
        gelu = 0.5 * gl * (1.0 + jnp.tanh(0.7978845608028654 * (gl + 0.044715 * gl * gl * gl)))
        o_ref[rs, :] = (gelu * hblk).astype(o_ref.dtype)
        return h

    h = lax.fori_loop(0, tt // blk, body, hp_ref[0:1, :])
    hp_ref[...] = jnp.broadcast_to(h, hp_ref.shape)


def _lru_mixer(z, cw, cb, wcat, ba, bx, lam, bsz, seq):
    n = z.shape[0]
    nt = seq // TT_LRU
    ncb = D_LRU // CB_LRU
    row = lambda b, j, t: b * nt + t
    vec = pl.BlockSpec((1, CB_LRU), lambda b, j, t: (0, j))
    return pl.pallas_call(
        _lru_kernel,
        grid=(bsz, ncb, nt),
        in_specs=[
            pl.BlockSpec((TT_LRU, CB_LRU), lambda b, j, t: (row(b, j, t), j)),
            pl.BlockSpec((TT_LRU, CB_LRU), lambda b, j, t: (row(b, j, t), ncb + j)),
            pl.BlockSpec((LRU_CONV_W, CB_LRU), lambda b, j, t: (0, j)),
            vec,
            pl.BlockSpec((1, CB_LRU, 2 * CB_LRU), lambda b, j, t: (j, 0, 0)),
            vec, vec, vec,
        ],
        out_specs=pl.BlockSpec((TT_LRU, CB_LRU), lambda b, j, t: (row(b, j, t), j)),
        out_shape=jax.ShapeDtypeStruct((n, D_LRU), BF16),
        scratch_shapes=[
            pltpu.VMEM((TT_LRU + SUBLANES, CB_LRU), F32),
            pltpu.VMEM((SUBLANES, CB_LRU), F32),
            pltpu.VMEM((TT_LRU, CB_LRU), F32),
            pltpu.VMEM((TT_LRU, 2 * CB_LRU), F32),
        ],
        compiler_params=pltpu.CompilerParams(
            dimension_semantics=("arbitrary", "arbitrary", "arbitrary"),
            vmem_limit_bytes=VMEM_LIMIT),
        name="lru_mixer",
    )(z, z, cw, cb, wcat, ba, bx, lam)


def _conf_kernel(zv_ref, zg_ref, w_ref, cb_ref, lg_ref, lb_ref, o_ref, cs_ref, cv_ref):
    t = pl.program_id(1)
    tt = TT_CONF
    nlb = D_CONV // LANES

    @pl.when(t == 0)
    def _():
        cs_ref[:, 0:CONF_HALO, :] = jnp.zeros((nlb, CONF_HALO, LANES), F32)

    @pl.when(t > 0)
    def _():
        cs_ref[:, 0:CONF_HALO, :] = cs_ref[:, tt:tt + CONF_HALO, :]

    for c in range(nlb):
        ls = slice(c * LANES, (c + 1) * LANES)
        v = zv_ref[:, ls].astype(F32)
        g = zg_ref[:, ls].astype(F32)
        cs_ref[c, CONF_HALO:CONF_HALO + tt, :] = v * jax.nn.sigmoid(g)

    rows = 64
    nrb = tt // rows
    base = CONF_HALO - (CONF_CONV_W - 1)

    def conv_body(c, carry):
        accs = [jnp.broadcast_to(cb_ref[c], (rows, LANES)) for _ in range(nrb)]
        for k in range(CONF_CONV_W):
            wk = w_ref[c, k:k + 1, :]
            for rb in range(nrb):
                off = rb * rows + base + k
                accs[rb] = accs[rb] + wk * cs_ref[c, off:off + rows, :]
        for rb in range(nrb):
            cv_ref[c, rb * rows:(rb + 1) * rows, :] = accs[rb]
        return carry

    lax.fori_loop(0, nlb, conv_body, 0)

    ln_rows = 32
    inv_n = 1.0 / D_CONV
    for rb in range(tt // ln_rows):
        rs = slice(rb * ln_rows, (rb + 1) * ln_rows)
        blk = cv_ref[:, rs, :]
        mu = jnp.sum(jnp.sum(blk, axis=0), axis=-1, keepdims=True) * inv_n
        d = blk - mu[None]
        var = jnp.sum(jnp.sum(d * d, axis=0), axis=-1, keepdims=True) * inv_n
        inv = lax.rsqrt(var + LN_EPS)
        for c in range(nlb):
            ls = slice(c * LANES, (c + 1) * LANES)
            y = d[c] * inv * lg_ref[:, ls] + lb_ref[:, ls]
            o_ref[rs, ls] = (y * jax.nn.sigmoid(y)).astype(o_ref.dtype)


def _conf_mixer(z, w3, cb3, lg, lb, bsz, seq):
    n = z.shape[0]
    nt = seq // TT_CONF
    nlb = D_CONV // LANES
    return pl.pallas_call(
        _conf_kernel,
        grid=(bsz, nt),
        in_specs=[
            pl.BlockSpec((TT_CONF, D_CONV), lambda b, t: (b * nt + t, 2)),
            pl.BlockSpec((TT_CONF, D_CONV), lambda b, t: (b * nt + t, 3)),
            pl.BlockSpec((nlb, 32, LANES), lambda b, t: (0, 0, 0)),
            pl.BlockSpec((nlb, 1, LANES), lambda b, t: (0, 0, 0)),
            pl.BlockSpec((1, D_CONV), lambda b, t: (0, 0)),
            pl.BlockSpec((1, D_CONV), lambda b, t: (0, 0)),
        ],
        out_specs=pl.BlockSpec((TT_CONF, D_CONV), lambda b, t: (b * nt + t, 0)),
        out_shape=jax.ShapeDtypeStruct((n, D_CONV), BF16),
        scratch_shapes=[
            pltpu.VMEM((nlb, CONF_HALO + TT_CONF, LANES), F32),
            pltpu.VMEM((nlb, TT_CONF, LANES), F32),
        ],
        compiler_params=pltpu.CompilerParams(
            dimension_semantics=("arbitrary", "arbitrary"),
            vmem_limit_bytes=VMEM_LIMIT),
        name="conf_mixer",
    )(z, z, w3, cb3, lg, lb)


def _split_bf16(v):
    hi = v.astype(BF16)
    lo = (v - hi.astype(F32)).astype(BF16)
    return hi, lo


def _wout_kernel(a_ref, b_ref, x_ref, gin_ref, bin_ref, wa_ref, wb_ref, g1_ref, b1_ref,
                 wrh_ref, wrl_ref, br_ref, h1_ref, lg_ref, mix_ref):
    mix_ref[...] = (jnp.dot(a_ref[...], wa_ref[...], preferred_element_type=F32)
                    + jnp.dot(b_ref[...], wb_ref[...], preferred_element_type=F32))
    rows = 64

    def body(i, _):
        rs = pl.ds(pl.multiple_of(i * rows, rows), rows)
        h = _ln_rows(x_ref[rs, :], gin_ref[...], bin_ref[...])
        h1 = _ln_rows(DN_ALPHA * h + mix_ref[rs, :], g1_ref[...], b1_ref[...])
        h1_ref[rs, :] = h1
        hi, lo = _split_bf16(h1)
        lg = (jnp.dot(hi, wrh_ref[...], preferred_element_type=F32)
              + jnp.dot(hi, wrl_ref[...], preferred_element_type=F32)
              + jnp.dot(lo, wrh_ref[...], preferred_element_type=F32))
        lg_ref[rs, :] = lg + br_ref[...]
        return 0

    lax.fori_loop(0, TM_OUT // rows, body, 0)


def _wout_router(a, b, x2, gin, bin_, wa, wb, g1, b1, wrh, wrl, br):
    n = x2.shape[0]
    full = lambda shape: pl.BlockSpec(shape, lambda i: tuple(0 for _ in shape))
    return pl.pallas_call(
        _wout_kernel,
        grid=(n // TM_OUT,),
        in_specs=[
            pl.BlockSpec((TM_OUT, D_LRU), lambda i: (i, 0)),
            pl.BlockSpec((TM_OUT, D_CONV), lambda i: (i, 0)),
            pl.BlockSpec((TM_OUT, D_MODEL), lambda i: (i, 0)),
            full((1, D_MODEL)), full((1, D_MODEL)),
            full((D_LRU, D_MODEL)), full((D_CONV, D_MODEL)),
            full((1, D_MODEL)), full((1, D_MODEL)),
            full((D_MODEL, ROUTE_W)), full((D_MODEL, ROUTE_W)), full((1, ROUTE_W)),
        ],
        out_specs=[
            pl.BlockSpec((TM_OUT, D_MODEL), lambda i: (i, 0)),
            pl.BlockSpec((TM_OUT, ROUTE_W), lambda i: (i, 0)),
        ],
        out_shape=[
            jax.ShapeDtypeStruct((n, D_MODEL), F32),
            jax.ShapeDtypeStruct((n, ROUTE_W), F32),
        ],
        scratch_shapes=[pltpu.VMEM((TM_OUT, D_MODEL), F32)],
        compiler_params=pltpu.CompilerParams(
            dimension_semantics=("arbitrary",),
            vmem_limit_bytes=VMEM_LIMIT),
        name="wout_router",
    )(a, b, x2, gin, bin_, wa, wb, g1, b1, wrh, wrl, br)


def _route_kernel(lg_ref, info_ref, cnt_ref, tot_ref, run_ref, off_ref):
    phase = pl.program_id(0)
    t = pl.program_id(1)
    tm = TM_ROUTE
    l = lg_ref[...]
    lane = lax.broadcasted_iota(jnp.int32, (tm, ROUTE_W), 1)
    neg = jnp.float32(-jnp.inf)
    big = jnp.int32(1 << 20)

    gmask = lane < N_GROUPS
    gmax = jnp.max(jnp.where(gmask, l, neg), axis=-1, keepdims=True)
    gsel = jnp.min(jnp.where(gmask & (l == gmax), lane, big), axis=-1, keepdims=True)
    gsum = jnp.sum(jnp.where(gmask, jnp.exp(l - gmax), 0.0), axis=-1, keepdims=True)
    pg_top = 1.0 / gsum

    lo = N_GROUPS + EXPERTS_PER_GROUP * gsel
    emask = (lane >= lo) & (lane < lo + EXPERTS_PER_GROUP)
    v1 = jnp.max(jnp.where(emask, l, neg), axis=-1, keepdims=True)
    i1 = jnp.min(jnp.where(emask & (l == v1), lane, big), axis=-1, keepdims=True)
    emask2 = emask & (lane != i1)
    v2 = jnp.max(jnp.where(emask2, l, neg), axis=-1, keepdims=True)
    i2 = jnp.min(jnp.where(emask2 & (l == v2), lane, big), axis=-1, keepdims=True)
    e21 = jnp.exp(v2 - v1)
    q1 = pg_top / (1.0 + e21)
    q2 = pg_top * e21 / (1.0 + e21)

    oh1 = (lane == i1).astype(F32)
    oh2 = (lane == i2).astype(F32)
    ohs = oh1 + oh2

    @pl.when((phase == 0) & (t == 0))
    def _():
        tot_ref[...] = jnp.zeros_like(tot_ref)

    @pl.when(phase == 0)
    def _():
        tot_ref[...] = tot_ref[...] + jnp.sum(ohs, axis=0, keepdims=True)

    @pl.when((phase == 1) & (t == 0))
    def _():
        cnt = tot_ref[...].astype(jnp.int32)
        shift = TM_EXP.bit_length() - 1
        padded = jnp.left_shift(jnp.right_shift(cnt + (TM_EXP - 1), shift), shift)
        lane1 = lax.broadcasted_iota(jnp.int32, tot_ref.shape, 1)
        padded = jnp.where((lane1 >= N_GROUPS) & (lane1 < N_GROUPS + N_EXPERTS), padded, 0)
        inc = padded.astype(F32)
        s = 1
        while s < ROUTE_W:
            inc = inc + jnp.where(lane1 >= s, pltpu.roll(inc, s, 1), 0.0)
            s *= 2
        off_ref[...] = inc - padded.astype(F32)
        run_ref[...] = jnp.zeros_like(run_ref)

    @pl.when(phase == 1)
    def _():
        r_i = lax.broadcasted_iota(jnp.int32, (tm, tm), 0)
        c_i = lax.broadcasted_iota(jnp.int32, (tm, tm), 1)
        lower = (c_i < r_i).astype(BF16)
        cum = jnp.dot(lower, ohs.astype(BF16), preferred_element_type=F32)
        basev = off_ref[0:1, :] + run_ref[0:1, :] + cum
        p1 = jnp.sum(oh1 * basev, axis=-1, keepdims=True)
        p2 = jnp.sum(oh2 * basev, axis=-1, keepdims=True)
        run_ref[...] = run_ref[...] + jnp.sum(ohs, axis=0, keepdims=True)
        info = jnp.where(lane == 0, p1, 0.0)
        info = jnp.where(lane == 1, p2, info)
        info = jnp.where(lane == 2, q1, info)
        info = jnp.where(lane == 3, q2, info)
        info_ref[...] = info

    cnt_ref[...] = tot_ref[...]


def _route(logits):
    n = logits.shape[0]
    nt = n // TM_ROUTE
    return pl.pallas_call(
        _route_kernel,
        grid=(2, nt),
        in_specs=[pl.BlockSpec((TM_ROUTE, ROUTE_W), lambda p, t: (t, 0))],
        out_specs=[
            pl.BlockSpec((TM_ROUTE, ROUTE_W), lambda p, t: (p * t, 0)),
            pl.BlockSpec((SUBLANES, ROUTE_W), lambda p, t: (0, 0)),
        ],
        out_shape=[
            jax.ShapeDtypeStruct((n, ROUTE_W), F32),
            jax.ShapeDtypeStruct((SUBLANES, ROUTE_W), F32),
        ],
        scratch_shapes=[
            pltpu.VMEM((SUBLANES, ROUTE_W), F32),
            pltpu.VMEM((SUBLANES, ROUTE_W), F32),
            pltpu.VMEM((SUBLANES, ROUTE_W), F32),
        ],
        compiler_params=pltpu.CompilerParams(
            dimension_semantics=("arbitrary", "arbitrary"),
            vmem_limit_bytes=VMEM_LIMIT),
        name="route",
    )(logits)


def _row_copy(src_ref, src_row, dst_ref, dst_row, sem):
    return pltpu.make_async_copy(src_ref.at[pl.ds(src_row, 1)], dst_ref.at[pl.ds(dst_row, 1)], sem)


ZERO_ROWS = TM_EXP // 2


def _pad_fill(e, ps_ref, pl_ref, zbuf_ref, xs_ref, zsem, wait):
    ln = pl_ref[e]
    st = ps_ref[e]

    def run(cp):
        if wait:
            cp.wait()
        else:
            cp.start()

    for j in range(SUBLANES - 1):
        @pl.when(j < (ln & (SUBLANES - 1)))
        def _(j=j):
            run(_row_copy(zbuf_ref, 0, xs_ref, st + j, zsem))

    b = SUBLANES
    while b <= ZERO_ROWS:
        @pl.when((ln & b) != 0)
        def _(b=b):
            off = pl.multiple_of(st + (ln & (b - 1)), SUBLANES)
            run(pltpu.make_async_copy(zbuf_ref.at[pl.ds(0, b)], xs_ref.at[pl.ds(off, b)], zsem))
        b *= 2


def _dispatch_kernel(pos0_ref, pos1_ref, ps_ref, pl_ref, h1_ref, xs_ref, zbuf_ref, sem, zsem):
    n = h1_ref.shape[0]
    unroll = 8
    zbuf_ref[...] = jnp.zeros_like(zbuf_ref)

    def fill_start(e, _):
        _pad_fill(e, ps_ref, pl_ref, zbuf_ref, xs_ref, zsem, False)
        return 0

    lax.fori_loop(0, N_EXPERTS, fill_start, 0)

    def issue(i, _):
        for u in range(unroll):
            tok = i * unroll + u
            _row_copy(h1_ref, tok, xs_ref, pos0_ref[tok], sem).start()
            _row_copy(h1_ref, tok, xs_ref, pos1_ref[tok], sem).start()
        return 0

    lax.fori_loop(0, n // unroll, issue, 0)

    def drain(i, _):
        for u in range(unroll):
            _row_copy(h1_ref, 0, xs_ref, 0, sem).wait()
            _row_copy(h1_ref, 0, xs_ref, 0, sem).wait()
        return 0

    lax.fori_loop(0, n // unroll, drain, 0)

    def fill_wait(e, _):
        _pad_fill(e, ps_ref, pl_ref, zbuf_ref, xs_ref, zsem, True)
        return 0

    lax.fori_loop(0, N_EXPERTS, fill_wait, 0)


def _dispatch(pos0, pos1, pad_start, pad_len, h1, n_rows):
    return pl.pallas_call(
        _dispatch_kernel,
        grid_spec=pltpu.PrefetchScalarGridSpec(
            num_scalar_prefetch=4,
            grid=(1,),
            in_specs=[pl.BlockSpec(memory_space=pl.ANY)],
            out_specs=pl.BlockSpec(memory_space=pl.ANY),
            scratch_shapes=[
                pltpu.VMEM((ZERO_ROWS, D_MODEL), F32),
                pltpu.SemaphoreType.DMA(()),
                pltpu.SemaphoreType.DMA(()),
            ],
        ),
        out_shape=jax.ShapeDtypeStruct((n_rows, D_MODEL), F32),
        compiler_params=pltpu.CompilerParams(
            dimension_semantics=("arbitrary",),
            vmem_limit_bytes=VMEM_LIMIT),
        name="dispatch",
    )(pos0, pos1, pad_start, pad_len, h1)


def _expert_kernel(te_ref, nu_ref, xs_ref, wg_ref, wu_ref, wd_ref, ys_ref, wgb_ref, wub_ref, wdb_ref):
    i = pl.program_id(0)
    used = i < nu_ref[0]
    prev = te_ref[jnp.maximum(i - 1, 0)]
    fresh = (i == 0) | (te_ref[i] != prev)

    @pl.when(used & fresh)
    def _():
        wgb_ref[...] = wg_ref[0].astype(BF16)
        wub_ref[...] = wu_ref[0].astype(BF16)
        wdb_ref[...] = wd_ref[0].astype(BF16)

    @pl.when(used)
    def _():
        x = xs_ref[...].astype(BF16)
        g = jnp.dot(x, wgb_ref[...], preferred_element_type=F32)
        u = jnp.dot(x, wub_ref[...], preferred_element_type=F32)
        h = (g * jax.nn.sigmoid(g) * u).astype(BF16)
        ys_ref[...] = jnp.dot(h, wdb_ref[...], preferred_element_type=F32)


def _expert_ffn(tile_expert, n_used, xs, wg, wu, wd):
    n_rows = xs.shape[0]
    n_tiles = n_rows // TM_EXP

    def row_map(i, te, nu):
        return (jnp.minimum(i, nu[0] - 1), 0)

    def w_map(i, te, nu):
        return (te[i], 0, 0)

    return pl.pallas_call(
        _expert_kernel,
        grid_spec=pltpu.PrefetchScalarGridSpec(
            num_scalar_prefetch=2,
            grid=(n_tiles,),
            in_specs=[
                pl.BlockSpec((TM_EXP, D_MODEL), row_map),
                pl.BlockSpec((1, D_MODEL, D_EXPERT), w_map),
                pl.BlockSpec((1, D_MODEL, D_EXPERT), w_map),
                pl.BlockSpec((1, D_EXPERT, D_MODEL), w_map),
            ],
            out_specs=pl.BlockSpec((TM_EXP, D_MODEL), row_map),
            scratch_shapes=[
                pltpu.VMEM((D_MODEL, D_EXPERT), BF16),
                pltpu.VMEM((D_MODEL, D_EXPERT), BF16),
                pltpu.VMEM((D_EXPERT, D_MODEL), BF16),
            ],
        ),
        out_shape=jax.ShapeDtypeStruct((n_rows, D_MODEL), F32),
        compiler_params=pltpu.CompilerParams(
            dimension_semantics=("arbitrary",),
            vmem_limit_bytes=VMEM_LIMIT),
        name="expert_ffn",
    )(tile_expert, n_used, xs, wg, wu, wd)


def _combine_kernel(pos0_ref, pos1_ref, h1_ref, info_ref, g2_ref, b2_ref, ys_ref, o_ref, ybuf_ref, sem):
    i = pl.program_id(0)
    tm = TM_COMB
    unroll = 8

    def issue(j, _):
        for u in range(unroll):
            r = j * unroll + u
            tok = i * tm + r
            _row_copy(ys_ref, pos0_ref[tok], ybuf_ref.at[0], r, sem).start()
            _row_copy(ys_ref, pos1_ref[tok], ybuf_ref.at[1], r, sem).start()
        return 0

    lax.fori_loop(0, tm // unroll, issue, 0)

    def drain(j, _):
        for u in range(unroll):
            _row_copy(ys_ref, 0, ybuf_ref.at[0], 0, sem).wait()
            _row_copy(ys_ref, 0, ybuf_ref.at[1], 0, sem).wait()
        return 0

    lax.fori_loop(0, tm // unroll, drain, 0)

    rows = 64

    def body(k, _):
        rs = pl.ds(pl.multiple_of(k * rows, rows), rows)
        q1 = info_ref[rs, 2:3]
        q2 = info_ref[rs, 3:4]
        y = DN_ALPHA * h1_ref[rs, :] + q1 * ybuf_ref[0, rs, :] + q2 * ybuf_ref[1, rs, :]
        o_ref[rs, :] = _ln_rows(y, g2_ref[...], b2_ref[...])
        return 0

    lax.fori_loop(0, tm // rows, body, 0)


def _combine(pos0, pos1, h1, info, g2, b2, ys):
    n = h1.shape[0]
    return pl.pallas_call(
        _combine_kernel,
        grid_spec=pltpu.PrefetchScalarGridSpec(
            num_scalar_prefetch=2,
            grid=(n // TM_COMB,),
            in_specs=[
                pl.BlockSpec((TM_COMB, D_MODEL), lambda i, p0, p1: (i, 0)),
                pl.BlockSpec((TM_COMB, ROUTE_W), lambda i, p0, p1: (i, 0)),
                pl.BlockSpec((1, D_MODEL), lambda i, p0, p1: (0, 0)),
                pl.BlockSpec((1, D_MODEL), lambda i, p0, p1: (0, 0)),
                pl.BlockSpec(memory_space=pl.ANY),
            ],
            out_specs=pl.BlockSpec((TM_COMB, D_MODEL), lambda i, p0, p1: (i, 0)),
            scratch_shapes=[
                pltpu.VMEM((2, TM_COMB, D_MODEL), F32),
                pltpu.SemaphoreType.DMA(()),
            ],
        ),
        out_shape=jax.ShapeDtypeStruct((n, D_MODEL), F32),
        compiler_params=pltpu.CompilerParams(
            dimension_semantics=("arbitrary",),
            vmem_limit_bytes=VMEM_LIMIT),
        name="combine",
    )(pos0, pos1, h1, info, g2, b2, ys)


def _block_diag(w, per):
    h, hd, _ = w.shape
    wg = w.reshape(h // per, per, hd, hd)
    eye = jnp.eye(per, dtype=w.dtype)
    return jnp.einsum("gpij,pq->gpiqj", wg, eye).reshape(h // per, per * hd, per * hd)


def kernel(x, ln_in_g, ln_in_b, w_in, lru_conv_w, lru_conv_b, lru_w_a, lru_b_a, lru_w_x, lru_b_x,
           lru_lambda, conf_conv_w, conf_conv_b, conf_ln_g, conf_ln_b, w_out, ln1_g, ln1_b,
           router_group_w, router_group_b, router_expert_w, router_expert_b, exp_w_gate, exp_w_up,
           exp_w_down, ln2_g, ln2_b):
    bsz, seq, d = x.shape
    n = bsz * seq
    x2 = x.reshape(n, d)
    row = lambda v: v.reshape(1, -1).astype(F32)
    l = 0

    z = _ln_win(x2, row(ln_in_g), row(ln_in_b), w_in[l].astype(BF16))

    per = CB_LRU // LRU_HEAD_DIM
    wcat = jnp.concatenate([_block_diag(lru_w_a[l], per), _block_diag(lru_w_x[l], per)],
                           axis=-1).astype(BF16)
    a_out = _lru_mixer(z, lru_conv_w[l], row(lru_conv_b[l]), wcat, row(lru_b_a[l]),
                       row(lru_b_x[l]), row(lru_lambda[l]), bsz, seq)

    nlb = D_CONV // LANES
    w3 = jnp.pad(conf_conv_w[l], ((0, 32 - CONF_CONV_W), (0, 0)))
    w3 = w3.reshape(32, nlb, LANES).transpose(1, 0, 2)
    cb3 = conf_conv_b[l].reshape(nlb, 1, LANES)
    b_out = _conf_mixer(z, w3, cb3, row(conf_ln_g[l]), row(conf_ln_b[l]), bsz, seq)

    wr = jnp.concatenate([router_group_w[l], router_expert_w[l]], axis=1)
    wr = jnp.pad(wr, ((0, 0), (0, ROUTE_W - wr.shape[1])))
    wr_hi = wr.astype(BF16)
    wr_lo = (wr - wr_hi.astype(F32)).astype(BF16)
    br = jnp.concatenate([router_group_b[l], router_expert_b[l]])
    br = jnp.pad(br, (0, ROUTE_W - br.shape[0])).reshape(1, ROUTE_W)
    wo = w_out[l].astype(BF16)
    h1, logits = _wout_router(a_out, b_out, x2, row(ln_in_g), row(ln_in_b), wo[:D_LRU], wo[D_LRU:],
                              row(ln1_g[l]), row(ln1_b[l]), wr_hi, wr_lo, br)

    info, counts = _route(logits)
    pos0 = info[:, 0].astype(jnp.int32)
    pos1 = info[:, 1].astype(jnp.int32)

    n_tiles = (n * 2) // TM_EXP + N_EXPERTS
    cnt = counts[0, N_GROUPS:N_GROUPS + N_EXPERTS].astype(jnp.int32)
    tiles_per = (cnt + TM_EXP - 1) // TM_EXP
    ends = jnp.cumsum(tiles_per)
    n_used = ends[-1:].astype(jnp.int32)
    tile_ids = jnp.arange(n_tiles, dtype=jnp.int32)
    tile_expert = jnp.minimum(jnp.sum(tile_ids[:, None] >= ends[None, :], axis=1),
                              N_EXPERTS - 1).astype(jnp.int32)
    last = tile_expert[jnp.maximum(n_used[0] - 1, 0)]
    tile_expert = jnp.where(tile_ids < n_used[0], tile_expert, last)

    starts = (ends - tiles_per) * TM_EXP
    pad_start = (starts + cnt).astype(jnp.int32)
    pad_len = (tiles_per * TM_EXP - cnt).astype(jnp.int32)

    xs = _dispatch(pos0, pos1, pad_start, pad_len, h1, n_tiles * TM_EXP)
    shp = (N_EXPERTS, D_MODEL, D_EXPERT)
    ys = _expert_ffn(tile_expert, n_used, xs, exp_w_gate[l].reshape(shp), exp_w_up[l].reshape(shp),
                     exp_w_down[l].reshape(N_EXPERTS, D_EXPERT, D_MODEL))
    out = _combine(pos0, pos1, h1, info, row(ln2_g[l]), row(ln2_b[l]), ys)
    return out.reshape(bsz, seq, d)
```

```python
import functools
import math

import jax
import jax.numpy as jnp
from jax import lax
from jax.experimental import pallas as pl
from jax.experimental.pallas import tpu as pltpu

F32 = jnp.float32
BF16 = jnp.bfloat16

D_MODEL = 2048
D_LRU = 1024
D_CONV = 1024
LRU_HEADS = 16
LRU_HEAD_DIM = 64
LRU_C = 8.0
LRU_CONV_W = 4
CONF_CONV_W = 31
N_GROUPS = 4
EXPERTS_PER_GROUP = 8
N_EXPERTS = N_GROUPS * EXPERTS_PER_GROUP
D_EXPERT = 512
LN_EPS = 1e-5
DEPTH = 1
DN_ALPHA = (2 * DEPTH) ** 0.25

LANES = 128
SUBLANES = 8
VMEM_LIMIT = 56 * 1024 * 1024

TM_WIN = 1024
TN_WIN = 1024
TT_LRU = 512
CB_LRU = 256
TT_CONF = 256
CONF_HALO = 32
TM_OUT = 512
TM_ROUTE = 512
TM_DISP = 256
TM_EXP = 256
TM_COMB = 256
ROUTE_W = 128


def _ln_rows(x, g, b):
    mu = jnp.mean(x, axis=-1, keepdims=True)
    xc = x - mu
    var = jnp.mean(xc * xc, axis=-1, keepdims=True)
    return xc * lax.rsqrt(var + LN_EPS) * g + b


def _ln_win_kernel(x_ref, g_ref, b_ref, w_ref, z_ref, xn_ref):
    rows = 128

    @pl.when(pl.program_id(1) == 0)
    def _():
        def body(i, _):
            rs = pl.ds(pl.multiple_of(i * rows, rows), rows)
            xn_ref[rs, :] = _ln_rows(x_ref[rs, :], g_ref[...], b_ref[...]).astype(BF16)
            return 0
        lax.fori_loop(0, TM_WIN // rows, body, 0)

    z_ref[...] = jnp.dot(xn_ref[...], w_ref[...],
                         preferred_element_type=F32).astype(z_ref.dtype)


def _ln_win(x2, g, b, w_bf16):
    n = x2.shape[0]
    ncol = w_bf16.shape[1]
    return pl.pallas_call(
        _ln_win_kernel,
        grid=(n // TM_WIN, ncol // TN_WIN),
        in_specs=[
            pl.BlockSpec((TM_WIN, D_MODEL), lambda i, j: (i, 0)),
            pl.BlockSpec((1, D_MODEL), lambda i, j: (0, 0)),
            pl.BlockSpec((1, D_MODEL), lambda i, j: (0, 0)),
            pl.BlockSpec((D_MODEL, TN_WIN), lambda i, j: (0, j)),
        ],
        out_specs=pl.BlockSpec((TM_WIN, TN_WIN), lambda i, j: (i, j)),
        out_shape=jax.ShapeDtypeStruct((n, ncol), BF16),
        scratch_shapes=[pltpu.VMEM((TM_WIN, D_MODEL), BF16)],
        compiler_params=pltpu.CompilerParams(
            dimension_semantics=("arbitrary", "arbitrary"),
            vmem_limit_bytes=VMEM_LIMIT),
        name="ln_win",
    )(x2, g, b, w_bf16)


def _lru_kernel(zx_ref, zg_ref, cw_ref, cb_ref, wcat_ref, ba_ref, bx_ref, lam_ref,
                o_ref, xs_ref, hp_ref, a_ref, g_ref):
    t = pl.program_id(2)
    tt = TT_LRU

    @pl.when(t == 0)
    def _():
        xs_ref[0:SUBLANES, :] = jnp.zeros((SUBLANES, CB_LRU), F32)
        hp_ref[...] = jnp.zeros_like(hp_ref)

    @pl.when(t > 0)
    def _():
        xs_ref[0:SUBLANES, :] = xs_ref[tt:tt + SUBLANES, :]

    xs_ref[SUBLANES:SUBLANES + tt, :] = zx_ref[...].astype(F32)

    rows = 128
    for rb in range(tt // rows):
        acc = jnp.broadcast_to(cb_ref[...], (rows, CB_LRU))
        for k in range(LRU_CONV_W):
            off = rb * rows + SUBLANES - (LRU_CONV_W - 1) + k
            acc = acc + cw_ref[k:k + 1, :] * xs_ref[off:off + rows, :]
        a_ref[rb * rows:(rb + 1) * rows, :] = acc

    g_ref[...] = jnp.dot(a_ref[...].astype(BF16), wcat_ref[0], preferred_element_type=F32)

    lam = lam_ref[...]
    softplus_neg = jnp.maximum(-lam, 0.0) + jnp.log1p(jnp.exp(-jnp.abs(lam)))
    cvec = -LRU_C * softplus_neg
    ba = ba_ref[...]
    bx = bx_ref[...]
    blk = 64
    row_in_vreg = lax.broadcasted_iota(jnp.int32, (blk, CB_LRU), 0) & (SUBLANES - 1)

    def body(rb, h):
        rs = pl.ds(pl.multiple_of(rb * blk, blk), blk)
        a_in = a_ref[rs, :]
        r = jax.nn.sigmoid(g_ref[rs, 0:CB_LRU] + ba)
        i = jax.nn.sigmoid(g_ref[rs, CB_LRU:2 * CB_LRU] + bx)
        log_a = cvec * r
        a = jnp.exp(log_a)
        u = jnp.sqrt(-jnp.tanh(log_a) * (a * a + 1.0)) * (i * a_in)
        for s in (1, 2, 4):
            m = row_in_vreg >= s
            a_sh = jnp.where(m, pltpu.roll(a, s, 0), 1.0)
            u_sh = jnp.where(m, pltpu.roll(u, s, 0), 0.0)
            u = u + a * u_sh
            a = a * a_sh
        outs = []
        for gi in range(blk // SUBLANES):
            ag = a[gi * SUBLANES:(gi + 1) * SUBLANES, :]
            ug = u[gi * SUBLANES:(gi + 1) * SUBLANES, :]
            hg = ug + ag * h
            h = hg[SUBLANES - 1:SUBLANES, :]
            outs.append(hg)
        hblk = jnp.concatenate(outs, axis=0)
        gl = zg_ref[rs, :].astype(F32)
        gelu = 0.5 * gl * (1.0 + jnp.tanh(0.7978845608028654 * (gl + 0.044715 * gl * gl * gl)))
        o_ref[rs, :] = (gelu * hblk).astype(o_ref.dtype)
        return h

    h = lax.fori_loop(0, tt // blk, body, hp_ref[0:1, :])
    hp_ref[...] = jnp.broadcast_to(h, hp_ref.shape)


def _lru_mixer(z, cw, cb, wcat, ba, bx, lam, bsz, seq):
    n = z.shape[0]
    nt = seq // TT_LRU
    ncb = D_LRU // CB_LRU
    row = lambda b, j, t: b * nt + t
    vec = pl.BlockSpec((1, CB_LRU), lambda b, j, t: (0, j))
    return pl.pallas_call(
        _lru_kernel,
        grid=(bsz, ncb, nt),
        in_specs=[
            pl.BlockSpec((TT_LRU, CB_LRU), lambda b, j, t: (row(b, j, t), j)),
            pl.BlockSpec((TT_LRU, CB_LRU), lambda b, j, t: (row(b, j, t), ncb + j)),
            pl.BlockSpec((LRU_CONV_W, CB_LRU), lambda b, j, t: (0, j)),
            vec,
            pl.BlockSpec((1, CB_LRU, 2 * CB_LRU), lambda b, j, t: (j, 0, 0)),
            vec, vec, vec,
        ],
        out_specs=pl.BlockSpec((TT_LRU, CB_LRU), lambda b, j, t: (row(b, j, t), j)),
        out_shape=jax.ShapeDtypeStruct((n, D_LRU), BF16),
        scratch_shapes=[
            pltpu.VMEM((TT_LRU + SUBLANES, CB_LRU), F32),
            pltpu.VMEM((SUBLANES, CB_LRU), F32),
            pltpu.VMEM((TT_LRU, CB_LRU), F32),
            pltpu.VMEM((TT_LRU, 2 * CB_LRU), F32),
        ],
        compiler_params=pltpu.CompilerParams(
            dimension_semantics=("arbitrary", "arbitrary", "arbitrary"),
            vmem_limit_bytes=VMEM_LIMIT),
        name="lru_mixer",
    )(z, z, cw, cb, wcat, ba, bx, lam)


def _conf_kernel(zv_ref, zg_ref, w_ref, cb_ref, lg_ref, lb_ref, o_ref, cs_ref, cv_ref):
    t = pl.program_id(1)
    tt = TT_CONF
    nlb = D_CONV // LANES

    @pl.when(t == 0)
    def _():
        cs_ref[:, 0:CONF_HALO, :] = jnp.zeros((nlb, CONF_HALO, LANES), F32)

    @pl.when(t > 0)
    def _():
        cs_ref[:, 0:CONF_HALO, :] = cs_ref[:, tt:tt + CONF_HALO, :]

    for c in range(nlb):
        ls = slice(c * LANES, (c + 1) * LANES)
        v = zv_ref[:, ls].astype(F32)
        g = zg_ref[:, ls].astype(F32)
        cs_ref[c, CONF_HALO:CONF_HALO + tt, :] = v * jax.nn.sigmoid(g)

    rows = 64
    nrb = tt // rows
    base = CONF_HALO - (CONF_CONV_W - 1)

    def conv_body(c, carry):
        accs = [jnp.broadcast_to(cb_ref[c], (rows, LANES)) for _ in range(nrb)]
        for k in range(CONF_CONV_W):
            wk = w_ref[c, k:k + 1, :]
            for rb in range(nrb):
                off = rb * rows + base + k
                accs[rb] = accs[rb] + wk * cs_ref[c, off:off + rows, :]
        for rb in range(nrb):
            cv_ref[c, rb * rows:(rb + 1) * rows, :] = accs[rb]
        return carry

    lax.fori_loop(0, nlb, conv_body, 0)

    ln_rows = 32
    inv_n = 1.0 / D_CONV
    for rb in range(tt // ln_rows):
        rs = slice(rb * ln_rows, (rb + 1) * ln_rows)
        blk = cv_ref[:, rs, :]
        mu = jnp.sum(jnp.sum(blk, axis=0), axis=-1, keepdims=True) * inv_n
        d = blk - mu[None]
        var = jnp.sum(jnp.sum(d * d, axis=0), axis=-1, keepdims=True) * inv_n
        inv = lax.rsqrt(var + LN_EPS)
        for c in range(nlb):
            ls = slice(c * LANES, (c + 1) * LANES)
            y = d[c] * inv * lg_ref[:, ls] + lb_ref[:, ls]
            o_ref[rs, ls] = (y * jax.nn.sigmoid(y)).astype(o_ref.dtype)


def _conf_mixer(z, w3, cb3, lg, lb, bsz, seq):
    n = z.shape[0]
    nt = seq // TT_CONF
    nlb = D_CONV // LANES
    return pl.pallas_call(
        _conf_kernel,
        grid=(bsz, nt),
        in_specs=[
            pl.BlockSpec((TT_CONF, D_CONV), lambda b, t: (b * nt + t, 2)),
            pl.BlockSpec((TT_CONF, D_CONV), lambda b, t: (b * nt + t, 3)),
            pl.BlockSpec((nlb, 32, LANES), lambda b, t: (0, 0, 0)),
            pl.BlockSpec((nlb, 1, LANES), lambda b, t: (0, 0, 0)),
            pl.BlockSpec((1, D_CONV), lambda b, t: (0, 0)),
            pl.BlockSpec((1, D_CONV), lambda b, t: (0, 0)),
        ],
        out_specs=pl.BlockSpec((TT_CONF, D_CONV), lambda b, t: (b * nt + t, 0)),
        out_shape=jax.ShapeDtypeStruct((n, D_CONV), BF16),
        scratch_shapes=[
            pltpu.VMEM((nlb, CONF_HALO + TT_CONF, LANES), F32),
            pltpu.VMEM((nlb, TT_CONF, LANES), F32),
        ],
        compiler_params=pltpu.CompilerParams(
            dimension_semantics=("arbitrary", "arbitrary"),
            vmem_limit_bytes=VMEM_LIMIT),
        name="conf_mixer",
    )(z, z, w3, cb3, lg, lb)


def _split_bf16(v):
    hi = v.astype(BF16)
    lo = (v - hi.astype(F32)).astype(BF16)
    return hi, lo


RT = D_MODEL // LANES


def _store_rows(dst_ref, row0, rows, v):
    for s in range(RT):
        dst_ref[pl.ds(row0 * RT + s, rows, stride=RT), :] = v[:, s * LANES:(s + 1) * LANES]


def _load_rows(src_ref, row0, rows):
    return [src_ref[pl.ds(row0 * RT + s, rows, stride=RT), :] for s in range(RT)]


def _wout_kernel(a_ref, b_ref, x_ref, gin_ref, bin_ref, wa_ref, wb_ref, g1_ref, b1_ref,
                 wr_ref, br_ref, h1r_ref, lg_ref, mix_ref, hl_ref):
    mix_ref[...] = (jnp.dot(a_ref[...], wa_ref[...], preferred_element_type=F32)
                    + jnp.dot(b_ref[...], wb_ref[...], preferred_element_type=F32))
    rows = 64
    for c in range(TM_OUT // rows):
        rs = slice(c * rows, (c + 1) * rows)
        h = _ln_rows(x_ref[rs, :], gin_ref[...], bin_ref[...])
        h1 = _ln_rows(DN_ALPHA * h + mix_ref[rs, :], g1_ref[...], b1_ref[...])
        _store_rows(h1r_ref, c * rows, rows, h1)
        hi, lo = _split_bf16(h1)
        hl_ref[rs, :] = hi
        hl_ref[TM_OUT + c * rows:TM_OUT + (c + 1) * rows, :] = lo
    p = jnp.dot(hl_ref[...], wr_ref[...], preferred_element_type=F32)
    lg_ref[...] = (p[0:TM_OUT, 0:ROUTE_W] + p[0:TM_OUT, ROUTE_W:2 * ROUTE_W]
                   + p[TM_OUT:2 * TM_OUT, 0:ROUTE_W] + br_ref[...])


def _wout_router(a, b, x2, gin, bin_, wa, wb, g1, b1, wr_cat, br):
    n = x2.shape[0]
    full = lambda shape: pl.BlockSpec(shape, lambda i: tuple(0 for _ in shape))
    return pl.pallas_call(
        _wout_kernel,
        grid=(n // TM_OUT,),
        in_specs=[
            pl.BlockSpec((TM_OUT, D_LRU), lambda i: (i, 0)),
            pl.BlockSpec((TM_OUT, D_CONV), lambda i: (i, 0)),
            pl.BlockSpec((TM_OUT, D_MODEL), lambda i: (i, 0)),
            full((1, D_MODEL)), full((1, D_MODEL)),
            full((D_LRU, D_MODEL)), full((D_CONV, D_MODEL)),
            full((1, D_MODEL)), full((1, D_MODEL)),
            full((D_MODEL, 2 * ROUTE_W)), full((1, ROUTE_W)),
        ],
        out_specs=[
            pl.BlockSpec((TM_OUT * RT, LANES), lambda i: (i, 0)),
            pl.BlockSpec((TM_OUT, ROUTE_W), lambda i: (i, 0)),
        ],
        out_shape=[
            jax.ShapeDtypeStruct((n * RT, LANES), F32),
            jax.ShapeDtypeStruct((n, ROUTE_W), F32),
        ],
        scratch_shapes=[
            pltpu.VMEM((TM_OUT, D_MODEL), F32),
            pltpu.VMEM((2 * TM_OUT, D_MODEL), BF16),
        ],
        compiler_params=pltpu.CompilerParams(
            dimension_semantics=("arbitrary",),
            vmem_limit_bytes=VMEM_LIMIT),
        name="wout_router",
    )(a, b, x2, gin, bin_, wa, wb, g1, b1, wr_cat, br)


def _route_kernel(lg_ref, info_ref, cnt_ref, tot_ref, run_ref, off_ref):
    phase = pl.program_id(0)
    t = pl.program_id(1)
    tm = TM_ROUTE
    l = lg_ref[...]
    lane = lax.broadcasted_iota(jnp.int32, (tm, ROUTE_W), 1)
    neg = jnp.float32(-jnp.inf)
    big = jnp.int32(1 << 20)

    gmask = lane < N_GROUPS
    gmax = jnp.max(jnp.where(gmask, l, neg), axis=-1, keepdims=True)
    gsel = jnp.min(jnp.where(gmask & (l == gmax), lane, big), axis=-1, keepdims=True)
    gsum = jnp.sum(jnp.where(gmask, jnp.exp(l - gmax), 0.0), axis=-1, keepdims=True)
    pg_top = 1.0 / gsum

    lo = N_GROUPS + EXPERTS_PER_GROUP * gsel
    emask = (lane >= lo) & (lane < lo + EXPERTS_PER_GROUP)
    v1 = jnp.max(jnp.where(emask, l, neg), axis=-1, keepdims=True)
    i1 = jnp.min(jnp.where(emask & (l == v1), lane, big), axis=-1, keepdims=True)
    emask2 = emask & (lane != i1)
    v2 = jnp.max(jnp.where(emask2, l, neg), axis=-1, keepdims=True)
    i2 = jnp.min(jnp.where(emask2 & (l == v2), lane, big), axis=-1, keepdims=True)
    e21 = jnp.exp(v2 - v1)
    q1 = pg_top / (1.0 + e21)
    q2 = pg_top * e21 / (1.0 + e21)

    oh1 = (lane == i1).astype(F32)
    oh2 = (lane == i2).astype(F32)
    ohs = oh1 + oh2

    @pl.when((phase == 0) & (t == 0))
    def _():
        tot_ref[...] = jnp.zeros_like(tot_ref)

    @pl.when(phase == 0)
    def _():
        tot_ref[...] = tot_ref[...] + jnp.sum(ohs, axis=0, keepdims=True)

    @pl.when((phase == 1) & (t == 0))
    def _():
        cnt = tot_ref[...].astype(jnp.int32)
        shift = TM_EXP.bit_length() - 1
        padded = jnp.left_shift(jnp.right_shift(cnt + (TM_EXP - 1), shift), shift)
        lane1 = lax.broadcasted_iota(jnp.int32, tot_ref.shape, 1)
        padded = jnp.where((lane1 >= N_GROUPS) & (lane1 < N_GROUPS + N_EXPERTS), padded, 0)
        inc = padded.astype(F32)
        s = 1
        while s < ROUTE_W:
            inc = inc + jnp.where(lane1 >= s, pltpu.roll(inc, s, 1), 0.0)
            s *= 2
        off_ref[...] = inc - padded.astype(F32)
        run_ref[...] = jnp.zeros_like(run_ref)

    @pl.when(phase == 1)
    def _():
        r_i = lax.broadcasted_iota(jnp.int32, (tm, tm), 0)
        c_i = lax.broadcasted_iota(jnp.int32, (tm, tm), 1)
        lower = (c_i < r_i).astype(BF16)
        cum = jnp.dot(lower, ohs.astype(BF16), preferred_element_type=F32)
        basev = off_ref[0:1, :] + run_ref[0:1, :] + cum
        p1 = jnp.sum(oh1 * basev, axis=-1, keepdims=True)
        p2 = jnp.sum(oh2 * basev, axis=-1, keepdims=True)
        run_ref[...] = run_ref[...] + jnp.sum(ohs, axis=0, keepdims=True)
        info = jnp.where(lane == 0, p1, 0.0)
        info = jnp.where(lane == 1, p2, info)
        info = jnp.where(lane == 2, q1, info)
        info = jnp.where(lane == 3, q2, info)
        info_ref[...] = info

    cnt_ref[...] = tot_ref[...]


def _route(logits):
    n = logits.shape[0]
    nt = n // TM_ROUTE
    return pl.pallas_call(
        _route_kernel,
        grid=(2, nt),
        in_specs=[pl.BlockSpec((TM_ROUTE, ROUTE_W), lambda p, t: (t, 0))],
        out_specs=[
            pl.BlockSpec((TM_ROUTE, ROUTE_W), lambda p, t: (p * t, 0)),
            pl.BlockSpec((SUBLANES, ROUTE_W), lambda p, t: (0, 0)),
        ],
        out_shape=[
            jax.ShapeDtypeStruct((n, ROUTE_W), F32),
            jax.ShapeDtypeStruct((SUBLANES, ROUTE_W), F32),
        ],
        scratch_shapes=[
            pltpu.VMEM((SUBLANES, ROUTE_W), F32),
            pltpu.VMEM((SUBLANES, ROUTE_W), F32),
            pltpu.VMEM((SUBLANES, ROUTE_W), F32),
        ],
        compiler_params=pltpu.CompilerParams(
            dimension_semantics=("arbitrary", "arbitrary"),
            vmem_limit_bytes=VMEM_LIMIT),
        name="route",
    )(logits)


def _prow_copy(src_ref, src_row, dst_ref, dst_row, sem, rows=1):
    src = src_ref.at[pl.ds(pl.multiple_of(src_row * RT, RT), rows * RT)]
    dst = dst_ref.at[pl.ds(pl.multiple_of(dst_row * RT, RT), rows * RT)]
    return pltpu.make_async_copy(src, dst, sem)


ZERO_ROWS = TM_EXP // 2
DMA_UNROLL = 8


def _pad_fill(e, ps_ref, pl_ref, zbuf_ref, xs_ref, zsem, wait):
    ln = pl_ref[e]
    st = ps_ref[e]
    b = 1
    while b <= ZERO_ROWS:
        @pl.when((ln & b) != 0)
        def _(b=b):
            cp = _prow_copy(zbuf_ref, 0, xs_ref, st + (ln & (b - 1)), zsem, rows=b)
            if wait:
                cp.wait()
            else:
                cp.start()
        b *= 2


def _dispatch_kernel(pos0_ref, pos1_ref, ps_ref, pl_ref, h1p_ref, xs_ref, zbuf_ref, sem, zsem):
    i = pl.program_id(0)
    tm = TM_DISP

    @pl.when(i == 0)
    def _():
        zbuf_ref[...] = jnp.zeros_like(zbuf_ref)

        def fill_start(e, _):
            _pad_fill(e, ps_ref, pl_ref, zbuf_ref, xs_ref, zsem, False)
            return 0

        lax.fori_loop(0, N_EXPERTS, fill_start, 0)

    def issue(k, _):
        for u in range(DMA_UNROLL):
            r = k * DMA_UNROLL + u
            tok = i * tm + r
            _prow_copy(h1p_ref, r, xs_ref, pos0_ref[tok], sem).start()
            _prow_copy(h1p_ref, r, xs_ref, pos1_ref[tok], sem).start()
        return 0

    lax.fori_loop(0, tm // DMA_UNROLL, issue, 0)

    def drain(k, _):
        for u in range(DMA_UNROLL):
            _prow_copy(h1p_ref, 0, xs_ref, 0, sem).wait()
            _prow_copy(h1p_ref, 0, xs_ref, 0, sem).wait()
        return 0

    lax.fori_loop(0, tm // DMA_UNROLL, drain, 0)

    @pl.when(i == pl.num_programs(0) - 1)
    def _():
        def fill_wait(e, _):
            _pad_fill(e, ps_ref, pl_ref, zbuf_ref, xs_ref, zsem, True)
            return 0

        lax.fori_loop(0, N_EXPERTS, fill_wait, 0)


def _dispatch(pos0, pos1, pad_start, pad_len, h1p, n_rows):
    n = h1p.shape[0] // RT
    return pl.pallas_call(
        _dispatch_kernel,
        grid_spec=pltpu.PrefetchScalarGridSpec(
            num_scalar_prefetch=4,
            grid=(n // TM_DISP,),
            in_specs=[pl.BlockSpec((TM_DISP * RT, LANES), lambda i, *_: (i, 0))],
            out_specs=pl.BlockSpec(memory_space=pl.ANY),
            scratch_shapes=[
                pltpu.VMEM((ZERO_ROWS * RT, LANES), F32),
                pltpu.SemaphoreType.DMA(()),
                pltpu.SemaphoreType.DMA(()),
            ],
        ),
        out_shape=jax.ShapeDtypeStruct((n_rows * RT, LANES), F32),
        compiler_params=pltpu.CompilerParams(
            dimension_semantics=("arbitrary",),
            vmem_limit_bytes=VMEM_LIMIT),
        name="dispatch",
    )(pos0, pos1, pad_start, pad_len, h1p)


def _expert_kernel(te_ref, nu_ref, nx_ref, sl_ref, xs_ref, wg_hbm, wu_hbm, wd_hbm, ys_ref,
                   wgf_ref, wuf_ref, wdf_ref, wgb_ref, wub_ref, wdb_ref, wsem):
    i = pl.program_id(0)
    used = i < nu_ref[0]
    e = te_ref[i]
    s = sl_ref[i]
    fresh = (i == 0) | (e != te_ref[jnp.maximum(i - 1, 0)])

    def weight_copies(expert, slot):
        return (pltpu.make_async_copy(wg_hbm.at[expert], wgf_ref.at[slot], wsem.at[slot, 0]),
                pltpu.make_async_copy(wu_hbm.at[expert], wuf_ref.at[slot], wsem.at[slot, 1]),
                pltpu.make_async_copy(wd_hbm.at[expert], wdf_ref.at[slot], wsem.at[slot, 2]))

    @pl.when(i == 0)
    def _():
        for cp in weight_copies(e, s):
            cp.start()

    @pl.when(used & fresh)
    def _():
        for cp in weight_copies(e, s):
            cp.wait()

        @pl.when(nx_ref[i] >= 0)
        def _():
            for cp in weight_copies(nx_ref[i], 1 - s):
                cp.start()

        wgb_ref[...] = wgf_ref[s].astype(BF16)
        wub_ref[...] = wuf_ref[s].astype(BF16)
        wdb_ref[...] = wdf_ref[s].astype(BF16)

    @pl.when(used)
    def _():
        x = jnp.concatenate([p.astype(BF16) for p in _load_rows(xs_ref, 0, TM_EXP)], axis=1)
        g = jnp.dot(x, wgb_ref[...], preferred_element_type=F32)
        u = jnp.dot(x, wub_ref[...], preferred_element_type=F32)
        h = (g * jax.nn.sigmoid(g) * u).astype(BF16)
        y = jnp.dot(h, wdb_ref[...], preferred_element_type=F32)
        _store_rows(ys_ref, 0, TM_EXP, y)


def _expert_ffn(tile_expert, n_used, next_expert, slot, xs, wg, wu, wd):
    n_rows = xs.shape[0] // RT
    n_tiles = n_rows // TM_EXP

    def row_map(i, te, nu, nx, sl):
        return (jnp.minimum(i, nu[0] - 1), 0)

    return pl.pallas_call(
        _expert_kernel,
        grid_spec=pltpu.PrefetchScalarGridSpec(
            num_scalar_prefetch=4,
            grid=(n_tiles,),
            in_specs=[
                pl.BlockSpec((TM_EXP * RT, LANES), row_map),
                pl.BlockSpec(memory_space=pl.ANY),
                pl.BlockSpec(memory_space=pl.ANY),
                pl.BlockSpec(memory_space=pl.ANY),
            ],
            out_specs=pl.BlockSpec((TM_EXP * RT, LANES), row_map),
            scratch_shapes=[
                pltpu.VMEM((2, D_MODEL, D_EXPERT), F32),
                pltpu.VMEM((2, D_MODEL, D_EXPERT), F32),
                pltpu.VMEM((2, D_EXPERT, D_MODEL), F32),
                pltpu.VMEM((D_MODEL, D_EXPERT), BF16),
                pltpu.VMEM((D_MODEL, D_EXPERT), BF16),
                pltpu.VMEM((D_EXPERT, D_MODEL), BF16),
                pltpu.SemaphoreType.DMA((2, 3)),
            ],
        ),
        out_shape=jax.ShapeDtypeStruct((n_rows * RT, LANES), F32),
        compiler_params=pltpu.CompilerParams(
            dimension_semantics=("arbitrary",),
            vmem_limit_bytes=VMEM_LIMIT),
        name="expert_ffn",
    )(tile_expert, n_used, next_expert, slot, xs, wg, wu, wd)


def _combine_kernel(pos0_ref, pos1_ref, h1_ref, info_ref, g2_ref, b2_ref, ys_ref, o_ref, ybuf_ref, sem):
    i = pl.program_id(0)
    tm = TM_COMB

    def issue(k, _):
        for u in range(DMA_UNROLL):
            r = k * DMA_UNROLL + u
            tok = i * tm + r
            _prow_copy(ys_ref, pos0_ref[tok], ybuf_ref.at[0], r, sem).start()
            _prow_copy(ys_ref, pos1_ref[tok], ybuf_ref.at[1], r, sem).start()
        return 0

    lax.fori_loop(0, tm // DMA_UNROLL, issue, 0)

    def drain(k, _):
        for u in range(DMA_UNROLL):
            _prow_copy(ys_ref, 0, ybuf_ref.at[0], 0, sem).wait()
            _prow_copy(ys_ref, 0, ybuf_ref.at[1], 0, sem).wait()
        return 0

    lax.fori_loop(0, tm // DMA_UNROLL, drain, 0)

    rows = 64
    for c in range(tm // rows):
        rs = slice(c * rows, (c + 1) * rows)
        q1 = info_ref[rs, 2:3]
        q2 = info_ref[rs, 3:4]
        hh = _load_rows(h1_ref, c * rows, rows)
        y0 = _load_rows(ybuf_ref.at[0], c * rows, rows)
        y1 = _load_rows(ybuf_ref.at[1], c * rows, rows)
        y = jnp.concatenate([DN_ALPHA * h + (q1 * a + q2 * b) for h, a, b in zip(hh, y0, y1)], axis=1)
        o_ref[rs, :] = _ln_rows(y, g2_ref[...], b2_ref[...])


def _combine(pos0, pos1, h1, info, g2, b2, ys):
    n = h1.shape[0] // RT
    return pl.pallas_call(
        _combine_kernel,
        grid_spec=pltpu.PrefetchScalarGridSpec(
            num_scalar_prefetch=2,
            grid=(n // TM_COMB,),
            in_specs=[
                pl.BlockSpec((TM_COMB * RT, LANES), lambda i, p0, p1: (i, 0)),
                pl.BlockSpec((TM_COMB, ROUTE_W), lambda i, p0, p1: (i, 0)),
                pl.BlockSpec((1, D_MODEL), lambda i, p0, p1: (0, 0)),
                pl.BlockSpec((1, D_MODEL), lambda i, p0, p1: (0, 0)),
                pl.BlockSpec(memory_space=pl.ANY),
            ],
            out_specs=pl.BlockSpec((TM_COMB, D_MODEL), lambda i, p0, p1: (i, 0)),
            scratch_shapes=[
                pltpu.VMEM((2, TM_COMB * RT, LANES), F32),
                pltpu.SemaphoreType.DMA(()),
            ],
        ),
        out_shape=jax.ShapeDtypeStruct((n, D_MODEL), F32),
        compiler_params=pltpu.CompilerParams(
            dimension_semantics=("arbitrary",),
            vmem_limit_bytes=VMEM_LIMIT),
        name="combine",
    )(pos0, pos1, h1, info, g2, b2, ys)


def _block_diag(w, per):
    h, hd, _ = w.shape
    wg = w.reshape(h // per, per, hd, hd)
    eye = jnp.eye(per, dtype=w.dtype)
    return jnp.einsum("gpij,pq->gpiqj", wg, eye).reshape(h // per, per * hd, per * hd)


def kernel(x, ln_in_g, ln_in_b, w_in, lru_conv_w, lru_conv_b, lru_w_a, lru_b_a, lru_w_x, lru_b_x,
           lru_lambda, conf_conv_w, conf_conv_b, conf_ln_g, conf_ln_b, w_out, ln1_g, ln1_b,
           router_group_w, router_group_b, router_expert_w, router_expert_b, exp_w_gate, exp_w_up,
           exp_w_down, ln2_g, ln2_b):
    bsz, seq, d = x.shape
    n = bsz * seq
    x2 = x.reshape(n, d)
    row = lambda v: v.reshape(1, -1).astype(F32)
    l = 0

    z = _ln_win(x2, row(ln_in_g), row(ln_in_b), w_in[l].astype(BF16))

    per = CB_LRU // LRU_HEAD_DIM
    wcat = jnp.concatenate([_block_diag(lru_w_a[l], per), _block_diag(lru_w_x[l], per)],
                           axis=-1).astype(BF16)
    a_out = _lru_mixer(z, lru_conv_w[l], row(lru_conv_b[l]), wcat, row(lru_b_a[l]),
                       row(lru_b_x[l]), row(lru_lambda[l]), bsz, seq)

    nlb = D_CONV // LANES
    w3 = jnp.pad(conf_conv_w[l], ((0, 32 - CONF_CONV_W), (0, 0)))
    w3 = w3.reshape(32, nlb, LANES).transpose(1, 0, 2)
    cb3 = conf_conv_b[l].reshape(nlb, 1, LANES)
    b_out = _conf_mixer(z, w3, cb3, row(conf_ln_g[l]), row(conf_ln_b[l]), bsz, seq)

    wr = jnp.concatenate([router_group_w[l], router_expert_w[l]], axis=1)
    wr = jnp.pad(wr, ((0, 0), (0, ROUTE_W - wr.shape[1])))
    wr_hi = wr.astype(BF16)
    wr_lo = (wr - wr_hi.astype(F32)).astype(BF16)
    wr_cat = jnp.concatenate([wr_hi, wr_lo], axis=1)
    br = jnp.concatenate([router_group_b[l], router_expert_b[l]])
    br = jnp.pad(br, (0, ROUTE_W - br.shape[0])).reshape(1, ROUTE_W)
    wo = w_out[l].astype(BF16)
    h1p, logits = _wout_router(a_out, b_out, x2, row(ln_in_g), row(ln_in_b), wo[:D_LRU],
                                   wo[D_LRU:], row(ln1_g[l]), row(ln1_b[l]), wr_cat, br)

    info, counts = _route(logits)
    pos0 = info[:, 0].astype(jnp.int32)
    pos1 = info[:, 1].astype(jnp.int32)

    n_tiles = (n * 2) // TM_EXP + N_EXPERTS
    cnt = counts[0, N_GROUPS:N_GROUPS + N_EXPERTS].astype(jnp.int32)
    tiles_per = (cnt + TM_EXP - 1) // TM_EXP
    ends = jnp.cumsum(tiles_per)
    n_used = ends[-1:].astype(jnp.int32)
    tile_ids = jnp.arange(n_tiles, dtype=jnp.int32)
    tile_expert = jnp.minimum(jnp.sum(tile_ids[:, None] >= ends[None, :], axis=1),
                              N_EXPERTS - 1).astype(jnp.int32)
    last = tile_expert[jnp.maximum(n_used[0] - 1, 0)]
    tile_expert = jnp.where(tile_ids < n_used[0], tile_expert, last)

    nxt_tile = ends[tile_expert]
    next_expert = jnp.where(nxt_tile < n_used[0],
                            tile_expert[jnp.minimum(nxt_tile, n_tiles - 1)], -1).astype(jnp.int32)
    changes = jnp.concatenate([jnp.zeros((1,), jnp.int32),
                               (tile_expert[1:] != tile_expert[:-1]).astype(jnp.int32)])
    slot = (jnp.cumsum(changes) & 1).astype(jnp.int32)
    starts = (ends - tiles_per) * TM_EXP
    pad_start = (starts + cnt).astype(jnp.int32)
    pad_len = (tiles_per * TM_EXP - cnt).astype(jnp.int32)

    xs = _dispatch(pos0, pos1, pad_start, pad_len, h1p, n_tiles * TM_EXP)
    shp = (N_EXPERTS, D_MODEL, D_EXPERT)
    ys = _expert_ffn(tile_expert, n_used, next_expert, slot, xs, exp_w_gate[l].reshape(shp),
                     exp_w_up[l].reshape(shp), exp_w_down[l].reshape(N_EXPERTS, D_EXPERT, D_MODEL))
    out = _combine(pos0, pos1, h1p, info, row(ln2_g[l]), row(ln2_b[l]), ys)
    return out.reshape(bsz, seq, d)
```

```python
import functools
import math

import jax
import jax.numpy as jnp
from jax import lax
from jax.experimental import pallas as pl
from jax.experimental.pallas import tpu as pltpu

F32 = jnp.float32
BF16 = jnp.bfloat16

D_MODEL = 2048
D_LRU = 1024
D_CONV = 1024
LRU_HEADS = 16
LRU_HEAD_DIM = 64
LRU_C = 8.0
LRU_CONV_W = 4
CONF_CONV_W = 31
N_GROUPS = 4
EXPERTS_PER_GROUP = 8
N_EXPERTS = N_GROUPS * EXPERTS_PER_GROUP
D_EXPERT = 512
LN_EPS = 1e-5
DEPTH = 1
DN_ALPHA = (2 * DEPTH) ** 0.25

LANES = 128
SUBLANES = 8
VMEM_LIMIT = 56 * 1024 * 1024

TM_WIN = 1024
TN_WIN = 1024
TT_LRU = 512
CB_LRU = 256
TT_CONF = 256
CONF_HALO = 32
TM_OUT = 512
TM_ROUTE = 512
TM_DISP = 256
TM_EXP = 256
TM_COMB = 256
ROUTE_W = 128


def _sigmoid(x):
    return 0.5 * (jnp.tanh(0.5 * x) + 1.0)


def _ln_rows(x, g, b):
    mu = jnp.mean(x, axis=-1, keepdims=True)
    xc = x - mu
    var = jnp.mean(xc * xc, axis=-1, keepdims=True)
    return xc * lax.rsqrt(var + LN_EPS) * g + b


def _ln_win_kernel(x_ref, g_ref, b_ref, w_ref, z_ref, xn_ref):
    rows = 128

    @pl.when(pl.program_id(1) == 0)
    def _():
        def body(i, _):
            rs = pl.ds(pl.multiple_of(i * rows, rows), rows)
            xn_ref[rs, :] = _ln_rows(x_ref[rs, :], g_ref[...], b_ref[...]).astype(BF16)
            return 0
        lax.fori_loop(0, TM_WIN // rows, body, 0)

    z_ref[...] = jnp.dot(xn_ref[...], w_ref[...].astype(BF16),
                         preferred_element_type=F32).astype(z_ref.dtype)


def _ln_win(x2, g, b, w):
    n = x2.shape[0]
    ncol = w.shape[1]
    return pl.pallas_call(
        _ln_win_kernel,
        grid=(n // TM_WIN, ncol // TN_WIN),
        in_specs=[
            pl.BlockSpec((TM_WIN, D_MODEL), lambda i, j: (i, 0)),
            pl.BlockSpec((1, D_MODEL), lambda i, j: (0, 0)),
            pl.BlockSpec((1, D_MODEL), lambda i, j: (0, 0)),
            pl.BlockSpec((D_MODEL, TN_WIN), lambda i, j: (0, j)),
        ],
        out_specs=pl.BlockSpec((TM_WIN, TN_WIN), lambda i, j: (i, j)),
        out_shape=jax.ShapeDtypeStruct((n, ncol), BF16),
        scratch_shapes=[pltpu.VMEM((TM_WIN, D_MODEL), BF16)],
        compiler_params=pltpu.CompilerParams(
            dimension_semantics=("arbitrary", "arbitrary"),
            vmem_limit_bytes=VMEM_LIMIT),
        name="ln_win",
    )(x2, g, b, w)


def _lru_kernel(zx_ref, zg_ref, cw_ref, cb_ref, wcat_ref, ba_ref, bx_ref, lam_ref,
                o_ref, xs_ref, hp_ref, a_ref, g_ref):
    t = pl.program_id(2)
    tt = TT_LRU

    @pl.when(t == 0)
    def _():
        xs_ref[0:SUBLANES, :] = jnp.zeros((SUBLANES, CB_LRU), F32)
        hp_ref[...] = jnp.zeros_like(hp_ref)

    @pl.when(t > 0)
    def _():
        xs_ref[0:SUBLANES, :] = xs_ref[tt:tt + SUBLANES, :]

    xs_ref[SUBLANES:SUBLANES + tt, :] = zx_ref[...].astype(F32)

    rows = 128
    for rb in range(tt // rows):
        acc = jnp.broadcast_to(cb_ref[...], (rows, CB_LRU))
        for k in range(LRU_CONV_W):
            off = rb * rows + SUBLANES - (LRU_CONV_W - 1) + k
            acc = acc + cw_ref[k:k + 1, :] * xs_ref[off:off + rows, :]
        a_ref[rb * rows:(rb + 1) * rows, :] = acc

    g_ref[...] = jnp.dot(a_ref[...].astype(BF16), wcat_ref[0], preferred_element_type=F32)

    lam = lam_ref[...]
    softplus_neg = jnp.maximum(-lam, 0.0) + jnp.log1p(jnp.exp(-jnp.abs(lam)))
    cvec = -LRU_C * softplus_neg
    ba = ba_ref[...]
    bx = bx_ref[...]
    blk = 64
    row_in_vreg = lax.broadcasted_iota(jnp.int32, (blk, CB_LRU), 0) & (SUBLANES - 1)

    def body(rb, h):
        rs = pl.ds(pl.multiple_of(rb * blk, blk), blk)
        a_in = a_ref[rs, :]
        r = _sigmoid(g_ref[rs, 0:CB_LRU] + ba)
        i = _sigmoid(g_ref[rs, CB_LRU:2 * CB_LRU] + bx)
        log_a = cvec * r
        a = jnp.exp(log_a)
        u = jnp.sqrt(-jnp.tanh(log_a) * (a * a + 1.0)) * (i * a_in)
        for s in (1, 2, 4):
            m = row_in_vreg >= s
            a_sh = jnp.where(m, pltpu.roll(a, s, 0), 1.0)
            u_sh = jnp.where(m, pltpu.roll(u, s, 0), 0.0)
            u = u + a * u_sh
            a = a * a_sh
        outs = []
        for gi in range(blk // SUBLANES):
            ag = a[gi * SUBLANES:(gi + 1) * SUBLANES, :]
            ug = u[gi * SUBLANES:(gi + 1) * SUBLANES, :]
            hg = ug + ag * h
            h = hg[SUBLANES - 1:SUBLANES, :]
            outs.append(hg)
        hblk = jnp.concatenate(outs, axis=0)
        gl = zg_ref[rs, :].astype(F32)
        gelu = 0.5 * gl * (1.0 + jnp.tanh(0.7978845608028654 * (gl + 0.044715 * gl * gl * gl)))
        o_ref[rs, :] = (gelu * hblk).astype(o_ref.dtype)
        return h

    h = lax.fori_loop(0, tt // blk, body, hp_ref[0:1, :])
    hp_ref[...] = jnp.broadcast_to(h, hp_ref.shape)


def _lru_mixer(z, cw, cb, wcat, ba, bx, lam, bsz, seq):
    n = z.shape[0]
    nt = seq // TT_LRU
    ncb = D_LRU // CB_LRU
    row = lambda b, j, t: b * nt + t
    vec = pl.BlockSpec((1, CB_LRU), lambda b, j, t: (0, j))
    return pl.pallas_call(
        _lru_kernel,
        grid=(bsz, ncb, nt),
        in_specs=[
            pl.BlockSpec((TT_LRU, CB_LRU), lambda b, j, t: (row(b, j, t), j)),
            pl.BlockSpec((TT_LRU, CB_LRU), lambda b, j, t: (row(b, j, t), ncb + j)),
            pl.BlockSpec((LRU_CONV_W, CB_LRU), lambda b, j, t: (0, j)),
            vec,
            pl.BlockSpec((1, CB_LRU, 2 * CB_LRU), lambda b, j, t: (j, 0, 0)),
            vec, vec, vec,
        ],
        out_specs=pl.BlockSpec((TT_LRU, CB_LRU), lambda b, j, t: (row(b, j, t), j)),
        out_shape=jax.ShapeDtypeStruct((n, D_LRU), BF16),
        scratch_shapes=[
            pltpu.VMEM((TT_LRU + SUBLANES, CB_LRU), F32),
            pltpu.VMEM((SUBLANES, CB_LRU), F32),
            pltpu.VMEM((TT_LRU, CB_LRU), F32),
            pltpu.VMEM((TT_LRU, 2 * CB_LRU), F32),
        ],
        compiler_params=pltpu.CompilerParams(
            dimension_semantics=("arbitrary", "arbitrary", "arbitrary"),
            vmem_limit_bytes=VMEM_LIMIT),
        name="lru_mixer",
    )(z, z, cw, cb, wcat, ba, bx, lam)


def _conf_kernel(zv_ref, zg_ref, w_ref, cb_ref, lg_ref, lb_ref, o_ref, cs_ref, cv_ref):
    t = pl.program_id(1)
    tt = TT_CONF
    nlb = D_CONV // LANES

    @pl.when(t == 0)
    def _():
        cs_ref[:, 0:CONF_HALO, :] = jnp.zeros((nlb, CONF_HALO, LANES), F32)

    @pl.when(t > 0)
    def _():
        cs_ref[:, 0:CONF_HALO, :] = cs_ref[:, tt:tt + CONF_HALO, :]

    for c in range(nlb):
        ls = slice(c * LANES, (c + 1) * LANES)
        v = zv_ref[:, ls].astype(F32)
        g = zg_ref[:, ls].astype(F32)
        cs_ref[c, CONF_HALO:CONF_HALO + tt, :] = v * _sigmoid(g)

    rows = 64
    nrb = tt // rows
    base = CONF_HALO - (CONF_CONV_W - 1)

    def conv_body(c, carry):
        accs = [jnp.broadcast_to(cb_ref[c], (rows, LANES)) for _ in range(nrb)]
        for k in range(CONF_CONV_W):
            wk = w_ref[c, k:k + 1, :]
            for rb in range(nrb):
                off = rb * rows + base + k
                accs[rb] = accs[rb] + wk * cs_ref[c, off:off + rows, :]
        for rb in range(nrb):
            cv_ref[c, rb * rows:(rb + 1) * rows, :] = accs[rb]
        return carry

    lax.fori_loop(0, nlb, conv_body, 0)

    ln_rows = 32
    inv_n = 1.0 / D_CONV
    for rb in range(tt // ln_rows):
        rs = slice(rb * ln_rows, (rb + 1) * ln_rows)
        blk = cv_ref[:, rs, :]
        mu = jnp.sum(jnp.sum(blk, axis=0), axis=-1, keepdims=True) * inv_n
        d = blk - mu[None]
        var = jnp.sum(jnp.sum(d * d, axis=0), axis=-1, keepdims=True) * inv_n
        inv = lax.rsqrt(var + LN_EPS)
        for c in range(nlb):
            ls = slice(c * LANES, (c + 1) * LANES)
            y = d[c] * inv * lg_ref[:, ls] + lb_ref[:, ls]
            o_ref[rs, ls] = (y * _sigmoid(y)).astype(o_ref.dtype)


def _conf_mixer(z, w3, cb3, lg, lb, bsz, seq):
    n = z.shape[0]
    nt = seq // TT_CONF
    nlb = D_CONV // LANES
    return pl.pallas_call(
        _conf_kernel,
        grid=(bsz, nt),
        in_specs=[
            pl.BlockSpec((TT_CONF, D_CONV), lambda b, t: (b * nt + t, 2)),
            pl.BlockSpec((TT_CONF, D_CONV), lambda b, t: (b * nt + t, 3)),
            pl.BlockSpec((nlb, 32, LANES), lambda b, t: (0, 0, 0)),
            pl.BlockSpec((nlb, 1, LANES), lambda b, t: (0, 0, 0)),
            pl.BlockSpec((1, D_CONV), lambda b, t: (0, 0)),
            pl.BlockSpec((1, D_CONV), lambda b, t: (0, 0)),
        ],
        out_specs=pl.BlockSpec((TT_CONF, D_CONV), lambda b, t: (b * nt + t, 0)),
        out_shape=jax.ShapeDtypeStruct((n, D_CONV), BF16),
        scratch_shapes=[
            pltpu.VMEM((nlb, CONF_HALO + TT_CONF, LANES), F32),
            pltpu.VMEM((nlb, TT_CONF, LANES), F32),
        ],
        compiler_params=pltpu.CompilerParams(
            dimension_semantics=("arbitrary", "arbitrary"),
            vmem_limit_bytes=VMEM_LIMIT),
        name="conf_mixer",
    )(z, z, w3, cb3, lg, lb)


def _split_bf16(v):
    hi = v.astype(BF16)
    lo = (v - hi.astype(F32)).astype(BF16)
    return hi, lo


RT = D_MODEL // LANES


def _store_rows(dst_ref, row0, rows, v, stage_ref):
    for s in range(RT):
        stage_ref[pl.ds(s, rows, stride=RT), :] = v[:, s * LANES:(s + 1) * LANES]
    dst_ref[row0 * RT:(row0 + rows) * RT, :] = stage_ref[0:rows * RT, :].astype(BF16)


def _load_rows(src_ref, row0, rows, stage_ref):
    stage_ref[0:rows * RT, :] = src_ref[row0 * RT:(row0 + rows) * RT, :].astype(F32)
    return [stage_ref[pl.ds(s, rows, stride=RT), :] for s in range(RT)]


def _wout_kernel(a_ref, b_ref, x_ref, gin_ref, bin_ref, wa_ref, wb_ref, g1_ref, b1_ref,
                 wr_ref, br_ref, h1_ref, h1r_ref, lg_ref, mix_ref, hl_ref, stage_ref):
    mix_ref[...] = (jnp.dot(a_ref[...], wa_ref[...], preferred_element_type=F32)
                    + jnp.dot(b_ref[...], wb_ref[...], preferred_element_type=F32))
    rows = 64
    for c in range(TM_OUT // rows):
        rs = slice(c * rows, (c + 1) * rows)
        h = _ln_rows(x_ref[rs, :], gin_ref[...], bin_ref[...])
        h1 = _ln_rows(DN_ALPHA * h + mix_ref[rs, :], g1_ref[...], b1_ref[...])
        h1_ref[rs, :] = h1
        _store_rows(h1r_ref, c * rows, rows, h1, stage_ref)
        hi, lo = _split_bf16(h1)
        hl_ref[rs, :] = hi
        hl_ref[TM_OUT + c * rows:TM_OUT + (c + 1) * rows, :] = lo
    p = jnp.dot(hl_ref[...], wr_ref[...], preferred_element_type=F32)
    lg_ref[...] = (p[0:TM_OUT, 0:ROUTE_W] + p[0:TM_OUT, ROUTE_W:2 * ROUTE_W]
                   + p[TM_OUT:2 * TM_OUT, 0:ROUTE_W] + br_ref[...])


def _wout_router(a, b, x2, gin, bin_, wa, wb, g1, b1, wr_cat, br):
    n = x2.shape[0]
    full = lambda shape: pl.BlockSpec(shape, lambda i: tuple(0 for _ in shape))
    return pl.pallas_call(
        _wout_kernel,
        grid=(n // TM_OUT,),
        in_specs=[
            pl.BlockSpec((TM_OUT, D_LRU), lambda i: (i, 0)),
            pl.BlockSpec((TM_OUT, D_CONV), lambda i: (i, 0)),
            pl.BlockSpec((TM_OUT, D_MODEL), lambda i: (i, 0)),
            full((1, D_MODEL)), full((1, D_MODEL)),
            full((D_LRU, D_MODEL)), full((D_CONV, D_MODEL)),
            full((1, D_MODEL)), full((1, D_MODEL)),
            full((D_MODEL, 2 * ROUTE_W)), full((1, ROUTE_W)),
        ],
        out_specs=[
            pl.BlockSpec((TM_OUT, D_MODEL), lambda i: (i, 0)),
            pl.BlockSpec((TM_OUT * RT, LANES), lambda i: (i, 0)),
            pl.BlockSpec((TM_OUT, ROUTE_W), lambda i: (i, 0)),
        ],
        out_shape=[
            jax.ShapeDtypeStruct((n, D_MODEL), F32),
            jax.ShapeDtypeStruct((n * RT, LANES), BF16),
            jax.ShapeDtypeStruct((n, ROUTE_W), F32),
        ],
        scratch_shapes=[
            pltpu.VMEM((TM_OUT, D_MODEL), F32),
            pltpu.VMEM((2 * TM_OUT, D_MODEL), BF16),
            pltpu.VMEM((64 * RT, LANES), F32),
        ],
        compiler_params=pltpu.CompilerParams(
            dimension_semantics=("arbitrary",),
            vmem_limit_bytes=VMEM_LIMIT),
        name="wout_router",
    )(a, b, x2, gin, bin_, wa, wb, g1, b1, wr_cat, br)


def _route_kernel(lg_ref, info_ref, cnt_ref, tot_ref, run_ref, off_ref, tri_ref):
    phase = pl.program_id(0)
    t = pl.program_id(1)
    tm = TM_ROUTE
    l = lg_ref[...]
    lane = lax.broadcasted_iota(jnp.int32, (tm, ROUTE_W), 1)
    neg = jnp.float32(-jnp.inf)
    big = jnp.int32(1 << 20)

    gmask = lane < N_GROUPS
    gmax = jnp.max(jnp.where(gmask, l, neg), axis=-1, keepdims=True)
    gsel = jnp.min(jnp.where(gmask & (l == gmax), lane, big), axis=-1, keepdims=True)
    gsum = jnp.sum(jnp.where(gmask, jnp.exp(l - gmax), 0.0), axis=-1, keepdims=True)
    pg_top = 1.0 / gsum

    lo = N_GROUPS + EXPERTS_PER_GROUP * gsel
    emask = (lane >= lo) & (lane < lo + EXPERTS_PER_GROUP)
    v1 = jnp.max(jnp.where(emask, l, neg), axis=-1, keepdims=True)
    i1 = jnp.min(jnp.where(emask & (l == v1), lane, big), axis=-1, keepdims=True)
    emask2 = emask & (lane != i1)
    v2 = jnp.max(jnp.where(emask2, l, neg), axis=-1, keepdims=True)
    i2 = jnp.min(jnp.where(emask2 & (l == v2), lane, big), axis=-1, keepdims=True)
    e21 = jnp.exp(v2 - v1)
    q1 = pg_top / (1.0 + e21)
    q2 = pg_top * e21 / (1.0 + e21)

    oh1 = (lane == i1).astype(F32)
    oh2 = (lane == i2).astype(F32)
    ohs = oh1 + oh2

    @pl.when((phase == 0) & (t == 0))
    def _():
        tot_ref[...] = jnp.zeros_like(tot_ref)

    @pl.when(phase == 0)
    def _():
        tot_ref[...] = tot_ref[...] + jnp.sum(ohs, axis=0, keepdims=True)

    @pl.when((phase == 1) & (t == 0))
    def _():
        cnt = tot_ref[...].astype(jnp.int32)
        shift = TM_EXP.bit_length() - 1
        padded = jnp.left_shift(jnp.right_shift(cnt + (TM_EXP - 1), shift), shift)
        lane1 = lax.broadcasted_iota(jnp.int32, tot_ref.shape, 1)
        padded = jnp.where((lane1 >= N_GROUPS) & (lane1 < N_GROUPS + N_EXPERTS), padded, 0)
        inc = padded.astype(F32)
        s = 1
        while s < ROUTE_W:
            inc = inc + jnp.where(lane1 >= s, pltpu.roll(inc, s, 1), 0.0)
            s *= 2
        off_ref[...] = inc - padded.astype(F32)
        run_ref[...] = jnp.zeros_like(run_ref)
        r_i = lax.broadcasted_iota(jnp.int32, (tm, tm), 0)
        c_i = lax.broadcasted_iota(jnp.int32, (tm, tm), 1)
        tri_ref[...] = (c_i < r_i).astype(BF16)

    @pl.when(phase == 1)
    def _():
        cum = jnp.dot(tri_ref[...], ohs.astype(BF16), preferred_element_type=F32)
        basev = off_ref[0:1, :] + run_ref[0:1, :] + cum
        p1 = jnp.sum(oh1 * basev, axis=-1, keepdims=True)
        p2 = jnp.sum(oh2 * basev, axis=-1, keepdims=True)
        run_ref[...] = run_ref[...] + jnp.sum(ohs, axis=0, keepdims=True)
        info = jnp.where(lane == 0, p1, 0.0)
        info = jnp.where(lane == 1, p2, info)
        info = jnp.where(lane == 2, q1, info)
        info = jnp.where(lane == 3, q2, info)
        info_ref[...] = info

    cnt_ref[...] = tot_ref[...]


def _route(logits):
    n = logits.shape[0]
    nt = n // TM_ROUTE
    return pl.pallas_call(
        _route_kernel,
        grid=(2, nt),
        in_specs=[pl.BlockSpec((TM_ROUTE, ROUTE_W), lambda p, t: (t, 0))],
        out_specs=[
            pl.BlockSpec((TM_ROUTE, ROUTE_W), lambda p, t: (p * t, 0)),
            pl.BlockSpec((SUBLANES, ROUTE_W), lambda p, t: (0, 0)),
        ],
        out_shape=[
            jax.ShapeDtypeStruct((n, ROUTE_W), F32),
            jax.ShapeDtypeStruct((SUBLANES, ROUTE_W), F32),
        ],
        scratch_shapes=[
            pltpu.VMEM((SUBLANES, ROUTE_W), F32),
            pltpu.VMEM((SUBLANES, ROUTE_W), F32),
            pltpu.VMEM((SUBLANES, ROUTE_W), F32),
            pltpu.VMEM((TM_ROUTE, TM_ROUTE), BF16),
        ],
        compiler_params=pltpu.CompilerParams(
            dimension_semantics=("arbitrary", "arbitrary"),
            vmem_limit_bytes=VMEM_LIMIT),
        name="route",
    )(logits)


def _prow_copy(src_ref, src_row, dst_ref, dst_row, sem, rows=1):
    src = src_ref.at[pl.ds(pl.multiple_of(src_row * RT, RT), rows * RT)]
    dst = dst_ref.at[pl.ds(pl.multiple_of(dst_row * RT, RT), rows * RT)]
    return pltpu.make_async_copy(src, dst, sem)


ZERO_ROWS = TM_EXP // 2
DMA_UNROLL = 8


def _pad_fill(e, ps_ref, pl_ref, zbuf_ref, xs_ref, zsem, wait):
    ln = pl_ref[e]
    st = ps_ref[e]
    b = 1
    while b <= ZERO_ROWS:
        @pl.when((ln & b) != 0)
        def _(b=b):
            cp = _prow_copy(zbuf_ref, 0, xs_ref, st + (ln & (b - 1)), zsem, rows=b)
            if wait:
                cp.wait()
            else:
                cp.start()
        b *= 2


def _dispatch_kernel(pos0_ref, pos1_ref, ps_ref, pl_ref, h1p_ref, xs_ref, zbuf_ref, sem, zsem):
    i = pl.program_id(0)
    tm = TM_DISP

    @pl.when(i == 0)
    def _():
        zbuf_ref[...] = jnp.zeros_like(zbuf_ref)

        def fill_start(e, _):
            _pad_fill(e, ps_ref, pl_ref, zbuf_ref, xs_ref, zsem, False)
            return 0

        lax.fori_loop(0, N_EXPERTS, fill_start, 0)

    def issue(k, _):
        for u in range(DMA_UNROLL):
            r = k * DMA_UNROLL + u
            tok = i * tm + r
            _prow_copy(h1p_ref, r, xs_ref, pos0_ref[tok], sem).start(priority=0)
            _prow_copy(h1p_ref, r, xs_ref, pos1_ref[tok], sem).start(priority=1)
        return 0

    lax.fori_loop(0, tm // DMA_UNROLL, issue, 0)

    def drain(k, _):
        for u in range(DMA_UNROLL):
            _prow_copy(h1p_ref, 0, xs_ref, 0, sem).wait()
            _prow_copy(h1p_ref, 0, xs_ref, 0, sem).wait()
        return 0

    lax.fori_loop(0, tm // DMA_UNROLL, drain, 0)

    @pl.when(i == pl.num_programs(0) - 1)
    def _():
        def fill_wait(e, _):
            _pad_fill(e, ps_ref, pl_ref, zbuf_ref, xs_ref, zsem, True)
            return 0

        lax.fori_loop(0, N_EXPERTS, fill_wait, 0)


def _dispatch(pos0, pos1, pad_start, pad_len, h1p, n_rows):
    n = h1p.shape[0] // RT
    return pl.pallas_call(
        _dispatch_kernel,
        grid_spec=pltpu.PrefetchScalarGridSpec(
            num_scalar_prefetch=4,
            grid=(n // TM_DISP,),
            in_specs=[pl.BlockSpec((TM_DISP * RT, LANES), lambda i, *_: (i, 0))],
            out_specs=pl.BlockSpec(memory_space=pl.ANY),
            scratch_shapes=[
                pltpu.VMEM((ZERO_ROWS * RT, LANES), BF16),
                pltpu.SemaphoreType.DMA(()),
                pltpu.SemaphoreType.DMA(()),
            ],
        ),
        out_shape=jax.ShapeDtypeStruct((n_rows * RT, LANES), BF16),
        compiler_params=pltpu.CompilerParams(
            dimension_semantics=("arbitrary",),
            vmem_limit_bytes=VMEM_LIMIT),
        name="dispatch",
    )(pos0, pos1, pad_start, pad_len, h1p)


def _expert_kernel(te_ref, nu_ref, nx_ref, sl_ref, xs_ref, wg_hbm, wu_hbm, wd_hbm, ys_ref,
                   wgf_ref, wuf_ref, wdf_ref, wgb_ref, wub_ref, wdb_ref, stage_ref, wsem):
    i = pl.program_id(0)
    used = i < nu_ref[0]
    e = te_ref[i]
    s = sl_ref[i]
    fresh = (i == 0) | (e != te_ref[jnp.maximum(i - 1, 0)])

    def weight_copies(expert, slot):
        return (pltpu.make_async_copy(wg_hbm.at[expert], wgf_ref.at[slot], wsem.at[slot, 0]),
                pltpu.make_async_copy(wu_hbm.at[expert], wuf_ref.at[slot], wsem.at[slot, 1]),
                pltpu.make_async_copy(wd_hbm.at[expert], wdf_ref.at[slot], wsem.at[slot, 2]))

    @pl.when(i == 0)
    def _():
        for cp in weight_copies(e, s):
            cp.start()

    @pl.when(used & fresh)
    def _():
        for cp in weight_copies(e, s):
            cp.wait()

        @pl.when(nx_ref[i] >= 0)
        def _():
            for cp in weight_copies(nx_ref[i], 1 - s):
                cp.start()

        wgb_ref[...] = wgf_ref[s].astype(BF16)
        wub_ref[...] = wuf_ref[s].astype(BF16)
        wdb_ref[...] = wdf_ref[s].astype(BF16)

    @pl.when(used)
    def _():
        x = jnp.concatenate([p.astype(BF16) for p in _load_rows(xs_ref, 0, TM_EXP, stage_ref)],
                            axis=1)
        g = jnp.dot(x, wgb_ref[...], preferred_element_type=F32)
        u = jnp.dot(x, wub_ref[...], preferred_element_type=F32)
        h = (g * _sigmoid(g) * u).astype(BF16)
        y = jnp.dot(h, wdb_ref[...], preferred_element_type=F32)
        _store_rows(ys_ref, 0, TM_EXP, y, stage_ref)


def _expert_ffn(tile_expert, n_used, next_expert, slot, xs, wg, wu, wd):
    n_rows = xs.shape[0] // RT
    n_tiles = n_rows // TM_EXP

    def row_map(i, te, nu, nx, sl):
        return (jnp.minimum(i, nu[0] - 1), 0)

    return pl.pallas_call(
        _expert_kernel,
        grid_spec=pltpu.PrefetchScalarGridSpec(
            num_scalar_prefetch=4,
            grid=(n_tiles,),
            in_specs=[
                pl.BlockSpec((TM_EXP * RT, LANES), row_map),
                pl.BlockSpec(memory_space=pl.ANY),
                pl.BlockSpec(memory_space=pl.ANY),
                pl.BlockSpec(memory_space=pl.ANY),
            ],
            out_specs=pl.BlockSpec((TM_EXP * RT, LANES), row_map),
            scratch_shapes=[
                pltpu.VMEM((2, D_MODEL, D_EXPERT), F32),
                pltpu.VMEM((2, D_MODEL, D_EXPERT), F32),
                pltpu.VMEM((2, D_EXPERT, D_MODEL), F32),
                pltpu.VMEM((D_MODEL, D_EXPERT), BF16),
                pltpu.VMEM((D_MODEL, D_EXPERT), BF16),
                pltpu.VMEM((D_EXPERT, D_MODEL), BF16),
                pltpu.VMEM((TM_EXP * RT, LANES), F32),
                pltpu.SemaphoreType.DMA((2, 3)),
            ],
        ),
        out_shape=jax.ShapeDtypeStruct((n_rows * RT, LANES), BF16),
        compiler_params=pltpu.CompilerParams(
            dimension_semantics=("arbitrary",),
            vmem_limit_bytes=VMEM_LIMIT),
        name="expert_ffn",
    )(tile_expert, n_used, next_expert, slot, xs, wg, wu, wd)


def _combine_kernel(pos0_ref, pos1_ref, h1_ref, info_ref, g2_ref, b2_ref, ys_ref, o_ref,
                    ybuf_ref, stage_ref, sem):
    i = pl.program_id(0)
    tm = TM_COMB
    par = lax.rem(i, 2)

    def issue(step, parity):
        def body(k, _):
            for u in range(DMA_UNROLL):
                r = k * DMA_UNROLL + u
                tok = step * tm + r
                _prow_copy(ys_ref, pos0_ref[tok], ybuf_ref.at[parity, 0], r,
                           sem.at[parity]).start(priority=0)
                _prow_copy(ys_ref, pos1_ref[tok], ybuf_ref.at[parity, 1], r,
                           sem.at[parity]).start(priority=1)
            return 0
        lax.fori_loop(0, tm // DMA_UNROLL, body, 0)

    def drain(parity):
        def body(k, _):
            for u in range(DMA_UNROLL):
                _prow_copy(ys_ref, 0, ybuf_ref.at[parity, 0], 0, sem.at[parity]).wait()
                _prow_copy(ys_ref, 0, ybuf_ref.at[parity, 1], 0, sem.at[parity]).wait()
            return 0
        lax.fori_loop(0, tm // DMA_UNROLL, body, 0)

    @pl.when(i == 0)
    def _():
        issue(0, 0)

    @pl.when(i + 1 < pl.num_programs(0))
    def _():
        issue(i + 1, 1 - par)

    drain(par)

    rows = 64
    for c in range(tm // rows):
        rs = slice(c * rows, (c + 1) * rows)
        q1 = info_ref[rs, 2:3]
        q2 = info_ref[rs, 3:4]
        y0 = _load_rows(ybuf_ref.at[par, 0], c * rows, rows, stage_ref)
        y1 = _load_rows(ybuf_ref.at[par, 1], c * rows, rows, stage_ref)
        ffn = jnp.concatenate([q1 * a + q2 * b for a, b in zip(y0, y1)], axis=1)
        o_ref[rs, :] = _ln_rows(DN_ALPHA * h1_ref[rs, :] + ffn, g2_ref[...], b2_ref[...])


def _combine(pos0, pos1, h1, info, g2, b2, ys):
    n = h1.shape[0]
    return pl.pallas_call(
        _combine_kernel,
        grid_spec=pltpu.PrefetchScalarGridSpec(
            num_scalar_prefetch=2,
            grid=(n // TM_COMB,),
            in_specs=[
                pl.BlockSpec((TM_COMB, D_MODEL), lambda i, p0, p1: (i, 0)),
                pl.BlockSpec((TM_COMB, ROUTE_W), lambda i, p0, p1: (i, 0)),
                pl.BlockSpec((1, D_MODEL), lambda i, p0, p1: (0, 0)),
                pl.BlockSpec((1, D_MODEL), lambda i, p0, p1: (0, 0)),
                pl.BlockSpec(memory_space=pl.ANY),
            ],
            out_specs=pl.BlockSpec((TM_COMB, D_MODEL), lambda i, p0, p1: (i, 0)),
            scratch_shapes=[
                pltpu.VMEM((2, 2, TM_COMB * RT, LANES), BF16),
                pltpu.VMEM((64 * RT, LANES), F32),
                pltpu.SemaphoreType.DMA((2,)),
            ],
        ),
        out_shape=jax.ShapeDtypeStruct((n, D_MODEL), F32),
        compiler_params=pltpu.CompilerParams(
            dimension_semantics=("arbitrary",),
            vmem_limit_bytes=VMEM_LIMIT),
        name="combine",
    )(pos0, pos1, h1, info, g2, b2, ys)


def _block_diag(w, per):
    h, hd, _ = w.shape
    wg = w.reshape(h // per, per, hd, hd)
    eye = jnp.eye(per, dtype=w.dtype)
    return jnp.einsum("gpij,pq->gpiqj", wg, eye).reshape(h // per, per * hd, per * hd)


def kernel(x, ln_in_g, ln_in_b, w_in, lru_conv_w, lru_conv_b, lru_w_a, lru_b_a, lru_w_x, lru_b_x,
           lru_lambda, conf_conv_w, conf_conv_b, conf_ln_g, conf_ln_b, w_out, ln1_g, ln1_b,
           router_group_w, router_group_b, router_expert_w, router_expert_b, exp_w_gate, exp_w_up,
           exp_w_down, ln2_g, ln2_b):
    bsz, seq, d = x.shape
    n = bsz * seq
    x2 = x.reshape(n, d)
    row = lambda v: v.reshape(1, -1).astype(F32)
    l = 0

    z = _ln_win(x2, row(ln_in_g), row(ln_in_b), w_in[l])

    per = CB_LRU // LRU_HEAD_DIM
    wcat = jnp.concatenate([_block_diag(lru_w_a[l], per), _block_diag(lru_w_x[l], per)],
                           axis=-1).astype(BF16)
    a_out = _lru_mixer(z, lru_conv_w[l], row(lru_conv_b[l]), wcat, row(lru_b_a[l]),
                       row(lru_b_x[l]), row(lru_lambda[l]), bsz, seq)

    nlb = D_CONV // LANES
    w3 = jnp.pad(conf_conv_w[l], ((0, 32 - CONF_CONV_W), (0, 0)))
    w3 = w3.reshape(32, nlb, LANES).transpose(1, 0, 2)
    cb3 = conf_conv_b[l].reshape(nlb, 1, LANES)
    b_out = _conf_mixer(z, w3, cb3, row(conf_ln_g[l]), row(conf_ln_b[l]), bsz, seq)

    wr = jnp.concatenate([router_group_w[l], router_expert_w[l]], axis=1)
    wr = jnp.pad(wr, ((0, 0), (0, ROUTE_W - wr.shape[1])))
    wr_hi = wr.astype(BF16)
    wr_lo = (wr - wr_hi.astype(F32)).astype(BF16)
    wr_cat = jnp.concatenate([wr_hi, wr_lo], axis=1)
    br = jnp.concatenate([router_group_b[l], router_expert_b[l]])
    br = jnp.pad(br, (0, ROUTE_W - br.shape[0])).reshape(1, ROUTE_W)
    wo = w_out[l].astype(BF16)
    h1, h1p, logits = _wout_router(a_out, b_out, x2, row(ln_in_g), row(ln_in_b), wo[:D_LRU],
                                   wo[D_LRU:], row(ln1_g[l]), row(ln1_b[l]), wr_cat, br)

    info, counts = _route(logits)
    pos0 = info[:, 0].astype(jnp.int32)
    pos1 = info[:, 1].astype(jnp.int32)

    n_tiles = (n * 2) // TM_EXP + N_EXPERTS
    cnt = counts[0, N_GROUPS:N_GROUPS + N_EXPERTS].astype(jnp.int32)
    tiles_per = (cnt + TM_EXP - 1) // TM_EXP
    ends = jnp.cumsum(tiles_per)
    n_used = ends[-1:].astype(jnp.int32)
    tile_ids = jnp.arange(n_tiles, dtype=jnp.int32)
    tile_expert = jnp.minimum(jnp.sum(tile_ids[:, None] >= ends[None, :], axis=1),
                              N_EXPERTS - 1).astype(jnp.int32)
    last = tile_expert[jnp.maximum(n_used[0] - 1, 0)]
    tile_expert = jnp.where(tile_ids < n_used[0], tile_expert, last)

    nxt_tile = ends[tile_expert]
    next_expert = jnp.where(nxt_tile < n_used[0],
                            tile_expert[jnp.minimum(nxt_tile, n_tiles - 1)], -1).astype(jnp.int32)
    changes = jnp.concatenate([jnp.zeros((1,), jnp.int32),
                               (tile_expert[1:] != tile_expert[:-1]).astype(jnp.int32)])
    slot = (jnp.cumsum(changes) & 1).astype(jnp.int32)
    starts = (ends - tiles_per) * TM_EXP
    pad_start = (starts + cnt).astype(jnp.int32)
    pad_len = (tiles_per * TM_EXP - cnt).astype(jnp.int32)

    xs = _dispatch(pos0, pos1, pad_start, pad_len, h1p, n_tiles * TM_EXP)
    shp = (N_EXPERTS, D_MODEL, D_EXPERT)
    ys = _expert_ffn(tile_expert, n_used, next_expert, slot, xs, exp_w_gate[l].reshape(shp),
                     exp_w_up[l].reshape(shp), exp_w_down[l].reshape(N_EXPERTS, D_EXPERT, D_MODEL))
    out = _combine(pos0, pos1, h1, info, row(ln2_g[l]), row(ln2_b[l]), ys)
    return out.reshape(bsz, seq, d)
```

```python
import functools
import math

import jax
import jax.numpy as jnp
from jax import lax
from jax.experimental import pallas as pl
from jax.experimental.pallas import tpu as pltpu

F32 = jnp.float32
BF16 = jnp.bfloat16

D_MODEL = 2048
D_LRU = 1024
D_CONV = 1024
LRU_HEADS = 16
LRU_HEAD_DIM = 64
LRU_C = 8.0
LRU_CONV_W = 4
CONF_CONV_W = 31
N_GROUPS = 4
EXPERTS_PER_GROUP = 8
N_EXPERTS = N_GROUPS * EXPERTS_PER_GROUP
D_EXPERT = 512
LN_EPS = 1e-5
DEPTH = 1
DN_ALPHA = (2 * DEPTH) ** 0.25

LANES = 128
SUBLANES = 8
VMEM_LIMIT = 56 * 1024 * 1024

TM_WIN = 512
TN_WIN = 1024
W_CHUNK = 512
TT_LRU = 512
CB_LRU = 256
TT_CONF = 256
CONF_HALO = 32
TM_OUT = 256
TM_ROUTE = 512
TM_DISP = 256
TM_EXP = 256
TM_COMB = 256
ROUTE_W = 128


def _sigmoid(x):
    return 0.5 * (jnp.tanh(0.5 * x) + 1.0)


def _ln_rows(x, g, b):
    mu = jnp.mean(x, axis=-1, keepdims=True)
    xc = x - mu
    var = jnp.mean(xc * xc, axis=-1, keepdims=True)
    return xc * lax.rsqrt(var + LN_EPS) * g + b


def _stage_weight(w_hbm, wb_ref, wst_ref, wsem):
    nchunk = wb_ref.shape[1] // W_CHUNK

    def chunk_copy(c):
        return pltpu.make_async_copy(w_hbm.at[:, pl.ds(c * W_CHUNK, W_CHUNK)],
                                     wst_ref.at[c % 2], wsem.at[c % 2])

    chunk_copy(0).start()
    for c in range(nchunk):
        if c + 1 < nchunk:
            chunk_copy(c + 1).start()
        chunk_copy(c).wait()
        wb_ref[:, c * W_CHUNK:(c + 1) * W_CHUNK] = wst_ref[c % 2].astype(BF16)


def _ln_win_kernel(x_ref, g_ref, b_ref, w_hbm, z_ref, wb_ref, wst_ref, xn_ref, wsem):
    s = pl.program_id(0)
    par = lax.rem(s, 2)

    @pl.when(s == 0)
    def _():
        xn_ref[1] = jnp.zeros(xn_ref.shape[1:], BF16)
        _stage_weight(w_hbm, wb_ref, wst_ref, wsem)

    rows = 128
    for c in range(TM_WIN // rows):
        rs = slice(c * rows, (c + 1) * rows)
        xn_ref[par, rs, :] = _ln_rows(x_ref[rs, :], g_ref[...], b_ref[...]).astype(BF16)

    xprev = xn_ref[1 - par]
    for c in range(z_ref.shape[1] // TN_WIN):
        cs = slice(c * TN_WIN, (c + 1) * TN_WIN)
        z_ref[:, cs] = jnp.dot(xprev, wb_ref[:, cs], preferred_element_type=F32).astype(z_ref.dtype)


def _ln_win(x2, g, b, w):
    n = x2.shape[0]
    ncol = w.shape[1]
    nt = n // TM_WIN
    return pl.pallas_call(
        _ln_win_kernel,
        grid=(nt + 1,),
        in_specs=[
            pl.BlockSpec((TM_WIN, D_MODEL), lambda s: (jnp.minimum(s, nt - 1), 0)),
            pl.BlockSpec((1, D_MODEL), lambda s: (0, 0)),
            pl.BlockSpec((1, D_MODEL), lambda s: (0, 0)),
            pl.BlockSpec(memory_space=pl.ANY),
        ],
        out_specs=pl.BlockSpec((TM_WIN, ncol), lambda s: (jnp.maximum(s - 1, 0), 0)),
        out_shape=jax.ShapeDtypeStruct((n, ncol), BF16),
        scratch_shapes=[
            pltpu.VMEM((D_MODEL, ncol), BF16),
            pltpu.VMEM((2, D_MODEL, W_CHUNK), F32),
            pltpu.VMEM((2, TM_WIN, D_MODEL), BF16),
            pltpu.SemaphoreType.DMA((2,)),
        ],
        compiler_params=pltpu.CompilerParams(
            dimension_semantics=("arbitrary",),
            vmem_limit_bytes=VMEM_LIMIT),
        name="ln_win",
    )(x2, g, b, w)


def _lru_kernel(zx_ref, zg_ref, cw_ref, cb_ref, wcat_ref, ba_ref, bx_ref, lam_ref,
                o_ref, xs_ref, hp_ref, a_ref, g_ref):
    t = pl.program_id(2)
    tt = TT_LRU

    @pl.when(t == 0)
    def _():
        xs_ref[0:SUBLANES, :] = jnp.zeros((SUBLANES, CB_LRU), F32)
        hp_ref[...] = jnp.zeros_like(hp_ref)

    @pl.when(t > 0)
    def _():
        xs_ref[0:SUBLANES, :] = xs_ref[tt:tt + SUBLANES, :]

    xs_ref[SUBLANES:SUBLANES + tt, :] = zx_ref[...].astype(F32)

    rows = 128
    for rb in range(tt // rows):
        acc = jnp.broadcast_to(cb_ref[...], (rows, CB_LRU))
        for k in range(LRU_CONV_W):
            off = rb * rows + SUBLANES - (LRU_CONV_W - 1) + k
            acc = acc + cw_ref[k:k + 1, :] * xs_ref[off:off + rows, :]
        a_ref[rb * rows:(rb + 1) * rows, :] = acc

    g_ref[...] = jnp.dot(a_ref[...].astype(BF16), wcat_ref[0], preferred_element_type=F32)

    lam = lam_ref[...]
    softplus_neg = jnp.maximum(-lam, 0.0) + jnp.log1p(jnp.exp(-jnp.abs(lam)))
    cvec = -LRU_C * softplus_neg
    ba = ba_ref[...]
    bx = bx_ref[...]
    blk = 64
    row_in_vreg = lax.broadcasted_iota(jnp.int32, (blk, CB_LRU), 0) & (SUBLANES - 1)

    def body(rb, h):
        rs = pl.ds(pl.multiple_of(rb * blk, blk), blk)
        a_in = a_ref[rs, :]
        r = _sigmoid(g_ref[rs, 0:CB_LRU] + ba)
        i = _sigmoid(g_ref[rs, CB_LRU:2 * CB_LRU] + bx)
        log_a = cvec * r
        a = jnp.exp(log_a)
        u = jnp.sqrt(-jnp.tanh(log_a) * (a * a + 1.0)) * (i * a_in)
        for s in (1, 2, 4):
            m = row_in_vreg >= s
            a_sh = jnp.where(m, pltpu.roll(a, s, 0), 1.0)
            u_sh = jnp.where(m, pltpu.roll(u, s, 0), 0.0)
            u = u + a * u_sh
            a = a * a_sh
        outs = []
        for gi in range(blk // SUBLANES):
            ag = a[gi * SUBLANES:(gi + 1) * SUBLANES, :]
            ug = u[gi * SUBLANES:(gi + 1) * SUBLANES, :]
            hg = ug + ag * h
            h = hg[SUBLANES - 1:SUBLANES, :]
            outs.append(hg)
        hblk = jnp.concatenate(outs, axis=0)
        gl = zg_ref[rs, :].astype(F32)
        gelu = 0.5 * gl * (1.0 + jnp.tanh(0.7978845608028654 * (gl + 0.044715 * gl * gl * gl)))
        o_ref[rs, :] = (gelu * hblk).astype(o_ref.dtype)
        return h

    h = lax.fori_loop(0, tt // blk, body, hp_ref[0:1, :])
    hp_ref[...] = jnp.broadcast_to(h, hp_ref.shape)


def _lru_mixer(z, cw, cb, wcat, ba, bx, lam, bsz, seq):
    n = z.shape[0]
    nt = seq // TT_LRU
    ncb = D_LRU // CB_LRU
    row = lambda b, j, t: b * nt + t
    vec = pl.BlockSpec((1, CB_LRU), lambda b, j, t: (0, j))
    return pl.pallas_call(
        _lru_kernel,
        grid=(bsz, ncb, nt),
        in_specs=[
            pl.BlockSpec((TT_LRU, CB_LRU), lambda b, j, t: (row(b, j, t), j)),
            pl.BlockSpec((TT_LRU, CB_LRU), lambda b, j, t: (row(b, j, t), ncb + j)),
            pl.BlockSpec((LRU_CONV_W, CB_LRU), lambda b, j, t: (0, j)),
            vec,
            pl.BlockSpec((1, CB_LRU, 2 * CB_LRU), lambda b, j, t: (j, 0, 0)),
            vec, vec, vec,
        ],
        out_specs=pl.BlockSpec((TT_LRU, CB_LRU), lambda b, j, t: (row(b, j, t), j)),
        out_shape=jax.ShapeDtypeStruct((n, D_LRU), BF16),
        scratch_shapes=[
            pltpu.VMEM((TT_LRU + SUBLANES, CB_LRU), F32),
            pltpu.VMEM((SUBLANES, CB_LRU), F32),
            pltpu.VMEM((TT_LRU, CB_LRU), F32),
            pltpu.VMEM((TT_LRU, 2 * CB_LRU), F32),
        ],
        compiler_params=pltpu.CompilerParams(
            dimension_semantics=("arbitrary", "arbitrary", "arbitrary"),
            vmem_limit_bytes=VMEM_LIMIT),
        name="lru_mixer",
    )(z, z, cw, cb, wcat, ba, bx, lam)


def _conf_kernel(zv_ref, zg_ref, w_ref, cb_ref, lg_ref, lb_ref, o_ref, cs_ref, cv_ref):
    t = pl.program_id(1)
    tt = TT_CONF
    nlb = D_CONV // LANES

    @pl.when(t == 0)
    def _():
        cs_ref[:, 0:CONF_HALO, :] = jnp.zeros((nlb, CONF_HALO, LANES), F32)

    @pl.when(t > 0)
    def _():
        cs_ref[:, 0:CONF_HALO, :] = cs_ref[:, tt:tt + CONF_HALO, :]

    for c in range(nlb):
        ls = slice(c * LANES, (c + 1) * LANES)
        v = zv_ref[:, ls].astype(F32)
        g = zg_ref[:, ls].astype(F32)
        cs_ref[c, CONF_HALO:CONF_HALO + tt, :] = v * _sigmoid(g)

    rows = 64
    nrb = tt // rows
    base = CONF_HALO - (CONF_CONV_W - 1)

    def conv_body(c, carry):
        accs = [jnp.broadcast_to(cb_ref[c], (rows, LANES)) for _ in range(nrb)]
        for k in range(CONF_CONV_W):
            wk = w_ref[c, k:k + 1, :]
            for rb in range(nrb):
                off = rb * rows + base + k
                accs[rb] = accs[rb] + wk * cs_ref[c, off:off + rows, :]
        for rb in range(nrb):
            cv_ref[c, rb * rows:(rb + 1) * rows, :] = accs[rb]
        return carry

    lax.fori_loop(0, nlb, conv_body, 0)

    ln_rows = 32
    inv_n = 1.0 / D_CONV
    for rb in range(tt // ln_rows):
        rs = slice(rb * ln_rows, (rb + 1) * ln_rows)
        blk = cv_ref[:, rs, :]
        mu = jnp.sum(jnp.sum(blk, axis=0), axis=-1, keepdims=True) * inv_n
        d = blk - mu[None]
        var = jnp.sum(jnp.sum(d * d, axis=0), axis=-1, keepdims=True) * inv_n
        inv = lax.rsqrt(var + LN_EPS)
        for c in range(nlb):
            ls = slice(c * LANES, (c + 1) * LANES)
            y = d[c] * inv * lg_ref[:, ls] + lb_ref[:, ls]
            o_ref[rs, ls] = (y * _sigmoid(y)).astype(o_ref.dtype)


def _conf_mixer(z, w3, cb3, lg, lb, bsz, seq):
    n = z.shape[0]
    nt = seq // TT_CONF
    nlb = D_CONV // LANES
    return pl.pallas_call(
        _conf_kernel,
        grid=(bsz, nt),
        in_specs=[
            pl.BlockSpec((TT_CONF, D_CONV), lambda b, t: (b * nt + t, 2)),
            pl.BlockSpec((TT_CONF, D_CONV), lambda b, t: (b * nt + t, 3)),
            pl.BlockSpec((nlb, 32, LANES), lambda b, t: (0, 0, 0)),
            pl.BlockSpec((nlb, 1, LANES), lambda b, t: (0, 0, 0)),
            pl.BlockSpec((1, D_CONV), lambda b, t: (0, 0)),
            pl.BlockSpec((1, D_CONV), lambda b, t: (0, 0)),
        ],
        out_specs=pl.BlockSpec((TT_CONF, D_CONV), lambda b, t: (b * nt + t, 0)),
        out_shape=jax.ShapeDtypeStruct((n, D_CONV), BF16),
        scratch_shapes=[
            pltpu.VMEM((nlb, CONF_HALO + TT_CONF, LANES), F32),
            pltpu.VMEM((nlb, TT_CONF, LANES), F32),
        ],
        compiler_params=pltpu.CompilerParams(
            dimension_semantics=("arbitrary", "arbitrary"),
            vmem_limit_bytes=VMEM_LIMIT),
        name="conf_mixer",
    )(z, z, w3, cb3, lg, lb)


def _split_bf16(v):
    hi = v.astype(BF16)
    lo = (v - hi.astype(F32)).astype(BF16)
    return hi, lo


RT = D_MODEL // LANES


def _store_rows(dst_ref, row0, rows, v, stage_ref):
    for s in range(RT):
        stage_ref[pl.ds(s, rows, stride=RT), :] = v[:, s * LANES:(s + 1) * LANES]
    dst_ref[row0 * RT:(row0 + rows) * RT, :] = stage_ref[0:rows * RT, :].astype(BF16)


def _load_rows(src_ref, row0, rows, stage_ref):
    stage_ref[0:rows * RT, :] = src_ref[row0 * RT:(row0 + rows) * RT, :].astype(F32)
    return [stage_ref[pl.ds(s, rows, stride=RT), :] for s in range(RT)]


def _wout_kernel(a_ref, b_ref, x_ref, gin_ref, bin_ref, wo_hbm, g1_ref, b1_ref,
                 wr_ref, br_ref, h1_ref, h1r_ref, lg_ref, wob_ref, wst_ref, mix_ref, hl_ref,
                 stage_ref, wsem):
    s = pl.program_id(0)
    par = lax.rem(s, 2)

    @pl.when(s == 0)
    def _():
        mix_ref[1] = jnp.zeros(mix_ref.shape[1:], F32)
        _stage_weight(wo_hbm, wob_ref, wst_ref, wsem)

    mix_ref[par] = (jnp.dot(a_ref[...], wob_ref[0:D_LRU, :], preferred_element_type=F32)
                    + jnp.dot(b_ref[...], wob_ref[D_LRU:D_LRU + D_CONV, :],
                              preferred_element_type=F32))
    rows = 64
    for c in range(TM_OUT // rows):
        rs = slice(c * rows, (c + 1) * rows)
        h = _ln_rows(x_ref[rs, :], gin_ref[...], bin_ref[...])
        h1 = _ln_rows(DN_ALPHA * h + mix_ref[1 - par, rs, :], g1_ref[...], b1_ref[...])
        h1_ref[rs, :] = h1
        _store_rows(h1r_ref, c * rows, rows, h1, stage_ref)
        hi, lo = _split_bf16(h1)
        hl_ref[rs, :] = hi
        hl_ref[TM_OUT + c * rows:TM_OUT + (c + 1) * rows, :] = lo
    p = jnp.dot(hl_ref[...], wr_ref[...], preferred_element_type=F32)
    lg_ref[...] = (p[0:TM_OUT, 0:ROUTE_W] + p[0:TM_OUT, ROUTE_W:2 * ROUTE_W]
                   + p[TM_OUT:2 * TM_OUT, 0:ROUTE_W] + br_ref[...])


def _wout_router(a, b, x2, gin, bin_, wo, g1, b1, wr_cat, br):
    n = x2.shape[0]
    nt = n // TM_OUT
    full = lambda shape: pl.BlockSpec(shape, lambda s: tuple(0 for _ in shape))
    cur = lambda s: (jnp.minimum(s, nt - 1), 0)
    prev = lambda s: (jnp.maximum(s - 1, 0), 0)
    return pl.pallas_call(
        _wout_kernel,
        grid=(nt + 1,),
        in_specs=[
            pl.BlockSpec((TM_OUT, D_LRU), cur),
            pl.BlockSpec((TM_OUT, D_CONV), cur),
            pl.BlockSpec((TM_OUT, D_MODEL), prev),
            full((1, D_MODEL)), full((1, D_MODEL)),
            pl.BlockSpec(memory_space=pl.ANY),
            full((1, D_MODEL)), full((1, D_MODEL)),
            full((D_MODEL, 2 * ROUTE_W)), full((1, ROUTE_W)),
        ],
        out_specs=[
            pl.BlockSpec((TM_OUT, D_MODEL), prev),
            pl.BlockSpec((TM_OUT * RT, LANES), prev),
            pl.BlockSpec((TM_OUT, ROUTE_W), prev),
        ],
        out_shape=[
            jax.ShapeDtypeStruct((n, D_MODEL), F32),
            jax.ShapeDtypeStruct((n * RT, LANES), BF16),
            jax.ShapeDtypeStruct((n, ROUTE_W), F32),
        ],
        scratch_shapes=[
            pltpu.VMEM((D_LRU + D_CONV, D_MODEL), BF16),
            pltpu.VMEM((2, D_LRU + D_CONV, W_CHUNK), F32),
            pltpu.VMEM((2, TM_OUT, D_MODEL), F32),
            pltpu.VMEM((2 * TM_OUT, D_MODEL), BF16),
            pltpu.VMEM((64 * RT, LANES), F32),
            pltpu.SemaphoreType.DMA((2,)),
        ],
        compiler_params=pltpu.CompilerParams(
            dimension_semantics=("arbitrary",),
            vmem_limit_bytes=VMEM_LIMIT),
        name="wout_router",
    )(a, b, x2, gin, bin_, wo, g1, b1, wr_cat, br)


def _route_kernel(lg_ref, info_ref, cnt_ref, run_ref, tri_ref):
    t = pl.program_id(0)
    tm = TM_ROUTE
    l = lg_ref[...]
    lane = lax.broadcasted_iota(jnp.int32, (tm, ROUTE_W), 1)
    neg = jnp.float32(-jnp.inf)
    big = jnp.int32(1 << 20)

    gmask = lane < N_GROUPS
    gmax = jnp.max(jnp.where(gmask, l, neg), axis=-1, keepdims=True)
    gsel = jnp.min(jnp.where(gmask & (l == gmax), lane, big), axis=-1, keepdims=True)
    gsum = jnp.sum(jnp.where(gmask, jnp.exp(l - gmax), 0.0), axis=-1, keepdims=True)
    pg_top = 1.0 / gsum

    lo = N_GROUPS + EXPERTS_PER_GROUP * gsel
    emask = (lane >= lo) & (lane < lo + EXPERTS_PER_GROUP)
    v1 = jnp.max(jnp.where(emask, l, neg), axis=-1, keepdims=True)
    i1 = jnp.min(jnp.where(emask & (l == v1), lane, big), axis=-1, keepdims=True)
    emask2 = emask & (lane != i1)
    v2 = jnp.max(jnp.where(emask2, l, neg), axis=-1, keepdims=True)
    i2 = jnp.min(jnp.where(emask2 & (l == v2), lane, big), axis=-1, keepdims=True)
    e21 = jnp.exp(v2 - v1)
    q1 = pg_top / (1.0 + e21)
    q2 = pg_top * e21 / (1.0 + e21)

    oh1 = (lane == i1).astype(F32)
    oh2 = (lane == i2).astype(F32)
    ohs = oh1 + oh2

    @pl.when(t == 0)
    def _():
        run_ref[...] = jnp.zeros_like(run_ref)
        r_i = lax.broadcasted_iota(jnp.int32, (tm, tm), 0)
        c_i = lax.broadcasted_iota(jnp.int32, (tm, tm), 1)
        tri_ref[...] = (c_i < r_i).astype(BF16)

    cum = jnp.dot(tri_ref[...], ohs.astype(BF16), preferred_element_type=F32)
    basev = run_ref[0:1, :] + cum
    r1 = jnp.sum(oh1 * basev, axis=-1, keepdims=True)
    r2 = jnp.sum(oh2 * basev, axis=-1, keepdims=True)
    run_ref[...] = run_ref[...] + jnp.sum(ohs, axis=0, keepdims=True)
    info = jnp.where(lane == 0, r1, 0.0)
    info = jnp.where(lane == 1, r2, info)
    info = jnp.where(lane == 2, q1, info)
    info = jnp.where(lane == 3, q2, info)
    info = jnp.where(lane == 4, (i1 - N_GROUPS).astype(F32), info)
    info = jnp.where(lane == 5, (i2 - N_GROUPS).astype(F32), info)
    info_ref[...] = info
    cnt_ref[...] = run_ref[...]


def _route(logits):
    n = logits.shape[0]
    nt = n // TM_ROUTE
    return pl.pallas_call(
        _route_kernel,
        grid=(nt,),
        in_specs=[pl.BlockSpec((TM_ROUTE, ROUTE_W), lambda t: (t, 0))],
        out_specs=[
            pl.BlockSpec((TM_ROUTE, ROUTE_W), lambda t: (t, 0)),
            pl.BlockSpec((SUBLANES, ROUTE_W), lambda t: (0, 0)),
        ],
        out_shape=[
            jax.ShapeDtypeStruct((n, ROUTE_W), F32),
            jax.ShapeDtypeStruct((SUBLANES, ROUTE_W), F32),
        ],
        scratch_shapes=[
            pltpu.VMEM((SUBLANES, ROUTE_W), F32),
            pltpu.VMEM((TM_ROUTE, TM_ROUTE), BF16),
        ],
        compiler_params=pltpu.CompilerParams(
            dimension_semantics=("arbitrary",),
            vmem_limit_bytes=VMEM_LIMIT),
        name="route",
    )(logits)


def _prow_copy(src_ref, src_row, dst_ref, dst_row, sem, rows=1):
    src = src_ref.at[pl.ds(pl.multiple_of(src_row * RT, RT), rows * RT)]
    dst = dst_ref.at[pl.ds(pl.multiple_of(dst_row * RT, RT), rows * RT)]
    return pltpu.make_async_copy(src, dst, sem)


ZERO_ROWS = TM_EXP // 2
DMA_UNROLL = 8


def _pad_fill(e, ps_ref, pl_ref, zbuf_ref, xs_ref, zsem, wait):
    ln = pl_ref[e]
    st = ps_ref[e]
    b = 1
    while b <= ZERO_ROWS:
        @pl.when((ln & b) != 0)
        def _(b=b):
            cp = _prow_copy(zbuf_ref, 0, xs_ref, st + (ln & (b - 1)), zsem, rows=b)
            if wait:
                cp.wait()
            else:
                cp.start()
        b *= 2


class _RowPos:
    def __init__(self, e_ref, r_ref, off_ref):
        self.e_ref, self.r_ref, self.off_ref = e_ref, r_ref, off_ref

    def __getitem__(self, tok):
        return self.off_ref[self.e_ref[tok]] + self.r_ref[tok]


def _dispatch_kernel(e0_ref, e1_ref, r0_ref, r1_ref, off_ref, ps_ref, pl_ref, h1p_ref, xs_ref,
                     zbuf_ref, sem, zsem):
    i = pl.program_id(0)
    tm = TM_DISP
    pos0_ref = _RowPos(e0_ref, r0_ref, off_ref)
    pos1_ref = _RowPos(e1_ref, r1_ref, off_ref)

    @pl.when(i == 0)
    def _():
        zbuf_ref[...] = jnp.zeros_like(zbuf_ref)

        def fill_start(e, _):
            _pad_fill(e, ps_ref, pl_ref, zbuf_ref, xs_ref, zsem, False)
            return 0

        lax.fori_loop(0, N_EXPERTS, fill_start, 0)

    def issue(k, _):
        for u in range(DMA_UNROLL):
            r = k * DMA_UNROLL + u
            tok = i * tm + r
            _prow_copy(h1p_ref, r, xs_ref, pos0_ref[tok], sem).start(priority=0)
            _prow_copy(h1p_ref, r, xs_ref, pos1_ref[tok], sem).start(priority=1)
        return 0

    lax.fori_loop(0, tm // DMA_UNROLL, issue, 0)

    def drain(k, _):
        for u in range(DMA_UNROLL):
            _prow_copy(h1p_ref, 0, xs_ref, 0, sem).wait()
            _prow_copy(h1p_ref, 0, xs_ref, 0, sem).wait()
        return 0

    lax.fori_loop(0, tm // DMA_UNROLL, drain, 0)

    @pl.when(i == pl.num_programs(0) - 1)
    def _():
        def fill_wait(e, _):
            _pad_fill(e, ps_ref, pl_ref, zbuf_ref, xs_ref, zsem, True)
            return 0

        lax.fori_loop(0, N_EXPERTS, fill_wait, 0)


def _dispatch(route_idx, pad_start, pad_len, h1p, n_rows):
    n = h1p.shape[0] // RT
    return pl.pallas_call(
        _dispatch_kernel,
        grid_spec=pltpu.PrefetchScalarGridSpec(
            num_scalar_prefetch=7,
            grid=(n // TM_DISP,),
            in_specs=[pl.BlockSpec((TM_DISP * RT, LANES), lambda i, *_: (i, 0))],
            out_specs=pl.BlockSpec(memory_space=pl.ANY),
            scratch_shapes=[
                pltpu.VMEM((ZERO_ROWS * RT, LANES), BF16),
                pltpu.SemaphoreType.DMA(()),
                pltpu.SemaphoreType.DMA(()),
            ],
        ),
        out_shape=jax.ShapeDtypeStruct((n_rows * RT, LANES), BF16),
        compiler_params=pltpu.CompilerParams(
            dimension_semantics=("arbitrary",),
            vmem_limit_bytes=VMEM_LIMIT),
        name="dispatch",
    )(*route_idx, pad_start, pad_len, h1p)


def _expert_kernel(te_ref, nu_ref, nx_ref, sl_ref, xs_ref, wg_hbm, wu_hbm, wd_hbm, ys_ref,
                   wgf_ref, wuf_ref, wdf_ref, wgb_ref, wub_ref, wdb_ref, stage_ref, wsem):
    i = pl.program_id(0)
    used = i < nu_ref[0]
    e = te_ref[i]
    s = sl_ref[i]
    fresh = (i == 0) | (e != te_ref[jnp.maximum(i - 1, 0)])

    def weight_copies(expert, slot):
        return (pltpu.make_async_copy(wg_hbm.at[expert], wgf_ref.at[slot], wsem.at[slot, 0]),
                pltpu.make_async_copy(wu_hbm.at[expert], wuf_ref.at[slot], wsem.at[slot, 1]),
                pltpu.make_async_copy(wd_hbm.at[expert], wdf_ref.at[slot], wsem.at[slot, 2]))

    @pl.when(i == 0)
    def _():
        for cp in weight_copies(e, s):
            cp.start()

    @pl.when(used & fresh)
    def _():
        for cp in weight_copies(e, s):
            cp.wait()

        @pl.when(nx_ref[i] >= 0)
        def _():
            for cp in weight_copies(nx_ref[i], 1 - s):
                cp.start()

        wgb_ref[...] = wgf_ref[s].astype(BF16)
        wub_ref[...] = wuf_ref[s].astype(BF16)
        wdb_ref[...] = wdf_ref[s].astype(BF16)

    @pl.when(used)
    def _():
        x = jnp.concatenate([p.astype(BF16) for p in _load_rows(xs_ref, 0, TM_EXP, stage_ref)],
                            axis=1)
        g = jnp.dot(x, wgb_ref[...], preferred_element_type=F32)
        u = jnp.dot(x, wub_ref[...], preferred_element_type=F32)
        h = (g * _sigmoid(g) * u).astype(BF16)
        y = jnp.dot(h, wdb_ref[...], preferred_element_type=F32)
        _store_rows(ys_ref, 0, TM_EXP, y, stage_ref)


def _expert_ffn(tile_expert, n_used, next_expert, slot, xs, wg, wu, wd):
    n_rows = xs.shape[0] // RT
    n_tiles = n_rows // TM_EXP

    def row_map(i, te, nu, nx, sl):
        return (jnp.minimum(i, nu[0] - 1), 0)

    return pl.pallas_call(
        _expert_kernel,
        grid_spec=pltpu.PrefetchScalarGridSpec(
            num_scalar_prefetch=4,
            grid=(n_tiles,),
            in_specs=[
                pl.BlockSpec((TM_EXP * RT, LANES), row_map),
                pl.BlockSpec(memory_space=pl.ANY),
                pl.BlockSpec(memory_space=pl.ANY),
                pl.BlockSpec(memory_space=pl.ANY),
            ],
            out_specs=pl.BlockSpec((TM_EXP * RT, LANES), row_map),
            scratch_shapes=[
                pltpu.VMEM((2, D_MODEL, D_EXPERT), F32),
                pltpu.VMEM((2, D_MODEL, D_EXPERT), F32),
                pltpu.VMEM((2, D_EXPERT, D_MODEL), F32),
                pltpu.VMEM((D_MODEL, D_EXPERT), BF16),
                pltpu.VMEM((D_MODEL, D_EXPERT), BF16),
                pltpu.VMEM((D_EXPERT, D_MODEL), BF16),
                pltpu.VMEM((TM_EXP * RT, LANES), F32),
                pltpu.SemaphoreType.DMA((2, 3)),
            ],
        ),
        out_shape=jax.ShapeDtypeStruct((n_rows * RT, LANES), BF16),
        compiler_params=pltpu.CompilerParams(
            dimension_semantics=("arbitrary",),
            vmem_limit_bytes=VMEM_LIMIT),
        name="expert_ffn",
    )(tile_expert, n_used, next_expert, slot, xs, wg, wu, wd)


def _combine_kernel(e0_ref, e1_ref, r0_ref, r1_ref, off_ref, h1_ref, info_ref, g2_ref, b2_ref,
                    ys_ref, o_ref, ybuf_ref, stage_ref, sem):
    i = pl.program_id(0)
    tm = TM_COMB
    par = lax.rem(i, 2)
    pos0_ref = _RowPos(e0_ref, r0_ref, off_ref)
    pos1_ref = _RowPos(e1_ref, r1_ref, off_ref)

    def issue(step, parity):
        def body(k, _):
            for u in range(DMA_UNROLL):
                r = k * DMA_UNROLL + u
                tok = step * tm + r
                _prow_copy(ys_ref, pos0_ref[tok], ybuf_ref.at[parity, 0], r,
                           sem.at[parity]).start(priority=0)
                _prow_copy(ys_ref, pos1_ref[tok], ybuf_ref.at[parity, 1], r,
                           sem.at[parity]).start(priority=1)
            return 0
        lax.fori_loop(0, tm // DMA_UNROLL, body, 0)

    def drain(parity):
        def body(k, _):
            for u in range(DMA_UNROLL):
                _prow_copy(ys_ref, 0, ybuf_ref.at[parity, 0], 0, sem.at[parity]).wait()
                _prow_copy(ys_ref, 0, ybuf_ref.at[parity, 1], 0, sem.at[parity]).wait()
            return 0
        lax.fori_loop(0, tm // DMA_UNROLL, body, 0)

    @pl.when(i == 0)
    def _():
        issue(0, 0)

    @pl.when(i + 1 < pl.num_programs(0))
    def _():
        issue(i + 1, 1 - par)

    drain(par)

    rows = 64
    for c in range(tm // rows):
        rs = slice(c * rows, (c + 1) * rows)
        q1 = info_ref[rs, 2:3]
        q2 = info_ref[rs, 3:4]
        y0 = _load_rows(ybuf_ref.at[par, 0], c * rows, rows, stage_ref)
        y1 = _load_rows(ybuf_ref.at[par, 1], c * rows, rows, stage_ref)
        ffn = jnp.concatenate([q1 * a + q2 * b for a, b in zip(y0, y1)], axis=1)
        o_ref[rs, :] = _ln_rows(DN_ALPHA * h1_ref[rs, :] + ffn, g2_ref[...], b2_ref[...])


def _combine(route_idx, h1, info, g2, b2, ys):
    n = h1.shape[0]
    return pl.pallas_call(
        _combine_kernel,
        grid_spec=pltpu.PrefetchScalarGridSpec(
            num_scalar_prefetch=5,
            grid=(n // TM_COMB,),
            in_specs=[
                pl.BlockSpec((TM_COMB, D_MODEL), lambda i, *_: (i, 0)),
                pl.BlockSpec((TM_COMB, ROUTE_W), lambda i, *_: (i, 0)),
                pl.BlockSpec((1, D_MODEL), lambda i, *_: (0, 0)),
                pl.BlockSpec((1, D_MODEL), lambda i, *_: (0, 0)),
                pl.BlockSpec(memory_space=pl.ANY),
            ],
            out_specs=pl.BlockSpec((TM_COMB, D_MODEL), lambda i, *_: (i, 0)),
            scratch_shapes=[
                pltpu.VMEM((2, 2, TM_COMB * RT, LANES), BF16),
                pltpu.VMEM((64 * RT, LANES), F32),
                pltpu.SemaphoreType.DMA((2,)),
            ],
        ),
        out_shape=jax.ShapeDtypeStruct((n, D_MODEL), F32),
        compiler_params=pltpu.CompilerParams(
            dimension_semantics=("arbitrary",),
            vmem_limit_bytes=VMEM_LIMIT),
        name="combine",
    )(*route_idx, h1, info, g2, b2, ys)


def _block_diag(w, per):
    h, hd, _ = w.shape
    wg = w.reshape(h // per, per, hd, hd)
    eye = jnp.eye(per, dtype=w.dtype)
    return jnp.einsum("gpij,pq->gpiqj", wg, eye).reshape(h // per, per * hd, per * hd)


def kernel(x, ln_in_g, ln_in_b, w_in, lru_conv_w, lru_conv_b, lru_w_a, lru_b_a, lru_w_x, lru_b_x,
           lru_lambda, conf_conv_w, conf_conv_b, conf_ln_g, conf_ln_b, w_out, ln1_g, ln1_b,
           router_group_w, router_group_b, router_expert_w, router_expert_b, exp_w_gate, exp_w_up,
           exp_w_down, ln2_g, ln2_b):
    bsz, seq, d = x.shape
    n = bsz * seq
    x2 = x.reshape(n, d)
    row = lambda v: v.reshape(1, -1).astype(F32)
    l = 0

    z = _ln_win(x2, row(ln_in_g), row(ln_in_b), w_in[l])

    per = CB_LRU // LRU_HEAD_DIM
    wcat = jnp.concatenate([_block_diag(lru_w_a[l], per), _block_diag(lru_w_x[l], per)],
                           axis=-1).astype(BF16)
    a_out = _lru_mixer(z, lru_conv_w[l], row(lru_conv_b[l]), wcat, row(lru_b_a[l]),
                       row(lru_b_x[l]), row(lru_lambda[l]), bsz, seq)

    nlb = D_CONV // LANES
    w3 = jnp.pad(conf_conv_w[l], ((0, 32 - CONF_CONV_W), (0, 0)))
    w3 = w3.reshape(32, nlb, LANES).transpose(1, 0, 2)
    cb3 = conf_conv_b[l].reshape(nlb, 1, LANES)
    b_out = _conf_mixer(z, w3, cb3, row(conf_ln_g[l]), row(conf_ln_b[l]), bsz, seq)

    wr = jnp.concatenate([router_group_w[l], router_expert_w[l]], axis=1)
    wr = jnp.pad(wr, ((0, 0), (0, ROUTE_W - wr.shape[1])))
    wr_hi = wr.astype(BF16)
    wr_lo = (wr - wr_hi.astype(F32)).astype(BF16)
    wr_cat = jnp.concatenate([wr_hi, wr_lo], axis=1)
    br = jnp.concatenate([router_group_b[l], router_expert_b[l]])
    br = jnp.pad(br, (0, ROUTE_W - br.shape[0])).reshape(1, ROUTE_W)
    h1, h1p, logits = _wout_router(a_out, b_out, x2, row(ln_in_g), row(ln_in_b), w_out[l],
                                   row(ln1_g[l]), row(ln1_b[l]), wr_cat, br)

    info, counts = _route(logits)
    idx = info[:, 0:6].astype(jnp.int32)
    r0, r1, e0, e1 = idx[:, 0], idx[:, 1], idx[:, 4], idx[:, 5]

    n_tiles = (n * 2) // TM_EXP + N_EXPERTS
    cnt = counts[0, N_GROUPS:N_GROUPS + N_EXPERTS].astype(jnp.int32)
    tiles_per = (cnt + TM_EXP - 1) // TM_EXP
    ends = jnp.cumsum(tiles_per)
    n_used = ends[-1:].astype(jnp.int32)
    tile_ids = jnp.arange(n_tiles, dtype=jnp.int32)
    tile_expert = jnp.minimum(jnp.sum(tile_ids[:, None] >= ends[None, :], axis=1),
                              N_EXPERTS - 1).astype(jnp.int32)
    last = tile_expert[jnp.maximum(n_used[0] - 1, 0)]
    tile_expert = jnp.where(tile_ids < n_used[0], tile_expert, last)

    nxt_tile = ends[tile_expert]
    next_expert = jnp.where(nxt_tile < n_used[0],
                            tile_expert[jnp.minimum(nxt_tile, n_tiles - 1)], -1).astype(jnp.int32)
    changes = jnp.concatenate([jnp.zeros((1,), jnp.int32),
                               (tile_expert[1:] != tile_expert[:-1]).astype(jnp.int32)])
    slot = (jnp.cumsum(changes) & 1).astype(jnp.int32)
    starts = (ends - tiles_per) * TM_EXP
    pad_start = (starts + cnt).astype(jnp.int32)
    pad_len = (tiles_per * TM_EXP - cnt).astype(jnp.int32)

    route_idx = (e0, e1, r0, r1, starts.astype(jnp.int32))

    xs = _dispatch(route_idx, pad_start, pad_len, h1p, n_tiles * TM_EXP)
    shp = (N_EXPERTS, D_MODEL, D_EXPERT)
    ys = _expert_ffn(tile_expert, n_used, next_expert, slot, xs, exp_w_gate[l].reshape(shp),
                     exp_w_up[l].reshape(shp), exp_w_down[l].reshape(N_EXPERTS, D_EXPERT, D_MODEL))
    out = _combine(route_idx, h1, info, row(ln2_g[l]), row(ln2_b[l]), ys)
    return out.reshape(bsz, seq, d)
```

```python
import functools
import math

import jax
import jax.numpy as jnp
from jax import lax
from jax.experimental import pallas as pl
from jax.experimental.pallas import tpu as pltpu

F32 = jnp.float32
BF16 = jnp.bfloat16

D_MODEL = 2048
D_LRU = 1024
D_CONV = 1024
LRU_HEADS = 16
LRU_HEAD_DIM = 64
LRU_C = 8.0
LRU_CONV_W = 4
CONF_CONV_W = 31
N_GROUPS = 4
EXPERTS_PER_GROUP = 8
N_EXPERTS = N_GROUPS * EXPERTS_PER_GROUP
D_EXPERT = 512
LN_EPS = 1e-5
DEPTH = 1
DN_ALPHA = (2 * DEPTH) ** 0.25

LANES = 128
SUBLANES = 8
VMEM_LIMIT = 56 * 1024 * 1024

TM_WIN = 512
TN_WIN = 1024
W_CHUNK = 512
TT_LRU = 512
CB_LRU = 256
TT_CONF = 256
CONF_HALO = 32
TM_OUT = 512
TM_ROUTE = 512
TM_DISP = 256
TM_EXP = 256
TM_COMB = 256
ROUTE_W = 128


def _sigmoid(x):
    return 0.5 * (jnp.tanh(0.5 * x) + 1.0)


def _ln_rows(x, g, b):
    mu = jnp.mean(x, axis=-1, keepdims=True)
    xc = x - mu
    var = jnp.mean(xc * xc, axis=-1, keepdims=True)
    return xc * lax.rsqrt(var + LN_EPS) * g + b


def _stage_weight(w_hbm, wb_ref, wst_ref, wsem):
    nchunk = wb_ref.shape[1] // W_CHUNK

    def chunk_copy(c):
        return pltpu.make_async_copy(w_hbm.at[:, pl.ds(c * W_CHUNK, W_CHUNK)],
                                     wst_ref.at[c % 2], wsem.at[c % 2])

    chunk_copy(0).start()
    for c in range(nchunk):
        if c + 1 < nchunk:
            chunk_copy(c + 1).start()
        chunk_copy(c).wait()
        wb_ref[:, c * W_CHUNK:(c + 1) * W_CHUNK] = wst_ref[c % 2].astype(BF16)


def _ln_win_kernel(x_ref, g_ref, b_ref, w_hbm, z_ref, wb_ref, wst_ref, xn_ref, wsem):
    s = pl.program_id(0)
    par = lax.rem(s, 2)

    @pl.when(s == 0)
    def _():
        xn_ref[1] = jnp.zeros(xn_ref.shape[1:], BF16)
        _stage_weight(w_hbm, wb_ref, wst_ref, wsem)

    rows = 128
    for c in range(TM_WIN // rows):
        rs = slice(c * rows, (c + 1) * rows)
        xn_ref[par, rs, :] = _ln_rows(x_ref[rs, :], g_ref[...], b_ref[...]).astype(BF16)

    xprev = xn_ref[1 - par]
    for c in range(z_ref.shape[1] // TN_WIN):
        cs = slice(c * TN_WIN, (c + 1) * TN_WIN)
        z_ref[:, cs] = jnp.dot(xprev, wb_ref[:, cs], preferred_element_type=F32).astype(z_ref.dtype)


def _ln_win(x2, g, b, w):
    n = x2.shape[0]
    ncol = w.shape[1]
    nt = n // TM_WIN
    return pl.pallas_call(
        _ln_win_kernel,
        grid=(nt + 1,),
        in_specs=[
            pl.BlockSpec((TM_WIN, D_MODEL), lambda s: (jnp.minimum(s, nt - 1), 0)),
            pl.BlockSpec((1, D_MODEL), lambda s: (0, 0)),
            pl.BlockSpec((1, D_MODEL), lambda s: (0, 0)),
            pl.BlockSpec(memory_space=pl.ANY),
        ],
        out_specs=pl.BlockSpec((TM_WIN, ncol), lambda s: (jnp.maximum(s - 1, 0), 0)),
        out_shape=jax.ShapeDtypeStruct((n, ncol), BF16),
        scratch_shapes=[
            pltpu.VMEM((D_MODEL, ncol), BF16),
            pltpu.VMEM((2, D_MODEL, W_CHUNK), F32),
            pltpu.VMEM((2, TM_WIN, D_MODEL), BF16),
            pltpu.SemaphoreType.DMA((2,)),
        ],
        compiler_params=pltpu.CompilerParams(
            dimension_semantics=("arbitrary",),
            vmem_limit_bytes=VMEM_LIMIT),
        name="ln_win",
    )(x2, g, b, w)


def _lru_kernel(zx_ref, zg_ref, cw_ref, cb_ref, wcat_ref, ba_ref, bx_ref, lam_ref,
                o_ref, xs_ref, hp_ref, a_ref, g_ref):
    t = pl.program_id(2)
    tt = TT_LRU

    @pl.when(t == 0)
    def _():
        xs_ref[0:SUBLANES, :] = jnp.zeros((SUBLANES, CB_LRU), F32)
        hp_ref[...] = jnp.zeros_like(hp_ref)

    @pl.when(t > 0)
    def _():
        xs_ref[0:SUBLANES, :] = xs_ref[tt:tt + SUBLANES, :]

    xs_ref[SUBLANES:SUBLANES + tt, :] = zx_ref[...].astype(F32)

    rows = 128
    for rb in range(tt // rows):
        acc = jnp.broadcast_to(cb_ref[...], (rows, CB_LRU))
        for k in range(LRU_CONV_W):
            off = rb * rows + SUBLANES - (LRU_CONV_W - 1) + k
            acc = acc + cw_ref[k:k + 1, :] * xs_ref[off:off + rows, :]
        a_ref[rb * rows:(rb + 1) * rows, :] = acc

    g_ref[...] = jnp.dot(a_ref[...].astype(BF16), wcat_ref[0], preferred_element_type=F32)

    lam = lam_ref[...]
    softplus_neg = jnp.maximum(-lam, 0.0) + jnp.log1p(jnp.exp(-jnp.abs(lam)))
    cvec = -LRU_C * softplus_neg
    ba = ba_ref[...]
    bx = bx_ref[...]
    blk = 64
    row_in_vreg = lax.broadcasted_iota(jnp.int32, (blk, CB_LRU), 0) & (SUBLANES - 1)

    def body(rb, h):
        rs = pl.ds(pl.multiple_of(rb * blk, blk), blk)
        a_in = a_ref[rs, :]
        r = _sigmoid(g_ref[rs, 0:CB_LRU] + ba)
        i = _sigmoid(g_ref[rs, CB_LRU:2 * CB_LRU] + bx)
        log_a = cvec * r
        a = jnp.exp(log_a)
        u = jnp.sqrt(-jnp.tanh(log_a) * (a * a + 1.0)) * (i * a_in)
        for s in (1, 2, 4):
            m = row_in_vreg >= s
            a_sh = jnp.where(m, pltpu.roll(a, s, 0), 1.0)
            u_sh = jnp.where(m, pltpu.roll(u, s, 0), 0.0)
            u = u + a * u_sh
            a = a * a_sh
        outs = []
        for gi in range(blk // SUBLANES):
            ag = a[gi * SUBLANES:(gi + 1) * SUBLANES, :]
            ug = u[gi * SUBLANES:(gi + 1) * SUBLANES, :]
            hg = ug + ag * h
            h = hg[SUBLANES - 1:SUBLANES, :]
            outs.append(hg)
        hblk = jnp.concatenate(outs, axis=0)
        gl = zg_ref[rs, :].astype(F32)
        gelu = 0.5 * gl * (1.0 + jnp.tanh(0.7978845608028654 * (gl + 0.044715 * gl * gl * gl)))
        o_ref[rs, :] = (gelu * hblk).astype(o_ref.dtype)
        return h

    h = lax.fori_loop(0, tt // blk, body, hp_ref[0:1, :])
    hp_ref[...] = jnp.broadcast_to(h, hp_ref.shape)


def _lru_mixer(z, cw, cb, wcat, ba, bx, lam, bsz, seq):
    n = z.shape[0]
    nt = seq // TT_LRU
    ncb = D_LRU // CB_LRU
    row = lambda b, j, t: b * nt + t
    vec = pl.BlockSpec((1, CB_LRU), lambda b, j, t: (0, j))
    return pl.pallas_call(
        _lru_kernel,
        grid=(bsz, ncb, nt),
        in_specs=[
            pl.BlockSpec((TT_LRU, CB_LRU), lambda b, j, t: (row(b, j, t), j)),
            pl.BlockSpec((TT_LRU, CB_LRU), lambda b, j, t: (row(b, j, t), ncb + j)),
            pl.BlockSpec((LRU_CONV_W, CB_LRU), lambda b, j, t: (0, j)),
            vec,
            pl.BlockSpec((1, CB_LRU, 2 * CB_LRU), lambda b, j, t: (j, 0, 0)),
            vec, vec, vec,
        ],
        out_specs=pl.BlockSpec((TT_LRU, CB_LRU), lambda b, j, t: (row(b, j, t), j)),
        out_shape=jax.ShapeDtypeStruct((n, D_LRU), BF16),
        scratch_shapes=[
            pltpu.VMEM((TT_LRU + SUBLANES, CB_LRU), F32),
            pltpu.VMEM((SUBLANES, CB_LRU), F32),
            pltpu.VMEM((TT_LRU, CB_LRU), F32),
            pltpu.VMEM((TT_LRU, 2 * CB_LRU), F32),
        ],
        compiler_params=pltpu.CompilerParams(
            dimension_semantics=("arbitrary", "arbitrary", "arbitrary"),
            vmem_limit_bytes=VMEM_LIMIT),
        name="lru_mixer",
    )(z, z, cw, cb, wcat, ba, bx, lam)


def _conf_kernel(zv_ref, zg_ref, w_ref, cb_ref, lg_ref, lb_ref, o_ref, cs_ref, cv_ref):
    t = pl.program_id(1)
    tt = TT_CONF
    nlb = D_CONV // LANES

    @pl.when(t == 0)
    def _():
        cs_ref[:, 0:CONF_HALO, :] = jnp.zeros((nlb, CONF_HALO, LANES), F32)

    @pl.when(t > 0)
    def _():
        cs_ref[:, 0:CONF_HALO, :] = cs_ref[:, tt:tt + CONF_HALO, :]

    for c in range(nlb):
        ls = slice(c * LANES, (c + 1) * LANES)
        v = zv_ref[:, ls].astype(F32)
        g = zg_ref[:, ls].astype(F32)
        cs_ref[c, CONF_HALO:CONF_HALO + tt, :] = v * _sigmoid(g)

    rows = 64
    nrb = tt // rows
    base = CONF_HALO - (CONF_CONV_W - 1)

    def conv_body(c, carry):
        accs = [jnp.broadcast_to(cb_ref[c], (rows, LANES)) for _ in range(nrb)]
        for k in range(CONF_CONV_W):
            wk = w_ref[c, k:k + 1, :]
            for rb in range(nrb):
                off = rb * rows + base + k
                accs[rb] = accs[rb] + wk * cs_ref[c, off:off + rows, :]
        for rb in range(nrb):
            cv_ref[c, rb * rows:(rb + 1) * rows, :] = accs[rb]
        return carry

    lax.fori_loop(0, nlb, conv_body, 0)

    ln_rows = 32
    inv_n = 1.0 / D_CONV
    for rb in range(tt // ln_rows):
        rs = slice(rb * ln_rows, (rb + 1) * ln_rows)
        blk = cv_ref[:, rs, :]
        mu = jnp.sum(jnp.sum(blk, axis=0), axis=-1, keepdims=True) * inv_n
        d = blk - mu[None]
        var = jnp.sum(jnp.sum(d * d, axis=0), axis=-1, keepdims=True) * inv_n
        inv = lax.rsqrt(var + LN_EPS)
        for c in range(nlb):
            ls = slice(c * LANES, (c + 1) * LANES)
            y = d[c] * inv * lg_ref[:, ls] + lb_ref[:, ls]
            o_ref[rs, ls] = (y * _sigmoid(y)).astype(o_ref.dtype)


def _conf_mixer(z, w3, cb3, lg, lb, bsz, seq):
    n = z.shape[0]
    nt = seq // TT_CONF
    nlb = D_CONV // LANES
    return pl.pallas_call(
        _conf_kernel,
        grid=(bsz, nt),
        in_specs=[
            pl.BlockSpec((TT_CONF, D_CONV), lambda b, t: (b * nt + t, 2)),
            pl.BlockSpec((TT_CONF, D_CONV), lambda b, t: (b * nt + t, 3)),
            pl.BlockSpec((nlb, 32, LANES), lambda b, t: (0, 0, 0)),
            pl.BlockSpec((nlb, 1, LANES), lambda b, t: (0, 0, 0)),
            pl.BlockSpec((1, D_CONV), lambda b, t: (0, 0)),
            pl.BlockSpec((1, D_CONV), lambda b, t: (0, 0)),
        ],
        out_specs=pl.BlockSpec((TT_CONF, D_CONV), lambda b, t: (b * nt + t, 0)),
        out_shape=jax.ShapeDtypeStruct((n, D_CONV), BF16),
        scratch_shapes=[
            pltpu.VMEM((nlb, CONF_HALO + TT_CONF, LANES), F32),
            pltpu.VMEM((nlb, TT_CONF, LANES), F32),
        ],
        compiler_params=pltpu.CompilerParams(
            dimension_semantics=("arbitrary", "arbitrary"),
            vmem_limit_bytes=VMEM_LIMIT),
        name="conf_mixer",
    )(z, z, w3, cb3, lg, lb)


def _split_bf16(v):
    hi = v.astype(BF16)
    lo = (v - hi.astype(F32)).astype(BF16)
    return hi, lo


RT = D_MODEL // LANES
PITCH = RT + SUBLANES


def _store_rows(dst_ref, row0, rows, v, stage_ref):
    for s in range(RT):
        stage_ref[pl.ds(s, rows, stride=PITCH), :] = v[:, s * LANES:(s + 1) * LANES]
    staged = stage_ref[0:rows * PITCH, :].reshape(rows, PITCH, LANES)
    dst_ref[row0:row0 + rows] = staged[:, 0:RT, :].astype(BF16)


def _load_rows(src_ref, row0, rows, stage_ref):
    tile = src_ref[row0:row0 + rows].astype(F32)
    tile = jnp.concatenate([tile, jnp.zeros((rows, PITCH - RT, LANES), F32)], axis=1)
    stage_ref[0:rows * PITCH, :] = tile.reshape(rows * PITCH, LANES)
    return [stage_ref[pl.ds(s, rows, stride=PITCH), :] for s in range(RT)]


def _wout_kernel(a_ref, b_ref, x_ref, gin_ref, bin_ref, wa_ref, wb_ref, g1_ref, b1_ref,
                 wr_ref, br_ref, h1_ref, h1r_ref, lg_ref, mix_ref, hl_ref, stage_ref):
    mix_ref[...] = (jnp.dot(a_ref[...], wa_ref[...], preferred_element_type=F32)
                    + jnp.dot(b_ref[...], wb_ref[...], preferred_element_type=F32))
    rows = 64
    for c in range(TM_OUT // rows):
        rs = slice(c * rows, (c + 1) * rows)
        h = _ln_rows(x_ref[rs, :], gin_ref[...], bin_ref[...])
        h1 = _ln_rows(DN_ALPHA * h + mix_ref[rs, :], g1_ref[...], b1_ref[...])
        h1_ref[rs, :] = h1
        _store_rows(h1r_ref, c * rows, rows, h1, stage_ref)
        hi, lo = _split_bf16(h1)
        hl_ref[rs, :] = hi
        hl_ref[TM_OUT + c * rows:TM_OUT + (c + 1) * rows, :] = lo
    p = jnp.dot(hl_ref[...], wr_ref[...], preferred_element_type=F32)
    lg_ref[...] = (p[0:TM_OUT, 0:ROUTE_W] + p[0:TM_OUT, ROUTE_W:2 * ROUTE_W]
                   + p[TM_OUT:2 * TM_OUT, 0:ROUTE_W] + br_ref[...])


def _wout_router(a, b, x2, gin, bin_, wa, wb, g1, b1, wr_cat, br):
    n = x2.shape[0]
    full = lambda shape: pl.BlockSpec(shape, lambda i: tuple(0 for _ in shape))
    return pl.pallas_call(
        _wout_kernel,
        grid=(n // TM_OUT,),
        in_specs=[
            pl.BlockSpec((TM_OUT, D_LRU), lambda i: (i, 0)),
            pl.BlockSpec((TM_OUT, D_CONV), lambda i: (i, 0)),
            pl.BlockSpec((TM_OUT, D_MODEL), lambda i: (i, 0)),
            full((1, D_MODEL)), full((1, D_MODEL)),
            full((D_LRU, D_MODEL)), full((D_CONV, D_MODEL)),
            full((1, D_MODEL)), full((1, D_MODEL)),
            full((D_MODEL, 2 * ROUTE_W)), full((1, ROUTE_W)),
        ],
        out_specs=[
            pl.BlockSpec((TM_OUT, D_MODEL), lambda i: (i, 0)),
            pl.BlockSpec((TM_OUT, RT, LANES), lambda i: (i, 0, 0)),
            pl.BlockSpec((TM_OUT, ROUTE_W), lambda i: (i, 0)),
        ],
        out_shape=[
            jax.ShapeDtypeStruct((n, D_MODEL), F32),
            jax.ShapeDtypeStruct((n, RT, LANES), BF16),
            jax.ShapeDtypeStruct((n, ROUTE_W), F32),
        ],
        scratch_shapes=[
            pltpu.VMEM((TM_OUT, D_MODEL), F32),
            pltpu.VMEM((2 * TM_OUT, D_MODEL), BF16),
            pltpu.VMEM((64 * PITCH, LANES), F32),
        ],
        compiler_params=pltpu.CompilerParams(
            dimension_semantics=("arbitrary",),
            vmem_limit_bytes=VMEM_LIMIT),
        name="wout_router",
    )(a, b, x2, gin, bin_, wa, wb, g1, b1, wr_cat, br)


def _route_kernel(lg_ref, info_ref, cnt_ref, run_ref, tri_ref):
    t = pl.program_id(0)
    tm = TM_ROUTE
    l = lg_ref[...]
    lane = lax.broadcasted_iota(jnp.int32, (tm, ROUTE_W), 1)
    neg = jnp.float32(-jnp.inf)
    big = jnp.int32(1 << 20)

    gmask = lane < N_GROUPS
    gmax = jnp.max(jnp.where(gmask, l, neg), axis=-1, keepdims=True)
    gsel = jnp.min(jnp.where(gmask & (l == gmax), lane, big), axis=-1, keepdims=True)
    gsum = jnp.sum(jnp.where(gmask, jnp.exp(l - gmax), 0.0), axis=-1, keepdims=True)
    pg_top = 1.0 / gsum

    lo = N_GROUPS + EXPERTS_PER_GROUP * gsel
    emask = (lane >= lo) & (lane < lo + EXPERTS_PER_GROUP)
    v1 = jnp.max(jnp.where(emask, l, neg), axis=-1, keepdims=True)
    i1 = jnp.min(jnp.where(emask & (l == v1), lane, big), axis=-1, keepdims=True)
    emask2 = emask & (lane != i1)
    v2 = jnp.max(jnp.where(emask2, l, neg), axis=-1, keepdims=True)
    i2 = jnp.min(jnp.where(emask2 & (l == v2), lane, big), axis=-1, keepdims=True)
    e21 = jnp.exp(v2 - v1)
    q1 = pg_top / (1.0 + e21)
    q2 = pg_top * e21 / (1.0 + e21)

    oh1 = (lane == i1).astype(F32)
    oh2 = (lane == i2).astype(F32)
    ohs = oh1 + oh2

    @pl.when(t == 0)
    def _():
        run_ref[...] = jnp.zeros_like(run_ref)
        r_i = lax.broadcasted_iota(jnp.int32, (tm, tm), 0)
        c_i = lax.broadcasted_iota(jnp.int32, (tm, tm), 1)
        tri_ref[...] = (c_i < r_i).astype(BF16)

    cum = jnp.dot(tri_ref[...], ohs.astype(BF16), preferred_element_type=F32)
    basev = run_ref[0:1, :] + cum
    r1 = jnp.sum(oh1 * basev, axis=-1, keepdims=True)
    r2 = jnp.sum(oh2 * basev, axis=-1, keepdims=True)
    run_ref[...] = run_ref[...] + jnp.sum(ohs, axis=0, keepdims=True)
    info = jnp.where(lane == 0, r1, 0.0)
    info = jnp.where(lane == 1, r2, info)
    info = jnp.where(lane == 2, q1, info)
    info = jnp.where(lane == 3, q2, info)
    info = jnp.where(lane == 4, (i1 - N_GROUPS).astype(F32), info)
    info = jnp.where(lane == 5, (i2 - N_GROUPS).astype(F32), info)
    info_ref[...] = info
    cnt_ref[...] = run_ref[...]


def _route(logits):
    n = logits.shape[0]
    nt = n // TM_ROUTE
    return pl.pallas_call(
        _route_kernel,
        grid=(nt,),
        in_specs=[pl.BlockSpec((TM_ROUTE, ROUTE_W), lambda t: (t, 0))],
        out_specs=[
            pl.BlockSpec((TM_ROUTE, ROUTE_W), lambda t: (t, 0)),
            pl.BlockSpec((SUBLANES, ROUTE_W), lambda t: (0, 0)),
        ],
        out_shape=[
            jax.ShapeDtypeStruct((n, ROUTE_W), F32),
            jax.ShapeDtypeStruct((SUBLANES, ROUTE_W), F32),
        ],
        scratch_shapes=[
            pltpu.VMEM((SUBLANES, ROUTE_W), F32),
            pltpu.VMEM((TM_ROUTE, TM_ROUTE), BF16),
        ],
        compiler_params=pltpu.CompilerParams(
            dimension_semantics=("arbitrary",),
            vmem_limit_bytes=VMEM_LIMIT),
        name="route",
    )(logits)


def _prow_copy(src_ref, src_row, dst_ref, dst_row, sem, rows=1):
    return pltpu.make_async_copy(src_ref.at[pl.ds(src_row, rows)], dst_ref.at[pl.ds(dst_row, rows)], sem)


ZERO_ROWS = TM_EXP // 2
DMA_UNROLL = 8


def _pad_fill(e, ps_ref, pl_ref, zbuf_ref, xs_ref, zsem, wait):
    ln = pl_ref[e]
    st = ps_ref[e]
    b = 1
    while b <= ZERO_ROWS:
        @pl.when((ln & b) != 0)
        def _(b=b):
            cp = _prow_copy(zbuf_ref, 0, xs_ref, st + (ln & (b - 1)), zsem, rows=b)
            if wait:
                cp.wait()
            else:
                cp.start()
        b *= 2


def _dispatch_kernel(pos0_ref, pos1_ref, ps_ref, pl_ref, h1p_ref, xs_ref, zbuf_ref, sem, zsem):
    i = pl.program_id(0)
    tm = TM_DISP

    @pl.when(i == 0)
    def _():
        zbuf_ref[...] = jnp.zeros_like(zbuf_ref)

        def fill_start(e, _):
            _pad_fill(e, ps_ref, pl_ref, zbuf_ref, xs_ref, zsem, False)
            return 0

        lax.fori_loop(0, N_EXPERTS, fill_start, 0)

    def issue(k, _):
        for u in range(DMA_UNROLL):
            r = k * DMA_UNROLL + u
            tok = i * tm + r
            _prow_copy(h1p_ref, r, xs_ref, pos0_ref[tok], sem).start(priority=0)
            _prow_copy(h1p_ref, r, xs_ref, pos1_ref[tok], sem).start(priority=1)
        return 0

    lax.fori_loop(0, tm // DMA_UNROLL, issue, 0)

    def drain(k, _):
        for u in range(DMA_UNROLL):
            _prow_copy(h1p_ref, 0, xs_ref, 0, sem).wait()
            _prow_copy(h1p_ref, 0, xs_ref, 0, sem).wait()
        return 0

    lax.fori_loop(0, tm // DMA_UNROLL, drain, 0)

    @pl.when(i == pl.num_programs(0) - 1)
    def _():
        def fill_wait(e, _):
            _pad_fill(e, ps_ref, pl_ref, zbuf_ref, xs_ref, zsem, True)
            return 0

        lax.fori_loop(0, N_EXPERTS, fill_wait, 0)


def _dispatch(pos0, pos1, pad_start, pad_len, h1p, n_rows):
    n = h1p.shape[0]
    return pl.pallas_call(
        _dispatch_kernel,
        grid_spec=pltpu.PrefetchScalarGridSpec(
            num_scalar_prefetch=4,
            grid=(n // TM_DISP,),
            in_specs=[pl.BlockSpec((TM_DISP, RT, LANES), lambda i, *_: (i, 0, 0))],
            out_specs=pl.BlockSpec(memory_space=pl.ANY),
            scratch_shapes=[
                pltpu.VMEM((ZERO_ROWS, RT, LANES), BF16),
                pltpu.SemaphoreType.DMA(()),
                pltpu.SemaphoreType.DMA(()),
            ],
        ),
        out_shape=jax.ShapeDtypeStruct((n_rows, RT, LANES), BF16),
        compiler_params=pltpu.CompilerParams(
            dimension_semantics=("arbitrary",),
            vmem_limit_bytes=VMEM_LIMIT),
        name="dispatch",
    )(pos0, pos1, pad_start, pad_len, h1p)


def _expert_kernel(te_ref, nu_ref, nx_ref, sl_ref, xs_ref, wg_hbm, wu_hbm, wd_hbm, ys_ref,
                   wgf_ref, wuf_ref, wdf_ref, wgb_ref, wub_ref, wdb_ref, stage_ref, wsem):
    i = pl.program_id(0)
    used = i < nu_ref[0]
    e = te_ref[i]
    s = sl_ref[i]
    fresh = (i == 0) | (e != te_ref[jnp.maximum(i - 1, 0)])

    def weight_copies(expert, slot):
        return (pltpu.make_async_copy(wg_hbm.at[expert], wgf_ref.at[slot], wsem.at[slot, 0]),
                pltpu.make_async_copy(wu_hbm.at[expert], wuf_ref.at[slot], wsem.at[slot, 1]),
                pltpu.make_async_copy(wd_hbm.at[expert], wdf_ref.at[slot], wsem.at[slot, 2]))

    @pl.when(i == 0)
    def _():
        for cp in weight_copies(e, s):
            cp.start()

    @pl.when(used & fresh)
    def _():
        for cp in weight_copies(e, s):
            cp.wait()

        @pl.when(nx_ref[i] >= 0)
        def _():
            for cp in weight_copies(nx_ref[i], 1 - s):
                cp.start()

        wgb_ref[...] = wgf_ref[s].astype(BF16)
        wub_ref[...] = wuf_ref[s].astype(BF16)
        wdb_ref[...] = wdf_ref[s].astype(BF16)

    @pl.when(used)
    def _():
        x = jnp.concatenate([p.astype(BF16) for p in _load_rows(xs_ref, 0, TM_EXP, stage_ref)],
                            axis=1)
        g = jnp.dot(x, wgb_ref[...], preferred_element_type=F32)
        u = jnp.dot(x, wub_ref[...], preferred_element_type=F32)
        h = (g * _sigmoid(g) * u).astype(BF16)
        y = jnp.dot(h, wdb_ref[...], preferred_element_type=F32)
        _store_rows(ys_ref, 0, TM_EXP, y, stage_ref)


def _expert_ffn(tile_expert, n_used, next_expert, slot, xs, wg, wu, wd):
    n_rows = xs.shape[0]
    n_tiles = n_rows // TM_EXP

    def row_map(i, te, nu, nx, sl):
        return (jnp.minimum(i, nu[0] - 1), 0, 0)

    return pl.pallas_call(
        _expert_kernel,
        grid_spec=pltpu.PrefetchScalarGridSpec(
            num_scalar_prefetch=4,
            grid=(n_tiles,),
            in_specs=[
                pl.BlockSpec((TM_EXP, RT, LANES), row_map),
                pl.BlockSpec(memory_space=pl.ANY),
                pl.BlockSpec(memory_space=pl.ANY),
                pl.BlockSpec(memory_space=pl.ANY),
            ],
            out_specs=pl.BlockSpec((TM_EXP, RT, LANES), row_map),
            scratch_shapes=[
                pltpu.VMEM((2, D_MODEL, D_EXPERT), F32),
                pltpu.VMEM((2, D_MODEL, D_EXPERT), F32),
                pltpu.VMEM((2, D_EXPERT, D_MODEL), F32),
                pltpu.VMEM((D_MODEL, D_EXPERT), BF16),
                pltpu.VMEM((D_MODEL, D_EXPERT), BF16),
                pltpu.VMEM((D_EXPERT, D_MODEL), BF16),
                pltpu.VMEM((TM_EXP * PITCH, LANES), F32),
                pltpu.SemaphoreType.DMA((2, 3)),
            ],
        ),
        out_shape=jax.ShapeDtypeStruct((n_rows, RT, LANES), BF16),
        compiler_params=pltpu.CompilerParams(
            dimension_semantics=("arbitrary",),
            vmem_limit_bytes=VMEM_LIMIT),
        name="expert_ffn",
    )(tile_expert, n_used, next_expert, slot, xs, wg, wu, wd)


def _combine_kernel(pos0_ref, pos1_ref, h1_ref, info_ref, g2_ref, b2_ref, ys_ref, o_ref,
                    ybuf_ref, stage_ref, sem):
    i = pl.program_id(0)
    tm = TM_COMB
    par = lax.rem(i, 2)

    def issue(step, parity):
        def body(k, _):
            for u in range(DMA_UNROLL):
                r = k * DMA_UNROLL + u
                tok = step * tm + r
                _prow_copy(ys_ref, pos0_ref[tok], ybuf_ref.at[parity, 0], r,
                           sem.at[parity]).start(priority=0)
                _prow_copy(ys_ref, pos1_ref[tok], ybuf_ref.at[parity, 1], r,
                           sem.at[parity]).start(priority=1)
            return 0
        lax.fori_loop(0, tm // DMA_UNROLL, body, 0)

    def drain(parity):
        def body(k, _):
            for u in range(DMA_UNROLL):
                _prow_copy(ys_ref, 0, ybuf_ref.at[parity, 0], 0, sem.at[parity]).wait()
                _prow_copy(ys_ref, 0, ybuf_ref.at[parity, 1], 0, sem.at[parity]).wait()
            return 0
        lax.fori_loop(0, tm // DMA_UNROLL, body, 0)

    @pl.when(i == 0)
    def _():
        issue(0, 0)

    @pl.when(i + 1 < pl.num_programs(0))
    def _():
        issue(i + 1, 1 - par)

    drain(par)

    rows = 64
    for c in range(tm // rows):
        rs = slice(c * rows, (c + 1) * rows)
        q1 = info_ref[rs, 2:3]
        q2 = info_ref[rs, 3:4]
        y0 = _load_rows(ybuf_ref.at[par, 0], c * rows, rows, stage_ref)
        y1 = _load_rows(ybuf_ref.at[par, 1], c * rows, rows, stage_ref)
        ffn = jnp.concatenate([q1 * a + q2 * b for a, b in zip(y0, y1)], axis=1)
        o_ref[rs, :] = _ln_rows(DN_ALPHA * h1_ref[rs, :] + ffn, g2_ref[...], b2_ref[...])


def _combine(pos0, pos1, h1, info, g2, b2, ys):
    n = h1.shape[0]
    return pl.pallas_call(
        _combine_kernel,
        grid_spec=pltpu.PrefetchScalarGridSpec(
            num_scalar_prefetch=2,
            grid=(n // TM_COMB,),
            in_specs=[
                pl.BlockSpec((TM_COMB, D_MODEL), lambda i, *_: (i, 0)),
                pl.BlockSpec((TM_COMB, ROUTE_W), lambda i, *_: (i, 0)),
                pl.BlockSpec((1, D_MODEL), lambda i, *_: (0, 0)),
                pl.BlockSpec((1, D_MODEL), lambda i, *_: (0, 0)),
                pl.BlockSpec(memory_space=pl.ANY),
            ],
            out_specs=pl.BlockSpec((TM_COMB, D_MODEL), lambda i, *_: (i, 0)),
            scratch_shapes=[
                pltpu.VMEM((2, 2, TM_COMB, RT, LANES), BF16),
                pltpu.VMEM((64 * PITCH, LANES), F32),
                pltpu.SemaphoreType.DMA((2,)),
            ],
        ),
        out_shape=jax.ShapeDtypeStruct((n, D_MODEL), F32),
        compiler_params=pltpu.CompilerParams(
            dimension_semantics=("arbitrary",),
            vmem_limit_bytes=VMEM_LIMIT),
        name="combine",
    )(pos0, pos1, h1, info, g2, b2, ys)


def _block_diag(w, per):
    h, hd, _ = w.shape
    wg = w.reshape(h // per, per, hd, hd)
    eye = jnp.eye(per, dtype=w.dtype)
    return jnp.einsum("gpij,pq->gpiqj", wg, eye).reshape(h // per, per * hd, per * hd)


def kernel(x, ln_in_g, ln_in_b, w_in, lru_conv_w, lru_conv_b, lru_w_a, lru_b_a, lru_w_x, lru_b_x,
           lru_lambda, conf_conv_w, conf_conv_b, conf_ln_g, conf_ln_b, w_out, ln1_g, ln1_b,
           router_group_w, router_group_b, router_expert_w, router_expert_b, exp_w_gate, exp_w_up,
           exp_w_down, ln2_g, ln2_b):
    bsz, seq, d = x.shape
    n = bsz * seq
    x2 = x.reshape(n, d)
    row = lambda v: v.reshape(1, -1).astype(F32)
    l = 0

    z = _ln_win(x2, row(ln_in_g), row(ln_in_b), w_in[l])

    per = CB_LRU // LRU_HEAD_DIM
    wcat = jnp.concatenate([_block_diag(lru_w_a[l], per), _block_diag(lru_w_x[l], per)],
                           axis=-1).astype(BF16)
    a_out = _lru_mixer(z, lru_conv_w[l], row(lru_conv_b[l]), wcat, row(lru_b_a[l]),
                       row(lru_b_x[l]), row(lru_lambda[l]), bsz, seq)

    nlb = D_CONV // LANES
    w3 = jnp.pad(conf_conv_w[l], ((0, 32 - CONF_CONV_W), (0, 0)))
    w3 = w3.reshape(32, nlb, LANES).transpose(1, 0, 2)
    cb3 = conf_conv_b[l].reshape(nlb, 1, LANES)
    b_out = _conf_mixer(z, w3, cb3, row(conf_ln_g[l]), row(conf_ln_b[l]), bsz, seq)

    wr = jnp.concatenate([router_group_w[l], router_expert_w[l]], axis=1)
    wr = jnp.pad(wr, ((0, 0), (0, ROUTE_W - wr.shape[1])))
    wr_hi = wr.astype(BF16)
    wr_lo = (wr - wr_hi.astype(F32)).astype(BF16)
    wr_cat = jnp.concatenate([wr_hi, wr_lo], axis=1)
    br = jnp.concatenate([router_group_b[l], router_expert_b[l]])
    br = jnp.pad(br, (0, ROUTE_W - br.shape[0])).reshape(1, ROUTE_W)
    wo = w_out[l].astype(BF16)
    h1, h1p, logits = _wout_router(a_out, b_out, x2, row(ln_in_g), row(ln_in_b), wo[:D_LRU],
                                   wo[D_LRU:], row(ln1_g[l]), row(ln1_b[l]), wr_cat, br)

    info, counts = _route(logits)
    idx = info[:, 0:6].astype(jnp.int32)
    r0, r1, e0, e1 = idx[:, 0], idx[:, 1], idx[:, 4], idx[:, 5]

    n_tiles = (n * 2) // TM_EXP + N_EXPERTS
    cnt = counts[0, N_GROUPS:N_GROUPS + N_EXPERTS].astype(jnp.int32)
    tiles_per = (cnt + TM_EXP - 1) // TM_EXP
    ends = jnp.cumsum(tiles_per)
    n_used = ends[-1:].astype(jnp.int32)
    tile_ids = jnp.arange(n_tiles, dtype=jnp.int32)
    tile_expert = jnp.minimum(jnp.sum(tile_ids[:, None] >= ends[None, :], axis=1),
                              N_EXPERTS - 1).astype(jnp.int32)
    last = tile_expert[jnp.maximum(n_used[0] - 1, 0)]
    tile_expert = jnp.where(tile_ids < n_used[0], tile_expert, last)

    nxt_tile = ends[tile_expert]
    next_expert = jnp.where(nxt_tile < n_used[0],
                            tile_expert[jnp.minimum(nxt_tile, n_tiles - 1)], -1).astype(jnp.int32)
    changes = jnp.concatenate([jnp.zeros((1,), jnp.int32),
                               (tile_expert[1:] != tile_expert[:-1]).astype(jnp.int32)])
    slot = (jnp.cumsum(changes) & 1).astype(jnp.int32)
    starts = (ends - tiles_per) * TM_EXP
    pad_start = (starts + cnt).astype(jnp.int32)
    pad_len = (tiles_per * TM_EXP - cnt).astype(jnp.int32)

    expert_ids = jnp.arange(N_EXPERTS, dtype=jnp.int32)[None, :]
    start_of = lambda e: jnp.sum(jnp.where(e[:, None] == expert_ids, starts[None, :], 0), axis=1)
    pos0 = (start_of(e0) + r0).astype(jnp.int32)
    pos1 = (start_of(e1) + r1).astype(jnp.int32)

    xs = _dispatch(pos0, pos1, pad_start, pad_len, h1p, n_tiles * TM_EXP)
    shp = (N_EXPERTS, D_MODEL, D_EXPERT)
    ys = _expert_ffn(tile_expert, n_used, next_expert, slot, xs, exp_w_gate[l].reshape(shp),
                     exp_w_up[l].reshape(shp), exp_w_down[l].reshape(N_EXPERTS, D_EXPERT, D_MODEL))
    out = _combine(pos0, pos1, h1, info, row(ln2_g[l]), row(ln2_b[l]), ys)
    return out.reshape(bsz, seq, d)
```

```python
import functools
import math

import jax
import jax.numpy as jnp
from jax import lax
from jax.experimental import pallas as pl
from jax.experimental.pallas import tpu as pltpu

F32 = jnp.float32
BF16 = jnp.bfloat16

D_MODEL = 2048
D_LRU = 1024
D_CONV = 1024
LRU_HEADS = 16
LRU_HEAD_DIM = 64
LRU_C = 8.0
LRU_CONV_W = 4
CONF_CONV_W = 31
N_GROUPS = 4
EXPERTS_PER_GROUP = 8
N_EXPERTS = N_GROUPS * EXPERTS_PER_GROUP
D_EXPERT = 512
LN_EPS = 1e-5
DEPTH = 1
DN_ALPHA = (2 * DEPTH) ** 0.25

LANES = 128
SUBLANES = 8
VMEM_LIMIT = 56 * 1024 * 1024

TM_WIN = 512
TN_WIN = 1024
W_CHUNK = 256
TT_LRU = 512
CB_LRU = 256
TT_CONF = 256
CONF_HALO = 32
TM_OUT = 512
TM_ROUTE = 512
TM_DISP = 512
TM_EXP = 256
TM_COMB = 512
ROUTE_W = 128


def _sigmoid(x):
    return 0.5 * (jnp.tanh(0.5 * x) + 1.0)


def _ln_rows(x, g, b):
    mu = jnp.mean(x, axis=-1, keepdims=True)
    xc = x - mu
    var = jnp.mean(xc * xc, axis=-1, keepdims=True)
    return xc * lax.rsqrt(var + LN_EPS) * g + b


def _stage_weight(w_hbm, wb_ref, wst_ref, wsem):
    nchunk = wb_ref.shape[1] // W_CHUNK

    def chunk_copy(c):
        return pltpu.make_async_copy(w_hbm.at[:, pl.ds(c * W_CHUNK, W_CHUNK)],
                                     wst_ref.at[c % 2], wsem.at[c % 2])

    chunk_copy(0).start()
    for c in range(nchunk):
        if c + 1 < nchunk:
            chunk_copy(c + 1).start()
        chunk_copy(c).wait()
        wb_ref[:, c * W_CHUNK:(c + 1) * W_CHUNK] = wst_ref[c % 2].astype(BF16)


def _ln_win_kernel(x_ref, g_ref, b_ref, w_hbm, z_ref, h_ref, wb_ref, wst_ref, xn_ref, wsem):
    s = pl.program_id(0)
    par = lax.rem(s, 2)

    @pl.when(s == 0)
    def _():
        xn_ref[1] = jnp.zeros(xn_ref.shape[1:], BF16)
        _stage_weight(w_hbm, wb_ref, wst_ref, wsem)

    rows = 128
    for c in range(TM_WIN // rows):
        rs = slice(c * rows, (c + 1) * rows)
        hn = _ln_rows(x_ref[rs, :], g_ref[...], b_ref[...])
        h_ref[rs, :] = hn
        xn_ref[par, rs, :] = hn.astype(BF16)

    xprev = xn_ref[1 - par]
    for c in range(z_ref.shape[1] // TN_WIN):
        cs = slice(c * TN_WIN, (c + 1) * TN_WIN)
        z_ref[:, cs] = jnp.dot(xprev, wb_ref[:, cs], preferred_element_type=F32).astype(z_ref.dtype)


def _ln_win(x2, g, b, w):
    n = x2.shape[0]
    ncol = w.shape[1]
    nt = n // TM_WIN
    return pl.pallas_call(
        _ln_win_kernel,
        grid=(nt + 1,),
        in_specs=[
            pl.BlockSpec((TM_WIN, D_MODEL), lambda s: (jnp.minimum(s, nt - 1), 0)),
            pl.BlockSpec((1, D_MODEL), lambda s: (0, 0)),
            pl.BlockSpec((1, D_MODEL), lambda s: (0, 0)),
            pl.BlockSpec(memory_space=pl.ANY),
        ],
        out_specs=[
            pl.BlockSpec((TM_WIN, ncol), lambda s: (jnp.maximum(s - 1, 0), 0)),
            pl.BlockSpec((TM_WIN, D_MODEL), lambda s: (jnp.minimum(s, nt - 1), 0)),
        ],
        out_shape=[
            jax.ShapeDtypeStruct((n, ncol), BF16),
            jax.ShapeDtypeStruct((n, D_MODEL), F32),
        ],
        scratch_shapes=[
            pltpu.VMEM((D_MODEL, ncol), BF16),
            pltpu.VMEM((2, D_MODEL, W_CHUNK), F32),
            pltpu.VMEM((2, TM_WIN, D_MODEL), BF16),
            pltpu.SemaphoreType.DMA((2,)),
        ],
        compiler_params=pltpu.CompilerParams(
            dimension_semantics=("arbitrary",),
            vmem_limit_bytes=VMEM_LIMIT),
        name="ln_win",
    )(x2, g, b, w)


def _lru_kernel(zx_ref, zg_ref, cw_ref, cb_ref, wcat_ref, ba_ref, bx_ref, lam_ref,
                o_ref, xs_ref, hp_ref, a_ref, g_ref):
    t = pl.program_id(2)
    tt = TT_LRU

    @pl.when(t == 0)
    def _():
        xs_ref[0:SUBLANES, :] = jnp.zeros((SUBLANES, CB_LRU), F32)
        hp_ref[...] = jnp.zeros_like(hp_ref)

    @pl.when(t > 0)
    def _():
        xs_ref[0:SUBLANES, :] = xs_ref[tt:tt + SUBLANES, :]

    xs_ref[SUBLANES:SUBLANES + tt, :] = zx_ref[...].astype(F32)

    rows = 128
    for rb in range(tt // rows):
        acc = jnp.broadcast_to(cb_ref[...], (rows, CB_LRU))
        for k in range(LRU_CONV_W):
            off = rb * rows + SUBLANES - (LRU_CONV_W - 1) + k
            acc = acc + cw_ref[k:k + 1, :] * xs_ref[off:off + rows, :]
        a_ref[rb * rows:(rb + 1) * rows, :] = acc

    g_ref[...] = jnp.dot(a_ref[...].astype(BF16), wcat_ref[0], preferred_element_type=F32)

    lam = lam_ref[...]
    softplus_neg = jnp.maximum(-lam, 0.0) + jnp.log1p(jnp.exp(-jnp.abs(lam)))
    cvec = -LRU_C * softplus_neg
    ba = ba_ref[...]
    bx = bx_ref[...]
    blk = 64
    row_in_vreg = lax.broadcasted_iota(jnp.int32, (blk, CB_LRU), 0) & (SUBLANES - 1)

    def body(rb, h):
        rs = pl.ds(pl.multiple_of(rb * blk, blk), blk)
        a_in = a_ref[rs, :]
        r = _sigmoid(g_ref[rs, 0:CB_LRU] + ba)
        i = _sigmoid(g_ref[rs, CB_LRU:2 * CB_LRU] + bx)
        log_a = cvec * r
        a = jnp.exp(log_a)
        u = jnp.sqrt(-jnp.tanh(log_a) * (a * a + 1.0)) * (i * a_in)
        for s in (1, 2, 4):
            m = row_in_vreg >= s
            a_sh = jnp.where(m, pltpu.roll(a, s, 0), 1.0)
            u_sh = jnp.where(m, pltpu.roll(u, s, 0), 0.0)
            u = u + a * u_sh
            a = a * a_sh
        outs = []
        for gi in range(blk // SUBLANES):
            ag = a[gi * SUBLANES:(gi + 1) * SUBLANES, :]
            ug = u[gi * SUBLANES:(gi + 1) * SUBLANES, :]
            hg = ug + ag * h
            h = hg[SUBLANES - 1:SUBLANES, :]
            outs.append(hg)
        hblk = jnp.concatenate(outs, axis=0)
        gl = zg_ref[rs, :].astype(F32)
        gelu = 0.5 * gl * (1.0 + jnp.tanh(0.7978845608028654 * (gl + 0.044715 * gl * gl * gl)))
        o_ref[rs, :] = (gelu * hblk).astype(o_ref.dtype)
        return h

    h = lax.fori_loop(0, tt // blk, body, hp_ref[0:1, :])
    hp_ref[...] = jnp.broadcast_to(h, hp_ref.shape)


def _lru_mixer(z, cw, cb, wcat, ba, bx, lam, bsz, seq):
    n = z.shape[0]
    nt = seq // TT_LRU
    ncb = D_LRU // CB_LRU
    row = lambda b, j, t: b * nt + t
    vec = pl.BlockSpec((1, CB_LRU), lambda b, j, t: (0, j))
    return pl.pallas_call(
        _lru_kernel,
        grid=(bsz, ncb, nt),
        in_specs=[
            pl.BlockSpec((TT_LRU, CB_LRU), lambda b, j, t: (row(b, j, t), j)),
            pl.BlockSpec((TT_LRU, CB_LRU), lambda b, j, t: (row(b, j, t), ncb + j)),
            pl.BlockSpec((LRU_CONV_W, CB_LRU), lambda b, j, t: (0, j)),
            vec,
            pl.BlockSpec((1, CB_LRU, 2 * CB_LRU), lambda b, j, t: (j, 0, 0)),
            vec, vec, vec,
        ],
        out_specs=pl.BlockSpec((TT_LRU, CB_LRU), lambda b, j, t: (row(b, j, t), j)),
        out_shape=jax.ShapeDtypeStruct((n, D_LRU), BF16),
        scratch_shapes=[
            pltpu.VMEM((TT_LRU + SUBLANES, CB_LRU), F32),
            pltpu.VMEM((SUBLANES, CB_LRU), F32),
            pltpu.VMEM((TT_LRU, CB_LRU), F32),
            pltpu.VMEM((TT_LRU, 2 * CB_LRU), F32),
        ],
        compiler_params=pltpu.CompilerParams(
            dimension_semantics=("arbitrary", "arbitrary", "arbitrary"),
            vmem_limit_bytes=VMEM_LIMIT),
        name="lru_mixer",
    )(z, z, cw, cb, wcat, ba, bx, lam)


def _conf_kernel(zv_ref, zg_ref, w_ref, cb_ref, lg_ref, lb_ref, o_ref, cs_ref, cv_ref):
    t = pl.program_id(1)
    tt = TT_CONF
    nlb = D_CONV // LANES

    @pl.when(t == 0)
    def _():
        cs_ref[:, 0:CONF_HALO, :] = jnp.zeros((nlb, CONF_HALO, LANES), F32)

    @pl.when(t > 0)
    def _():
        cs_ref[:, 0:CONF_HALO, :] = cs_ref[:, tt:tt + CONF_HALO, :]

    for c in range(nlb):
        ls = slice(c * LANES, (c + 1) * LANES)
        v = zv_ref[:, ls].astype(F32)
        g = zg_ref[:, ls].astype(F32)
        cs_ref[c, CONF_HALO:CONF_HALO + tt, :] = v * _sigmoid(g)

    rows = 64
    nrb = tt // rows
    base = CONF_HALO - (CONF_CONV_W - 1)

    def conv_body(c, carry):
        accs = [jnp.broadcast_to(cb_ref[c], (rows, LANES)) for _ in range(nrb)]
        for k in range(CONF_CONV_W):
            wk = w_ref[c, k:k + 1, :]
            for rb in range(nrb):
                off = rb * rows + base + k
                accs[rb] = accs[rb] + wk * cs_ref[c, off:off + rows, :]
        for rb in range(nrb):
            cv_ref[c, rb * rows:(rb + 1) * rows, :] = accs[rb]
        return carry

    lax.fori_loop(0, nlb, conv_body, 0)

    ln_rows = 32
    inv_n = 1.0 / D_CONV
    for rb in range(tt // ln_rows):
        rs = slice(rb * ln_rows, (rb + 1) * ln_rows)
        blk = cv_ref[:, rs, :]
        mu = jnp.sum(jnp.sum(blk, axis=0), axis=-1, keepdims=True) * inv_n
        d = blk - mu[None]
        var = jnp.sum(jnp.sum(d * d, axis=0), axis=-1, keepdims=True) * inv_n
        inv = lax.rsqrt(var + LN_EPS)
        for c in range(nlb):
            ls = slice(c * LANES, (c + 1) * LANES)
            y = d[c] * inv * lg_ref[:, ls] + lb_ref[:, ls]
            o_ref[rs, ls] = (y * _sigmoid(y)).astype(o_ref.dtype)


def _conf_mixer(z, w3, cb3, lg, lb, bsz, seq):
    n = z.shape[0]
    nt = seq // TT_CONF
    nlb = D_CONV // LANES
    return pl.pallas_call(
        _conf_kernel,
        grid=(bsz, nt),
        in_specs=[
            pl.BlockSpec((TT_CONF, D_CONV), lambda b, t: (b * nt + t, 2)),
            pl.BlockSpec((TT_CONF, D_CONV), lambda b, t: (b * nt + t, 3)),
            pl.BlockSpec((nlb, 32, LANES), lambda b, t: (0, 0, 0)),
            pl.BlockSpec((nlb, 1, LANES), lambda b, t: (0, 0, 0)),
            pl.BlockSpec((1, D_CONV), lambda b, t: (0, 0)),
            pl.BlockSpec((1, D_CONV), lambda b, t: (0, 0)),
        ],
        out_specs=pl.BlockSpec((TT_CONF, D_CONV), lambda b, t: (b * nt + t, 0)),
        out_shape=jax.ShapeDtypeStruct((n, D_CONV), BF16),
        scratch_shapes=[
            pltpu.VMEM((nlb, CONF_HALO + TT_CONF, LANES), F32),
            pltpu.VMEM((nlb, TT_CONF, LANES), F32),
        ],
        compiler_params=pltpu.CompilerParams(
            dimension_semantics=("arbitrary", "arbitrary"),
            vmem_limit_bytes=VMEM_LIMIT),
        name="conf_mixer",
    )(z, z, w3, cb3, lg, lb)


def _split_bf16(v):
    hi = v.astype(BF16)
    lo = (v - hi.astype(F32)).astype(BF16)
    return hi, lo


RT = D_MODEL // LANES
PITCH = RT + SUBLANES


def _store_rows(dst_ref, row0, rows, v, stage_ref):
    for s in range(RT):
        stage_ref[pl.ds(s, rows, stride=PITCH), :] = v[:, s * LANES:(s + 1) * LANES]
    staged = stage_ref[0:rows * PITCH, :].reshape(rows, PITCH, LANES)
    dst_ref[row0:row0 + rows] = staged[:, 0:RT, :].astype(BF16)


def _load_rows(src_ref, row0, rows, stage_ref):
    tile = src_ref[row0:row0 + rows].astype(F32)
    tile = jnp.concatenate([tile, jnp.zeros((rows, PITCH - RT, LANES), F32)], axis=1)
    stage_ref[0:rows * PITCH, :] = tile.reshape(rows * PITCH, LANES)
    return [stage_ref[pl.ds(s, rows, stride=PITCH), :] for s in range(RT)]


def _wout_kernel(a_ref, b_ref, h_ref, wa_ref, wb_ref, g1_ref, b1_ref,
                 wr_ref, br_ref, h1_ref, h1r_ref, lg_ref, mix_ref, hl_ref, stage_ref):
    mix_ref[...] = (jnp.dot(a_ref[...], wa_ref[...], preferred_element_type=F32)
                    + jnp.dot(b_ref[...], wb_ref[...], preferred_element_type=F32))
    rows = 64
    for c in range(TM_OUT // rows):
        rs = slice(c * rows, (c + 1) * rows)
        h1 = _ln_rows(DN_ALPHA * h_ref[rs, :] + mix_ref[rs, :], g1_ref[...], b1_ref[...])
        h1_ref[rs, :] = h1
        _store_rows(h1r_ref, c * rows, rows, h1, stage_ref)
        hi, lo = _split_bf16(h1)
        hl_ref[rs, :] = hi
        hl_ref[TM_OUT + c * rows:TM_OUT + (c + 1) * rows, :] = lo
    p = jnp.dot(hl_ref[...], wr_ref[...], preferred_element_type=F32)
    lg_ref[...] = (p[0:TM_OUT, 0:ROUTE_W] + p[0:TM_OUT, ROUTE_W:2 * ROUTE_W]
                   + p[TM_OUT:2 * TM_OUT, 0:ROUTE_W] + br_ref[...])


def _wout_router(a, b, h, wa, wb, g1, b1, wr_cat, br):
    n = h.shape[0]
    full = lambda shape: pl.BlockSpec(shape, lambda i: tuple(0 for _ in shape))
    return pl.pallas_call(
        _wout_kernel,
        grid=(n // TM_OUT,),
        in_specs=[
            pl.BlockSpec((TM_OUT, D_LRU), lambda i: (i, 0)),
            pl.BlockSpec((TM_OUT, D_CONV), lambda i: (i, 0)),
            pl.BlockSpec((TM_OUT, D_MODEL), lambda i: (i, 0)),
            full((D_LRU, D_MODEL)), full((D_CONV, D_MODEL)),
            full((1, D_MODEL)), full((1, D_MODEL)),
            full((D_MODEL, 2 * ROUTE_W)), full((1, ROUTE_W)),
        ],
        out_specs=[
            pl.BlockSpec((TM_OUT, D_MODEL), lambda i: (i, 0)),
            pl.BlockSpec((TM_OUT, RT, LANES), lambda i: (i, 0, 0)),
            pl.BlockSpec((TM_OUT, ROUTE_W), lambda i: (i, 0)),
        ],
        out_shape=[
            jax.ShapeDtypeStruct((n, D_MODEL), F32),
            jax.ShapeDtypeStruct((n, RT, LANES), BF16),
            jax.ShapeDtypeStruct((n, ROUTE_W), F32),
        ],
        scratch_shapes=[
            pltpu.VMEM((TM_OUT, D_MODEL), F32),
            pltpu.VMEM((2 * TM_OUT, D_MODEL), BF16),
            pltpu.VMEM((64 * PITCH, LANES), F32),
        ],
        compiler_params=pltpu.CompilerParams(
            dimension_semantics=("arbitrary",),
            vmem_limit_bytes=VMEM_LIMIT),
        name="wout_router",
    )(a, b, h, wa, wb, g1, b1, wr_cat, br)


def _route_kernel(lg_ref, info_ref, cnt_ref, run_ref, tri_ref):
    t = pl.program_id(0)
    tm = TM_ROUTE
    l = lg_ref[...]
    lane = lax.broadcasted_iota(jnp.int32, (tm, ROUTE_W), 1)
    neg = jnp.float32(-jnp.inf)
    big = jnp.int32(1 << 20)

    gmask = lane < N_GROUPS
    gmax = jnp.max(jnp.where(gmask, l, neg), axis=-1, keepdims=True)
    gsel = jnp.min(jnp.where(gmask & (l == gmax), lane, big), axis=-1, keepdims=True)
    gsum = jnp.sum(jnp.where(gmask, jnp.exp(l - gmax), 0.0), axis=-1, keepdims=True)
    pg_top = 1.0 / gsum

    lo = N_GROUPS + EXPERTS_PER_GROUP * gsel
    emask = (lane >= lo) & (lane < lo + EXPERTS_PER_GROUP)
    v1 = jnp.max(jnp.where(emask, l, neg), axis=-1, keepdims=True)
    i1 = jnp.min(jnp.where(emask & (l == v1), lane, big), axis=-1, keepdims=True)
    emask2 = emask & (lane != i1)
    v2 = jnp.max(jnp.where(emask2, l, neg), axis=-1, keepdims=True)
    i2 = jnp.min(jnp.where(emask2 & (l == v2), lane, big), axis=-1, keepdims=True)
    e21 = jnp.exp(v2 - v1)
    q1 = pg_top / (1.0 + e21)
    q2 = pg_top * e21 / (1.0 + e21)

    oh1 = (lane == i1).astype(F32)
    oh2 = (lane == i2).astype(F32)
    ohs = oh1 + oh2

    @pl.when(t == 0)
    def _():
        run_ref[...] = jnp.zeros_like(run_ref)
        r_i = lax.broadcasted_iota(jnp.int32, (tm, tm), 0)
        c_i = lax.broadcasted_iota(jnp.int32, (tm, tm), 1)
        tri_ref[...] = (c_i < r_i).astype(BF16)

    cum = jnp.dot(tri_ref[...], ohs.astype(BF16), preferred_element_type=F32)
    basev = run_ref[0:1, :] + cum
    r1 = jnp.sum(oh1 * basev, axis=-1, keepdims=True)
    r2 = jnp.sum(oh2 * basev, axis=-1, keepdims=True)
    run_ref[...] = run_ref[...] + jnp.sum(ohs, axis=0, keepdims=True)
    info = jnp.where(lane == 0, r1, 0.0)
    info = jnp.where(lane == 1, r2, info)
    info = jnp.where(lane == 2, q1, info)
    info = jnp.where(lane == 3, q2, info)
    info = jnp.where(lane == 4, (i1 - N_GROUPS).astype(F32), info)
    info = jnp.where(lane == 5, (i2 - N_GROUPS).astype(F32), info)
    info_ref[...] = info
    cnt_ref[...] = run_ref[...]


def _route(logits):
    n = logits.shape[0]
    nt = n // TM_ROUTE
    return pl.pallas_call(
        _route_kernel,
        grid=(nt,),
        in_specs=[pl.BlockSpec((TM_ROUTE, ROUTE_W), lambda t: (t, 0))],
        out_specs=[
            pl.BlockSpec((TM_ROUTE, ROUTE_W), lambda t: (t, 0)),
            pl.BlockSpec((SUBLANES, ROUTE_W), lambda t: (0, 0)),
        ],
        out_shape=[
            jax.ShapeDtypeStruct((n, ROUTE_W), F32),
            jax.ShapeDtypeStruct((SUBLANES, ROUTE_W), F32),
        ],
        scratch_shapes=[
            pltpu.VMEM((SUBLANES, ROUTE_W), F32),
            pltpu.VMEM((TM_ROUTE, TM_ROUTE), BF16),
        ],
        compiler_params=pltpu.CompilerParams(
            dimension_semantics=("arbitrary",),
            vmem_limit_bytes=VMEM_LIMIT),
        name="route",
    )(logits)


def _prow_copy(src_ref, src_row, dst_ref, dst_row, sem, rows=1):
    return pltpu.make_async_copy(src_ref.at[pl.ds(src_row, rows)], dst_ref.at[pl.ds(dst_row, rows)], sem)


ZERO_ROWS = TM_EXP // 2
DMA_UNROLL = 8


def _pad_fill(e, ps_ref, pl_ref, zbuf_ref, xs_ref, zsem, wait):
    ln = pl_ref[e]
    st = ps_ref[e]
    b = 1
    while b <= ZERO_ROWS:
        @pl.when((ln & b) != 0)
        def _(b=b):
            cp = _prow_copy(zbuf_ref, 0, xs_ref, st + (ln & (b - 1)), zsem, rows=b)
            if wait:
                cp.wait()
            else:
                cp.start()
        b *= 2


def _dispatch_kernel(pos0_ref, pos1_ref, ps_ref, pl_ref, h1p_ref, xs_ref, zbuf_ref, sem, zsem):
    i = pl.program_id(0)
    tm = TM_DISP

    @pl.when(i == 0)
    def _():
        zbuf_ref[...] = jnp.zeros_like(zbuf_ref)

        def fill_start(e, _):
            _pad_fill(e, ps_ref, pl_ref, zbuf_ref, xs_ref, zsem, False)
            return 0

        lax.fori_loop(0, N_EXPERTS, fill_start, 0)

    def issue(k, _):
        for u in range(DMA_UNROLL):
            r = k * DMA_UNROLL + u
            tok = i * tm + r
            _prow_copy(h1p_ref, r, xs_ref, pos0_ref[tok], sem).start(priority=0)
            _prow_copy(h1p_ref, r, xs_ref, pos1_ref[tok], sem).start(priority=1)
        return 0

    lax.fori_loop(0, tm // DMA_UNROLL, issue, 0)

    def drain(k, _):
        for u in range(DMA_UNROLL):
            _prow_copy(h1p_ref, 0, xs_ref, 0, sem).wait()
            _prow_copy(h1p_ref, 0, xs_ref, 0, sem).wait()
        return 0

    lax.fori_loop(0, tm // DMA_UNROLL, drain, 0)

    @pl.when(i == pl.num_programs(0) - 1)
    def _():
        def fill_wait(e, _):
            _pad_fill(e, ps_ref, pl_ref, zbuf_ref, xs_ref, zsem, True)
            return 0

        lax.fori_loop(0, N_EXPERTS, fill_wait, 0)


def _dispatch(pos0, pos1, pad_start, pad_len, h1p, n_rows):
    n = h1p.shape[0]
    return pl.pallas_call(
        _dispatch_kernel,
        grid_spec=pltpu.PrefetchScalarGridSpec(
            num_scalar_prefetch=4,
            grid=(n // TM_DISP,),
            in_specs=[pl.BlockSpec((TM_DISP, RT, LANES), lambda i, *_: (i, 0, 0))],
            out_specs=pl.BlockSpec(memory_space=pl.ANY),
            scratch_shapes=[
                pltpu.VMEM((ZERO_ROWS, RT, LANES), BF16),
                pltpu.SemaphoreType.DMA(()),
                pltpu.SemaphoreType.DMA(()),
            ],
        ),
        out_shape=jax.ShapeDtypeStruct((n_rows, RT, LANES), BF16),
        compiler_params=pltpu.CompilerParams(
            dimension_semantics=("arbitrary",),
            vmem_limit_bytes=VMEM_LIMIT),
        name="dispatch",
    )(pos0, pos1, pad_start, pad_len, h1p)


def _expert_kernel(te_ref, nu_ref, nx_ref, sl_ref, xs_ref, wg_hbm, wu_hbm, wd_hbm, ys_ref,
                   wgf_ref, wuf_ref, wdf_ref, wgb_ref, wub_ref, wdb_ref, stage_ref, wsem):
    i = pl.program_id(0)
    used = i < nu_ref[0]
    e = te_ref[i]
    s = sl_ref[i]
    fresh = (i == 0) | (e != te_ref[jnp.maximum(i - 1, 0)])

    def weight_copies(expert, slot):
        return (pltpu.make_async_copy(wg_hbm.at[expert], wgf_ref.at[slot], wsem.at[slot, 0]),
                pltpu.make_async_copy(wu_hbm.at[expert], wuf_ref.at[slot], wsem.at[slot, 1]),
                pltpu.make_async_copy(wd_hbm.at[expert], wdf_ref.at[slot], wsem.at[slot, 2]))

    @pl.when(i == 0)
    def _():
        for cp in weight_copies(e, s):
            cp.start()

    @pl.when(used & fresh)
    def _():
        for cp in weight_copies(e, s):
            cp.wait()

        @pl.when(nx_ref[i] >= 0)
        def _():
            for cp in weight_copies(nx_ref[i], 1 - s):
                cp.start()

        wgb_ref[...] = wgf_ref[s].astype(BF16)
        wub_ref[...] = wuf_ref[s].astype(BF16)
        wdb_ref[...] = wdf_ref[s].astype(BF16)

    @pl.when(used)
    def _():
        x = jnp.concatenate([p.astype(BF16) for p in _load_rows(xs_ref, 0, TM_EXP, stage_ref)],
                            axis=1)
        g = jnp.dot(x, wgb_ref[...], preferred_element_type=F32)
        u = jnp.dot(x, wub_ref[...], preferred_element_type=F32)
        h = (g * _sigmoid(g) * u).astype(BF16)
        y = jnp.dot(h, wdb_ref[...], preferred_element_type=F32)
        _store_rows(ys_ref, 0, TM_EXP, y, stage_ref)


def _expert_ffn(tile_expert, n_used, next_expert, slot, xs, wg, wu, wd):
    n_rows = xs.shape[0]
    n_tiles = n_rows // TM_EXP

    def row_map(i, te, nu, nx, sl):
        return (jnp.minimum(i, nu[0] - 1), 0, 0)

    return pl.pallas_call(
        _expert_kernel,
        grid_spec=pltpu.PrefetchScalarGridSpec(
            num_scalar_prefetch=4,
            grid=(n_tiles,),
            in_specs=[
                pl.BlockSpec((TM_EXP, RT, LANES), row_map),
                pl.BlockSpec(memory_space=pl.ANY),
                pl.BlockSpec(memory_space=pl.ANY),
                pl.BlockSpec(memory_space=pl.ANY),
            ],
            out_specs=pl.BlockSpec((TM_EXP, RT, LANES), row_map),
            scratch_shapes=[
                pltpu.VMEM((2, D_MODEL, D_EXPERT), F32),
                pltpu.VMEM((2, D_MODEL, D_EXPERT), F32),
                pltpu.VMEM((2, D_EXPERT, D_MODEL), F32),
                pltpu.VMEM((D_MODEL, D_EXPERT), BF16),
                pltpu.VMEM((D_MODEL, D_EXPERT), BF16),
                pltpu.VMEM((D_EXPERT, D_MODEL), BF16),
                pltpu.VMEM((TM_EXP * PITCH, LANES), F32),
                pltpu.SemaphoreType.DMA((2, 3)),
            ],
        ),
        out_shape=jax.ShapeDtypeStruct((n_rows, RT, LANES), BF16),
        compiler_params=pltpu.CompilerParams(
            dimension_semantics=("arbitrary",),
            vmem_limit_bytes=VMEM_LIMIT),
        name="expert_ffn",
    )(tile_expert, n_used, next_expert, slot, xs, wg, wu, wd)


def _combine_kernel(pos0_ref, pos1_ref, h1_ref, info_ref, g2_ref, b2_ref, ys_ref, o_ref,
                    ybuf_ref, stage_ref, sem):
    i = pl.program_id(0)
    tm = TM_COMB
    par = lax.rem(i, 2)

    def issue(step, parity):
        def body(k, _):
            for u in range(DMA_UNROLL):
                r = k * DMA_UNROLL + u
                tok = step * tm + r
                _prow_copy(ys_ref, pos0_ref[tok], ybuf_ref.at[parity, 0], r,
                           sem.at[parity]).start(priority=0)
                _prow_copy(ys_ref, pos1_ref[tok], ybuf_ref.at[parity, 1], r,
                           sem.at[parity]).start(priority=1)
            return 0
        lax.fori_loop(0, tm // DMA_UNROLL, body, 0)

    def drain(parity):
        def body(k, _):
            for u in range(DMA_UNROLL):
                _prow_copy(ys_ref, 0, ybuf_ref.at[parity, 0], 0, sem.at[parity]).wait()
                _prow_copy(ys_ref, 0, ybuf_ref.at[parity, 1], 0, sem.at[parity]).wait()
            return 0
        lax.fori_loop(0, tm // DMA_UNROLL, body, 0)

    @pl.when(i == 0)
    def _():
        issue(0, 0)

    @pl.when(i + 1 < pl.num_programs(0))
    def _():
        issue(i + 1, 1 - par)

    drain(par)

    rows = 64
    for c in range(tm // rows):
        rs = slice(c * rows, (c + 1) * rows)
        q1 = info_ref[rs, 2:3]
        q2 = info_ref[rs, 3:4]
        y0 = _load_rows(ybuf_ref.at[par, 0], c * rows, rows, stage_ref)
        y1 = _load_rows(ybuf_ref.at[par, 1], c * rows, rows, stage_ref)
        ffn = jnp.concatenate([q1 * a + q2 * b for a, b in zip(y0, y1)], axis=1)
        o_ref[rs, :] = _ln_rows(DN_ALPHA * h1_ref[rs, :] + ffn, g2_ref[...], b2_ref[...])


def _combine(pos0, pos1, h1, info, g2, b2, ys):
    n = h1.shape[0]
    return pl.pallas_call(
        _combine_kernel,
        grid_spec=pltpu.PrefetchScalarGridSpec(
            num_scalar_prefetch=2,
            grid=(n // TM_COMB,),
            in_specs=[
                pl.BlockSpec((TM_COMB, D_MODEL), lambda i, *_: (i, 0)),
                pl.BlockSpec((TM_COMB, ROUTE_W), lambda i, *_: (i, 0)),
                pl.BlockSpec((1, D_MODEL), lambda i, *_: (0, 0)),
                pl.BlockSpec((1, D_MODEL), lambda i, *_: (0, 0)),
                pl.BlockSpec(memory_space=pl.ANY),
            ],
            out_specs=pl.BlockSpec((TM_COMB, D_MODEL), lambda i, *_: (i, 0)),
            scratch_shapes=[
                pltpu.VMEM((2, 2, TM_COMB, RT, LANES), BF16),
                pltpu.VMEM((64 * PITCH, LANES), F32),
                pltpu.SemaphoreType.DMA((2,)),
            ],
        ),
        out_shape=jax.ShapeDtypeStruct((n, D_MODEL), F32),
        compiler_params=pltpu.CompilerParams(
            dimension_semantics=("arbitrary",),
            vmem_limit_bytes=VMEM_LIMIT),
        name="combine",
    )(pos0, pos1, h1, info, g2, b2, ys)


def _block_diag(w, per):
    h, hd, _ = w.shape
    wg = w.reshape(h // per, per, hd, hd)
    eye = jnp.eye(per, dtype=w.dtype)
    return jnp.einsum("gpij,pq->gpiqj", wg, eye).reshape(h // per, per * hd, per * hd)


def kernel(x, ln_in_g, ln_in_b, w_in, lru_conv_w, lru_conv_b, lru_w_a, lru_b_a, lru_w_x, lru_b_x,
           lru_lambda, conf_conv_w, conf_conv_b, conf_ln_g, conf_ln_b, w_out, ln1_g, ln1_b,
           router_group_w, router_group_b, router_expert_w, router_expert_b, exp_w_gate, exp_w_up,
           exp_w_down, ln2_g, ln2_b):
    bsz, seq, d = x.shape
    n = bsz * seq
    x2 = x.reshape(n, d)
    row = lambda v: v.reshape(1, -1).astype(F32)
    l = 0

    z, h0 = _ln_win(x2, row(ln_in_g), row(ln_in_b), w_in[l])

    per = CB_LRU // LRU_HEAD_DIM
    wcat = jnp.concatenate([_block_diag(lru_w_a[l], per), _block_diag(lru_w_x[l], per)],
                           axis=-1).astype(BF16)
    a_out = _lru_mixer(z, lru_conv_w[l], row(lru_conv_b[l]), wcat, row(lru_b_a[l]),
                       row(lru_b_x[l]), row(lru_lambda[l]), bsz, seq)

    nlb = D_CONV // LANES
    w3 = jnp.pad(conf_conv_w[l], ((0, 32 - CONF_CONV_W), (0, 0)))
    w3 = w3.reshape(32, nlb, LANES).transpose(1, 0, 2)
    cb3 = conf_conv_b[l].reshape(nlb, 1, LANES)
    b_out = _conf_mixer(z, w3, cb3, row(conf_ln_g[l]), row(conf_ln_b[l]), bsz, seq)

    wr = jnp.concatenate([router_group_w[l], router_expert_w[l]], axis=1)
    wr = jnp.pad(wr, ((0, 0), (0, ROUTE_W - wr.shape[1])))
    wr_hi = wr.astype(BF16)
    wr_lo = (wr - wr_hi.astype(F32)).astype(BF16)
    wr_cat = jnp.concatenate([wr_hi, wr_lo], axis=1)
    br = jnp.concatenate([router_group_b[l], router_expert_b[l]])
    br = jnp.pad(br, (0, ROUTE_W - br.shape[0])).reshape(1, ROUTE_W)
    wo = w_out[l].astype(BF16)
    h1, h1p, logits = _wout_router(a_out, b_out, h0, wo[:D_LRU], wo[D_LRU:], row(ln1_g[l]),
                                   row(ln1_b[l]), wr_cat, br)

    info, counts = _route(logits)
    idx = info[:, 0:6].astype(jnp.int32)
    r0, r1, e0, e1 = idx[:, 0], idx[:, 1], idx[:, 4], idx[:, 5]

    n_tiles = (n * 2) // TM_EXP + N_EXPERTS
    cnt = counts[0, N_GROUPS:N_GROUPS + N_EXPERTS].astype(jnp.int32)
    tiles_per = (cnt + TM_EXP - 1) // TM_EXP
    ends = jnp.cumsum(tiles_per)
    n_used = ends[-1:].astype(jnp.int32)
    tile_ids = jnp.arange(n_tiles, dtype=jnp.int32)
    tile_expert = jnp.minimum(jnp.sum(tile_ids[:, None] >= ends[None, :], axis=1),
                              N_EXPERTS - 1).astype(jnp.int32)
    last = tile_expert[jnp.maximum(n_used[0] - 1, 0)]
    tile_expert = jnp.where(tile_ids < n_used[0], tile_expert, last)

    nxt_tile = ends[tile_expert]
    next_expert = jnp.where(nxt_tile < n_used[0],
                            tile_expert[jnp.minimum(nxt_tile, n_tiles - 1)], -1).astype(jnp.int32)
    changes = jnp.concatenate([jnp.zeros((1,), jnp.int32),
                               (tile_expert[1:] != tile_expert[:-1]).astype(jnp.int32)])
    slot = (jnp.cumsum(changes) & 1).astype(jnp.int32)
    starts = (ends - tiles_per) * TM_EXP
    pad_start = (starts + cnt).astype(jnp.int32)
    pad_len = (tiles_per * TM_EXP - cnt).astype(jnp.int32)

    expert_ids = jnp.arange(N_EXPERTS, dtype=jnp.int32)[None, :]
    start_of = lambda e: jnp.sum(jnp.where(e[:, None] == expert_ids, starts[None, :], 0), axis=1)
    pos0 = (start_of(e0) + r0).astype(jnp.int32)
    pos1 = (start_of(e1) + r1).astype(jnp.int32)

    xs = _dispatch(pos0, pos1, pad_start, pad_len, h1p, n_tiles * TM_EXP)
    shp = (N_EXPERTS, D_MODEL, D_EXPERT)
    ys = _expert_ffn(tile_expert, n_used, next_expert, slot, xs, exp_w_gate[l].reshape(shp),
                     exp_w_up[l].reshape(shp), exp_w_down[l].reshape(N_EXPERTS, D_EXPERT, D_MODEL))
    out = _combine(pos0, pos1, h1, info, row(ln2_g[l]), row(ln2_b[l]), ys)
    return out.reshape(bsz, seq, d)
```

```python
import functools
import math

import jax
import jax.numpy as jnp
from jax import lax
from jax.experimental import pallas as pl
from jax.experimental.pallas import tpu as pltpu

F32 = jnp.float32
BF16 = jnp.bfloat16

D_MODEL = 2048
D_LRU = 1024
D_CONV = 1024
LRU_HEADS = 16
LRU_HEAD_DIM = 64
LRU_C = 8.0
LRU_CONV_W = 4
CONF_CONV_W = 31
N_GROUPS = 4
EXPERTS_PER_GROUP = 8
N_EXPERTS = N_GROUPS * EXPERTS_PER_GROUP
D_EXPERT = 512
LN_EPS = 1e-5
DEPTH = 1
DN_ALPHA = (2 * DEPTH) ** 0.25

LANES = 128
SUBLANES = 8
VMEM_LIMIT = 56 * 1024 * 1024

TM_WIN = 512
TN_WIN = 1024
W_CHUNK = 256
TT_LRU = 1024
CB_LRU = 256
TT_CONF = 256
CONF_HALO = 32
TM_OUT = 512
TM_ROUTE = 512
TM_DISP = 512
TM_EXP = 256
TM_COMB = 256
ROUTE_W = 128


def _sigmoid(x):
    return 0.5 * (jnp.tanh(0.5 * x) + 1.0)


def _ln_rows(x, g, b):
    mu = jnp.mean(x, axis=-1, keepdims=True)
    xc = x - mu
    var = jnp.mean(xc * xc, axis=-1, keepdims=True)
    return xc * lax.rsqrt(var + LN_EPS) * g + b


def _stage_weight(w_hbm, wb_ref, wst_ref, wsem):
    nchunk = wb_ref.shape[1] // W_CHUNK

    def chunk_copy(c):
        return pltpu.make_async_copy(w_hbm.at[:, pl.ds(c * W_CHUNK, W_CHUNK)],
                                     wst_ref.at[c % 2], wsem.at[c % 2])

    chunk_copy(0).start()
    for c in range(nchunk):
        if c + 1 < nchunk:
            chunk_copy(c + 1).start()
        chunk_copy(c).wait()
        wb_ref[:, c * W_CHUNK:(c + 1) * W_CHUNK] = wst_ref[c % 2].astype(BF16)


def _ln_win_kernel(x_ref, g_ref, b_ref, w_hbm, z_ref, h_ref, wb_ref, wst_ref, xn_ref, wsem):
    s = pl.program_id(0)
    par = lax.rem(s, 2)

    @pl.when(s == 0)
    def _():
        xn_ref[1] = jnp.zeros(xn_ref.shape[1:], BF16)
        _stage_weight(w_hbm, wb_ref, wst_ref, wsem)

    rows = 128
    for c in range(TM_WIN // rows):
        rs = slice(c * rows, (c + 1) * rows)
        hn = _ln_rows(x_ref[rs, :], g_ref[...], b_ref[...])
        h_ref[rs, :] = hn
        xn_ref[par, rs, :] = hn.astype(BF16)

    xprev = xn_ref[1 - par]
    for c in range(z_ref.shape[1] // TN_WIN):
        cs = slice(c * TN_WIN, (c + 1) * TN_WIN)
        z_ref[:, cs] = jnp.dot(xprev, wb_ref[:, cs], preferred_element_type=F32).astype(z_ref.dtype)


def _ln_win(x2, g, b, w):
    n = x2.shape[0]
    ncol = w.shape[1]
    nt = n // TM_WIN
    return pl.pallas_call(
        _ln_win_kernel,
        grid=(nt + 1,),
        in_specs=[
            pl.BlockSpec((TM_WIN, D_MODEL), lambda s: (jnp.minimum(s, nt - 1), 0)),
            pl.BlockSpec((1, D_MODEL), lambda s: (0, 0)),
            pl.BlockSpec((1, D_MODEL), lambda s: (0, 0)),
            pl.BlockSpec(memory_space=pl.ANY),
        ],
        out_specs=[
            pl.BlockSpec((TM_WIN, ncol), lambda s: (jnp.maximum(s - 1, 0), 0)),
            pl.BlockSpec((TM_WIN, D_MODEL), lambda s: (jnp.minimum(s, nt - 1), 0)),
        ],
        out_shape=[
            jax.ShapeDtypeStruct((n, ncol), BF16),
            jax.ShapeDtypeStruct((n, D_MODEL), F32),
        ],
        scratch_shapes=[
            pltpu.VMEM((D_MODEL, ncol), BF16),
            pltpu.VMEM((2, D_MODEL, W_CHUNK), F32),
            pltpu.VMEM((2, TM_WIN, D_MODEL), BF16),
            pltpu.SemaphoreType.DMA((2,)),
        ],
        compiler_params=pltpu.CompilerParams(
            dimension_semantics=("arbitrary",),
            vmem_limit_bytes=VMEM_LIMIT),
        name="ln_win",
    )(x2, g, b, w)


def _lru_kernel(zx_ref, zg_ref, cw_ref, cb_ref, wcat_ref, ba_ref, bx_ref, lam_ref,
                o_ref, xs_ref, hp_ref, a_ref, g_ref):
    t = pl.program_id(2)
    tt = TT_LRU

    @pl.when(t == 0)
    def _():
        xs_ref[0:SUBLANES, :] = jnp.zeros((SUBLANES, CB_LRU), F32)
        hp_ref[...] = jnp.zeros_like(hp_ref)

    @pl.when(t > 0)
    def _():
        xs_ref[0:SUBLANES, :] = xs_ref[tt:tt + SUBLANES, :]

    xs_ref[SUBLANES:SUBLANES + tt, :] = zx_ref[...].astype(F32)

    rows = 128
    for rb in range(tt // rows):
        acc = jnp.broadcast_to(cb_ref[...], (rows, CB_LRU))
        for k in range(LRU_CONV_W):
            off = rb * rows + SUBLANES - (LRU_CONV_W - 1) + k
            acc = acc + cw_ref[k:k + 1, :] * xs_ref[off:off + rows, :]
        a_ref[rb * rows:(rb + 1) * rows, :] = acc

    g_ref[...] = jnp.dot(a_ref[...].astype(BF16), wcat_ref[0], preferred_element_type=F32)

    lam = lam_ref[...]
    softplus_neg = jnp.maximum(-lam, 0.0) + jnp.log1p(jnp.exp(-jnp.abs(lam)))
    cvec = -LRU_C * softplus_neg
    ba = ba_ref[...]
    bx = bx_ref[...]
    blk = 64
    row_in_vreg = lax.broadcasted_iota(jnp.int32, (blk, CB_LRU), 0) & (SUBLANES - 1)

    def body(rb, h):
        rs = pl.ds(pl.multiple_of(rb * blk, blk), blk)
        a_in = a_ref[rs, :]
        r = _sigmoid(g_ref[rs, 0:CB_LRU] + ba)
        i = _sigmoid(g_ref[rs, CB_LRU:2 * CB_LRU] + bx)
        log_a = cvec * r
        a = jnp.exp(log_a)
        u = jnp.sqrt(-jnp.tanh(log_a) * (a * a + 1.0)) * (i * a_in)
        for s in (1, 2, 4):
            m = row_in_vreg >= s
            a_sh = jnp.where(m, pltpu.roll(a, s, 0), 1.0)
            u_sh = jnp.where(m, pltpu.roll(u, s, 0), 0.0)
            u = u + a * u_sh
            a = a * a_sh
        outs = []
        for gi in range(blk // SUBLANES):
            ag = a[gi * SUBLANES:(gi + 1) * SUBLANES, :]
            ug = u[gi * SUBLANES:(gi + 1) * SUBLANES, :]
            hg = ug + ag * h
            h = hg[SUBLANES - 1:SUBLANES, :]
            outs.append(hg)
        hblk = jnp.concatenate(outs, axis=0)
        gl = zg_ref[rs, :].astype(F32)
        gelu = 0.5 * gl * (1.0 + jnp.tanh(0.7978845608028654 * (gl + 0.044715 * gl * gl * gl)))
        o_ref[rs, :] = (gelu * hblk).astype(o_ref.dtype)
        return h

    h = lax.fori_loop(0, tt // blk, body, hp_ref[0:1, :])
    hp_ref[...] = jnp.broadcast_to(h, hp_ref.shape)


def _lru_mixer(z, cw, cb, wcat, ba, bx, lam, bsz, seq):
    n = z.shape[0]
    nt = seq // TT_LRU
    ncb = D_LRU // CB_LRU
    row = lambda b, j, t: b * nt + t
    vec = pl.BlockSpec((1, CB_LRU), lambda b, j, t: (0, j))
    return pl.pallas_call(
        _lru_kernel,
        grid=(bsz, ncb, nt),
        in_specs=[
            pl.BlockSpec((TT_LRU, CB_LRU), lambda b, j, t: (row(b, j, t), j)),
            pl.BlockSpec((TT_LRU, CB_LRU), lambda b, j, t: (row(b, j, t), ncb + j)),
            pl.BlockSpec((LRU_CONV_W, CB_LRU), lambda b, j, t: (0, j)),
            vec,
            pl.BlockSpec((1, CB_LRU, 2 * CB_LRU), lambda b, j, t: (j, 0, 0)),
            vec, vec, vec,
        ],
        out_specs=pl.BlockSpec((TT_LRU, CB_LRU), lambda b, j, t: (row(b, j, t), j)),
        out_shape=jax.ShapeDtypeStruct((n, D_LRU), BF16),
        scratch_shapes=[
            pltpu.VMEM((TT_LRU + SUBLANES, CB_LRU), F32),
            pltpu.VMEM((SUBLANES, CB_LRU), F32),
            pltpu.VMEM((TT_LRU, CB_LRU), F32),
            pltpu.VMEM((TT_LRU, 2 * CB_LRU), F32),
        ],
        compiler_params=pltpu.CompilerParams(
            dimension_semantics=("arbitrary", "arbitrary", "arbitrary"),
            vmem_limit_bytes=VMEM_LIMIT),
        name="lru_mixer",
    )(z, z, cw, cb, wcat, ba, bx, lam)


def _conf_kernel(zv_ref, zg_ref, w_ref, cb_ref, lg_ref, lb_ref, wo_ref, o_ref, wob_ref,
                 cs_ref, cv_ref):
    t = pl.program_id(1)
    tt = TT_CONF
    nlb = D_CONV // LANES
    wob_ref[...] = wo_ref[...].astype(BF16)

    @pl.when(t == 0)
    def _():
        cs_ref[:, 0:CONF_HALO, :] = jnp.zeros((nlb, CONF_HALO, LANES), F32)

    @pl.when(t > 0)
    def _():
        cs_ref[:, 0:CONF_HALO, :] = cs_ref[:, tt:tt + CONF_HALO, :]

    for c in range(nlb):
        ls = slice(c * LANES, (c + 1) * LANES)
        v = zv_ref[:, ls].astype(F32)
        g = zg_ref[:, ls].astype(F32)
        cs_ref[c, CONF_HALO:CONF_HALO + tt, :] = v * _sigmoid(g)

    rows = 64
    nrb = tt // rows
    base = CONF_HALO - (CONF_CONV_W - 1)

    def conv_body(c, carry):
        accs = [jnp.broadcast_to(cb_ref[c], (rows, LANES)) for _ in range(nrb)]
        for k in range(CONF_CONV_W):
            wk = w_ref[c, k:k + 1, :]
            for rb in range(nrb):
                off = rb * rows + base + k
                accs[rb] = accs[rb] + wk * cs_ref[c, off:off + rows, :]
        for rb in range(nrb):
            cv_ref[c, rb * rows:(rb + 1) * rows, :] = accs[rb]
        return carry

    lax.fori_loop(0, nlb, conv_body, 0)

    ln_rows = 32
    inv_n = 1.0 / D_CONV
    for rb in range(tt // ln_rows):
        rs = slice(rb * ln_rows, (rb + 1) * ln_rows)
        blk = cv_ref[:, rs, :]
        mu = jnp.sum(jnp.sum(blk, axis=0), axis=-1, keepdims=True) * inv_n
        d = blk - mu[None]
        var = jnp.sum(jnp.sum(d * d, axis=0), axis=-1, keepdims=True) * inv_n
        inv = lax.rsqrt(var + LN_EPS)
        for c in range(nlb):
            ls = slice(c * LANES, (c + 1) * LANES)
            y = d[c] * inv * lg_ref[:, ls] + lb_ref[:, ls]
            o_ref[rs, ls] = (y * _sigmoid(y)).astype(o_ref.dtype)


def _conf_mixer(z, w3, cb3, lg, lb, wo, bsz, seq):
    n = z.shape[0]
    nt = seq // TT_CONF
    nlb = D_CONV // LANES
    wo_rows = wo.shape[0] // (bsz * nt)
    assert wo_rows * bsz * nt == wo.shape[0] and wo_rows % 16 == 0
    return pl.pallas_call(
        _conf_kernel,
        grid=(bsz, nt),
        in_specs=[
            pl.BlockSpec((TT_CONF, D_CONV), lambda b, t: (b * nt + t, 2)),
            pl.BlockSpec((TT_CONF, D_CONV), lambda b, t: (b * nt + t, 3)),
            pl.BlockSpec((nlb, 32, LANES), lambda b, t: (0, 0, 0)),
            pl.BlockSpec((nlb, 1, LANES), lambda b, t: (0, 0, 0)),
            pl.BlockSpec((1, D_CONV), lambda b, t: (0, 0)),
            pl.BlockSpec((1, D_CONV), lambda b, t: (0, 0)),
            pl.BlockSpec((wo_rows, D_MODEL), lambda b, t: (b * nt + t, 0)),
        ],
        out_specs=[
            pl.BlockSpec((TT_CONF, D_CONV), lambda b, t: (b * nt + t, 0)),
            pl.BlockSpec((wo_rows, D_MODEL), lambda b, t: (b * nt + t, 0)),
        ],
        out_shape=[
            jax.ShapeDtypeStruct((n, D_CONV), BF16),
            jax.ShapeDtypeStruct(wo.shape, BF16),
        ],
        scratch_shapes=[
            pltpu.VMEM((nlb, CONF_HALO + TT_CONF, LANES), F32),
            pltpu.VMEM((nlb, TT_CONF, LANES), F32),
        ],
        compiler_params=pltpu.CompilerParams(
            dimension_semantics=("arbitrary", "arbitrary"),
            vmem_limit_bytes=VMEM_LIMIT),
        name="conf_mixer",
    )(z, z, w3, cb3, lg, lb, wo)


def _split_bf16(v):
    hi = v.astype(BF16)
    lo = (v - hi.astype(F32)).astype(BF16)
    return hi, lo


RT = D_MODEL // LANES
PITCH = RT + SUBLANES


def _store_rows(dst_ref, row0, rows, v, stage_ref):
    for s in range(RT):
        stage_ref[pl.ds(s, rows, stride=PITCH), :] = v[:, s * LANES:(s + 1) * LANES]
    staged = stage_ref[0:rows * PITCH, :].reshape(rows, PITCH, LANES)
    dst_ref[row0:row0 + rows] = staged[:, 0:RT, :].astype(BF16)


def _load_rows(src_ref, row0, rows, stage_ref):
    tile = src_ref[row0:row0 + rows].astype(F32)
    tile = jnp.concatenate([tile, jnp.zeros((rows, PITCH - RT, LANES), F32)], axis=1)
    stage_ref[0:rows * PITCH, :] = tile.reshape(rows * PITCH, LANES)
    return [stage_ref[pl.ds(s, rows, stride=PITCH), :] for s in range(RT)]


def _wout_kernel(a_ref, b_ref, h_ref, wa_ref, wb_ref, g1_ref, b1_ref,
                 wr_ref, br_ref, h1_ref, h1r_ref, lg_ref, mix_ref, hl_ref, stage_ref):
    mix_ref[...] = (jnp.dot(a_ref[...], wa_ref[...], preferred_element_type=F32)
                    + jnp.dot(b_ref[...], wb_ref[...], preferred_element_type=F32))
    rows = 64
    for c in range(TM_OUT // rows):
        rs = slice(c * rows, (c + 1) * rows)
        h1 = _ln_rows(DN_ALPHA * h_ref[rs, :] + mix_ref[rs, :], g1_ref[...], b1_ref[...])
        h1_ref[rs, :] = h1
        _store_rows(h1r_ref, c * rows, rows, h1, stage_ref)
        hi, lo = _split_bf16(h1)
        hl_ref[rs, :] = hi
        hl_ref[TM_OUT + c * rows:TM_OUT + (c + 1) * rows, :] = lo
    p = jnp.dot(hl_ref[...], wr_ref[...], preferred_element_type=F32)
    lg_ref[...] = (p[0:TM_OUT, 0:ROUTE_W] + p[0:TM_OUT, ROUTE_W:2 * ROUTE_W]
                   + p[TM_OUT:2 * TM_OUT, 0:ROUTE_W] + br_ref[...])


def _wout_router(a, b, h, wo, g1, b1, wr_cat, br):
    n = h.shape[0]
    assert D_LRU == D_CONV
    full = lambda shape: pl.BlockSpec(shape, lambda i: tuple(0 for _ in shape))
    return pl.pallas_call(
        _wout_kernel,
        grid=(n // TM_OUT,),
        in_specs=[
            pl.BlockSpec((TM_OUT, D_LRU), lambda i: (i, 0)),
            pl.BlockSpec((TM_OUT, D_CONV), lambda i: (i, 0)),
            pl.BlockSpec((TM_OUT, D_MODEL), lambda i: (i, 0)),
            pl.BlockSpec((D_LRU, D_MODEL), lambda i: (0, 0)),
            pl.BlockSpec((D_CONV, D_MODEL), lambda i: (1, 0)),
            full((1, D_MODEL)), full((1, D_MODEL)),
            full((D_MODEL, 2 * ROUTE_W)), full((1, ROUTE_W)),
        ],
        out_specs=[
            pl.BlockSpec((TM_OUT, D_MODEL), lambda i: (i, 0)),
            pl.BlockSpec((TM_OUT, RT, LANES), lambda i: (i, 0, 0)),
            pl.BlockSpec((TM_OUT, ROUTE_W), lambda i: (i, 0)),
        ],
        out_shape=[
            jax.ShapeDtypeStruct((n, D_MODEL), F32),
            jax.ShapeDtypeStruct((n, RT, LANES), BF16),
            jax.ShapeDtypeStruct((n, ROUTE_W), F32),
        ],
        scratch_shapes=[
            pltpu.VMEM((TM_OUT, D_MODEL), F32),
            pltpu.VMEM((2 * TM_OUT, D_MODEL), BF16),
            pltpu.VMEM((64 * PITCH, LANES), F32),
        ],
        compiler_params=pltpu.CompilerParams(
            dimension_semantics=("arbitrary",),
            vmem_limit_bytes=VMEM_LIMIT),
        name="wout_router",
    )(a, b, h, wo, wo, g1, b1, wr_cat, br)


def _route_kernel(lg_ref, info_ref, cnt_ref, run_ref, tri_ref):
    t = pl.program_id(0)
    tm = TM_ROUTE
    l = lg_ref[...]
    lane = lax.broadcasted_iota(jnp.int32, (tm, ROUTE_W), 1)
    neg = jnp.float32(-jnp.inf)
    big = jnp.int32(1 << 20)

    gmask = lane < N_GROUPS
    gmax = jnp.max(jnp.where(gmask, l, neg), axis=-1, keepdims=True)
    gsel = jnp.min(jnp.where(gmask & (l == gmax), lane, big), axis=-1, keepdims=True)
    gsum = jnp.sum(jnp.where(gmask, jnp.exp(l - gmax), 0.0), axis=-1, keepdims=True)
    pg_top = 1.0 / gsum

    lo = N_GROUPS + EXPERTS_PER_GROUP * gsel
    emask = (lane >= lo) & (lane < lo + EXPERTS_PER_GROUP)
    v1 = jnp.max(jnp.where(emask, l, neg), axis=-1, keepdims=True)
    i1 = jnp.min(jnp.where(emask & (l == v1), lane, big), axis=-1, keepdims=True)
    emask2 = emask & (lane != i1)
    v2 = jnp.max(jnp.where(emask2, l, neg), axis=-1, keepdims=True)
    i2 = jnp.min(jnp.where(emask2 & (l == v2), lane, big), axis=-1, keepdims=True)
    e21 = jnp.exp(v2 - v1)
    q1 = pg_top / (1.0 + e21)
    q2 = pg_top * e21 / (1.0 + e21)

    oh1 = (lane == i1).astype(F32)
    oh2 = (lane == i2).astype(F32)
    ohs = oh1 + oh2

    @pl.when(t == 0)
    def _():
        run_ref[...] = jnp.zeros_like(run_ref)
        r_i = lax.broadcasted_iota(jnp.int32, (tm, tm), 0)
        c_i = lax.broadcasted_iota(jnp.int32, (tm, tm), 1)
        tri_ref[...] = (c_i < r_i).astype(BF16)

    cum = jnp.dot(tri_ref[...], ohs.astype(BF16), preferred_element_type=F32)
    basev = run_ref[0:1, :] + cum
    r1 = jnp.sum(oh1 * basev, axis=-1, keepdims=True)
    r2 = jnp.sum(oh2 * basev, axis=-1, keepdims=True)
    run_ref[...] = run_ref[...] + jnp.sum(ohs, axis=0, keepdims=True)
    info = jnp.where(lane == 0, r1, 0.0)
    info = jnp.where(lane == 1, r2, info)
    info = jnp.where(lane == 2, q1, info)
    info = jnp.where(lane == 3, q2, info)
    info = jnp.where(lane == 4, (i1 - N_GROUPS).astype(F32), info)
    info = jnp.where(lane == 5, (i2 - N_GROUPS).astype(F32), info)
    info_ref[...] = info
    cnt_ref[...] = run_ref[...]


def _route(logits):
    n = logits.shape[0]
    nt = n // TM_ROUTE
    return pl.pallas_call(
        _route_kernel,
        grid=(nt,),
        in_specs=[pl.BlockSpec((TM_ROUTE, ROUTE_W), lambda t: (t, 0))],
        out_specs=[
            pl.BlockSpec((TM_ROUTE, ROUTE_W), lambda t: (t, 0)),
            pl.BlockSpec((SUBLANES, ROUTE_W), lambda t: (0, 0)),
        ],
        out_shape=[
            jax.ShapeDtypeStruct((n, ROUTE_W), F32),
            jax.ShapeDtypeStruct((SUBLANES, ROUTE_W), F32),
        ],
        scratch_shapes=[
            pltpu.VMEM((SUBLANES, ROUTE_W), F32),
            pltpu.VMEM((TM_ROUTE, TM_ROUTE), BF16),
        ],
        compiler_params=pltpu.CompilerParams(
            dimension_semantics=("arbitrary",),
            vmem_limit_bytes=VMEM_LIMIT),
        name="route",
    )(logits)


def _plan_kernel(cnt_ref, te_ref, nu_ref, nx_ref, sl_ref, st_ref, ps_ref, pl_ref):
    n_tiles = te_ref.shape[0]
    shift = TM_EXP.bit_length() - 1

    def forward(e, carry):
        tile, parity = carry
        c = cnt_ref[N_GROUPS + e]
        tp = (c + (TM_EXP - 1)) >> shift
        st_ref[e] = tile * TM_EXP
        ps_ref[e] = tile * TM_EXP + c
        pl_ref[e] = tp * TM_EXP - c

        def mark(k, _):
            te_ref[tile + k] = e
            sl_ref[tile + k] = parity
            return 0

        lax.fori_loop(0, tp, mark, 0)
        return tile + tp, jnp.where(tp > 0, 1 - parity, parity)

    used, _ = lax.fori_loop(0, N_EXPERTS, forward, (jnp.int32(0), jnp.int32(0)))
    nu_ref[0] = used

    def backward(j, nxt):
        e = N_EXPERTS - 1 - j
        c = cnt_ref[N_GROUPS + e]
        tp = (c + (TM_EXP - 1)) >> shift
        first = st_ref[e] >> shift

        def mark(k, _):
            nx_ref[first + k] = nxt
            return 0

        lax.fori_loop(0, tp, mark, 0)
        return jnp.where(tp > 0, e, nxt)

    lax.fori_loop(0, N_EXPERTS, backward, jnp.int32(-1))

    last_e = te_ref[jnp.maximum(used - 1, 0)]
    last_s = sl_ref[jnp.maximum(used - 1, 0)]

    def tail(i, _):
        te_ref[i] = last_e
        sl_ref[i] = last_s
        nx_ref[i] = -1
        return 0

    lax.fori_loop(used, n_tiles, tail, 0)


def _plan(counts_i32, n_tiles):
    smem = lambda: pl.BlockSpec(memory_space=pltpu.SMEM)
    i32 = lambda k: jax.ShapeDtypeStruct((k,), jnp.int32)
    return pl.pallas_call(
        _plan_kernel,
        in_specs=[smem()],
        out_specs=[smem() for _ in range(7)],
        out_shape=[i32(n_tiles), i32(1), i32(n_tiles), i32(n_tiles),
                   i32(N_EXPERTS), i32(N_EXPERTS), i32(N_EXPERTS)],
        name="plan",
    )(counts_i32)


def _prow_copy(src_ref, src_row, dst_ref, dst_row, sem, rows=1):
    return pltpu.make_async_copy(src_ref.at[pl.ds(src_row, rows)], dst_ref.at[pl.ds(dst_row, rows)], sem)


ZERO_ROWS = TM_EXP // 2
DMA_UNROLL = 8


def _pad_fill(e, ps_ref, pl_ref, zbuf_ref, xs_ref, zsem, wait):
    ln = pl_ref[e]
    st = ps_ref[e]
    b = 1
    while b <= ZERO_ROWS:
        @pl.when((ln & b) != 0)
        def _(b=b):
            cp = _prow_copy(zbuf_ref, 0, xs_ref, st + (ln & (b - 1)), zsem, rows=b)
            if wait:
                cp.wait()
            else:
                cp.start()
        b *= 2


def _dispatch_kernel(pos0_ref, pos1_ref, ps_ref, pl_ref, h1p_ref, xs_ref, zbuf_ref, sem, zsem):
    i = pl.program_id(0)
    tm = TM_DISP

    @pl.when(i == 0)
    def _():
        zbuf_ref[...] = jnp.zeros_like(zbuf_ref)

        def fill_start(e, _):
            _pad_fill(e, ps_ref, pl_ref, zbuf_ref, xs_ref, zsem, False)
            return 0

        lax.fori_loop(0, N_EXPERTS, fill_start, 0)

    def issue(k, _):
        for u in range(DMA_UNROLL):
            r = k * DMA_UNROLL + u
            tok = i * tm + r
            _prow_copy(h1p_ref, r, xs_ref, pos0_ref[tok], sem).start(priority=0)
            _prow_copy(h1p_ref, r, xs_ref, pos1_ref[tok], sem).start(priority=1)
        return 0

    lax.fori_loop(0, tm // DMA_UNROLL, issue, 0)

    def drain(k, _):
        for u in range(DMA_UNROLL):
            _prow_copy(h1p_ref, 0, xs_ref, 0, sem).wait()
            _prow_copy(h1p_ref, 0, xs_ref, 0, sem).wait()
        return 0

    lax.fori_loop(0, tm // DMA_UNROLL, drain, 0)

    @pl.when(i == pl.num_programs(0) - 1)
    def _():
        def fill_wait(e, _):
            _pad_fill(e, ps_ref, pl_ref, zbuf_ref, xs_ref, zsem, True)
            return 0

        lax.fori_loop(0, N_EXPERTS, fill_wait, 0)


def _dispatch(pos0, pos1, pad_start, pad_len, h1p, n_rows):
    n = h1p.shape[0]
    return pl.pallas_call(
        _dispatch_kernel,
        grid_spec=pltpu.PrefetchScalarGridSpec(
            num_scalar_prefetch=4,
            grid=(n // TM_DISP,),
            in_specs=[pl.BlockSpec((TM_DISP, RT, LANES), lambda i, *_: (i, 0, 0))],
            out_specs=pl.BlockSpec(memory_space=pl.ANY),
            scratch_shapes=[
                pltpu.VMEM((ZERO_ROWS, RT, LANES), BF16),
                pltpu.SemaphoreType.DMA(()),
                pltpu.SemaphoreType.DMA(()),
            ],
        ),
        out_shape=jax.ShapeDtypeStruct((n_rows, RT, LANES), BF16),
        compiler_params=pltpu.CompilerParams(
            dimension_semantics=("arbitrary",),
            vmem_limit_bytes=VMEM_LIMIT),
        name="dispatch",
    )(pos0, pos1, pad_start, pad_len, h1p)


def _expert_kernel(te_ref, nu_ref, nx_ref, sl_ref, xs_ref, wg_hbm, wu_hbm, wd_hbm, ys_ref,
                   wgf_ref, wuf_ref, wdf_ref, wgb_ref, wub_ref, wdb_ref, stage_ref, wsem):
    i = pl.program_id(0)
    used = i < nu_ref[0]
    e = te_ref[i]
    s = sl_ref[i]
    fresh = (i == 0) | (e != te_ref[jnp.maximum(i - 1, 0)])

    def weight_copies(expert, slot):
        return (pltpu.make_async_copy(wg_hbm.at[expert], wgf_ref.at[slot], wsem.at[slot, 0]),
                pltpu.make_async_copy(wu_hbm.at[expert], wuf_ref.at[slot], wsem.at[slot, 1]),
                pltpu.make_async_copy(wd_hbm.at[expert], wdf_ref.at[slot], wsem.at[slot, 2]))

    @pl.when(i == 0)
    def _():
        for cp in weight_copies(e, s):
            cp.start()

    @pl.when(used & fresh)
    def _():
        for cp in weight_copies(e, s):
            cp.wait()

        @pl.when(nx_ref[i] >= 0)
        def _():
            for cp in weight_copies(nx_ref[i], 1 - s):
                cp.start()

        wgb_ref[...] = wgf_ref[s].astype(BF16)
        wub_ref[...] = wuf_ref[s].astype(BF16)
        wdb_ref[...] = wdf_ref[s].astype(BF16)

    @pl.when(used)
    def _():
        x = jnp.concatenate([p.astype(BF16) for p in _load_rows(xs_ref, 0, TM_EXP, stage_ref)],
                            axis=1)
        g = jnp.dot(x, wgb_ref[...], preferred_element_type=F32)
        u = jnp.dot(x, wub_ref[...], preferred_element_type=F32)
        h = (g * _sigmoid(g) * u).astype(BF16)
        y = jnp.dot(h, wdb_ref[...], preferred_element_type=F32)
        _store_rows(ys_ref, 0, TM_EXP, y, stage_ref)


def _expert_ffn(tile_expert, n_used, next_expert, slot, xs, wg, wu, wd):
    n_rows = xs.shape[0]
    n_tiles = n_rows // TM_EXP

    def row_map(i, te, nu, nx, sl):
        return (jnp.minimum(i, nu[0] - 1), 0, 0)

    return pl.pallas_call(
        _expert_kernel,
        grid_spec=pltpu.PrefetchScalarGridSpec(
            num_scalar_prefetch=4,
            grid=(n_tiles,),
            in_specs=[
                pl.BlockSpec((TM_EXP, RT, LANES), row_map),
                pl.BlockSpec(memory_space=pl.ANY),
                pl.BlockSpec(memory_space=pl.ANY),
                pl.BlockSpec(memory_space=pl.ANY),
            ],
            out_specs=pl.BlockSpec((TM_EXP, RT, LANES), row_map),
            scratch_shapes=[
                pltpu.VMEM((2, D_MODEL, D_EXPERT), F32),
                pltpu.VMEM((2, D_MODEL, D_EXPERT), F32),
                pltpu.VMEM((2, D_EXPERT, D_MODEL), F32),
                pltpu.VMEM((D_MODEL, D_EXPERT), BF16),
                pltpu.VMEM((D_MODEL, D_EXPERT), BF16),
                pltpu.VMEM((D_EXPERT, D_MODEL), BF16),
                pltpu.VMEM((TM_EXP * PITCH, LANES), F32),
                pltpu.SemaphoreType.DMA((2, 3)),
            ],
        ),
        out_shape=jax.ShapeDtypeStruct((n_rows, RT, LANES), BF16),
        compiler_params=pltpu.CompilerParams(
            dimension_semantics=("arbitrary",),
            vmem_limit_bytes=VMEM_LIMIT),
        name="expert_ffn",
    )(tile_expert, n_used, next_expert, slot, xs, wg, wu, wd)


def _combine_kernel(pos0_ref, pos1_ref, h1_ref, info_ref, g2_ref, b2_ref, ys_ref, o_ref,
                    ybuf_ref, stage_ref, sem):
    i = pl.program_id(0)
    tm = TM_COMB
    par = lax.rem(i, 2)

    def issue(step, parity):
        def body(k, _):
            for u in range(DMA_UNROLL):
                r = k * DMA_UNROLL + u
                tok = step * tm + r
                _prow_copy(ys_ref, pos0_ref[tok], ybuf_ref.at[parity, 0], r,
                           sem.at[parity]).start(priority=0)
                _prow_copy(ys_ref, pos1_ref[tok], ybuf_ref.at[parity, 1], r,
                           sem.at[parity]).start(priority=1)
            return 0
        lax.fori_loop(0, tm // DMA_UNROLL, body, 0)

    def drain(parity):
        def body(k, _):
            for u in range(DMA_UNROLL):
                _prow_copy(ys_ref, 0, ybuf_ref.at[parity, 0], 0, sem.at[parity]).wait()
                _prow_copy(ys_ref, 0, ybuf_ref.at[parity, 1], 0, sem.at[parity]).wait()
            return 0
        lax.fori_loop(0, tm // DMA_UNROLL, body, 0)

    @pl.when(i == 0)
    def _():
        issue(0, 0)

    @pl.when(i + 1 < pl.num_programs(0))
    def _():
        issue(i + 1, 1 - par)

    drain(par)

    rows = 64
    for c in range(tm // rows):
        rs = slice(c * rows, (c + 1) * rows)
        q1 = info_ref[rs, 2:3]
        q2 = info_ref[rs, 3:4]
        y0 = _load_rows(ybuf_ref.at[par, 0], c * rows, rows, stage_ref)
        y1 = _load_rows(ybuf_ref.at[par, 1], c * rows, rows, stage_ref)
        ffn = jnp.concatenate([q1 * a + q2 * b for a, b in zip(y0, y1)], axis=1)
        o_ref[rs, :] = _ln_rows(DN_ALPHA * h1_ref[rs, :] + ffn, g2_ref[...], b2_ref[...])


def _combine(pos0, pos1, h1, info, g2, b2, ys):
    n = h1.shape[0]
    return pl.pallas_call(
        _combine_kernel,
        grid_spec=pltpu.PrefetchScalarGridSpec(
            num_scalar_prefetch=2,
            grid=(n // TM_COMB,),
            in_specs=[
                pl.BlockSpec((TM_COMB, D_MODEL), lambda i, *_: (i, 0)),
                pl.BlockSpec((TM_COMB, ROUTE_W), lambda i, *_: (i, 0)),
                pl.BlockSpec((1, D_MODEL), lambda i, *_: (0, 0)),
                pl.BlockSpec((1, D_MODEL), lambda i, *_: (0, 0)),
                pl.BlockSpec(memory_space=pl.ANY),
            ],
            out_specs=pl.BlockSpec((TM_COMB, D_MODEL), lambda i, *_: (i, 0)),
            scratch_shapes=[
                pltpu.VMEM((2, 2, TM_COMB, RT, LANES), BF16),
                pltpu.VMEM((64 * PITCH, LANES), F32),
                pltpu.SemaphoreType.DMA((2,)),
            ],
        ),
        out_shape=jax.ShapeDtypeStruct((n, D_MODEL), F32),
        compiler_params=pltpu.CompilerParams(
            dimension_semantics=("arbitrary",),
            vmem_limit_bytes=VMEM_LIMIT),
        name="combine",
    )(pos0, pos1, h1, info, g2, b2, ys)


def _block_diag(w, per):
    h, hd, _ = w.shape
    wg = w.reshape(h // per, per, hd, hd)
    eye = jnp.eye(per, dtype=w.dtype)
    return jnp.einsum("gpij,pq->gpiqj", wg, eye).reshape(h // per, per * hd, per * hd)


def kernel(x, ln_in_g, ln_in_b, w_in, lru_conv_w, lru_conv_b, lru_w_a, lru_b_a, lru_w_x, lru_b_x,
           lru_lambda, conf_conv_w, conf_conv_b, conf_ln_g, conf_ln_b, w_out, ln1_g, ln1_b,
           router_group_w, router_group_b, router_expert_w, router_expert_b, exp_w_gate, exp_w_up,
           exp_w_down, ln2_g, ln2_b):
    bsz, seq, d = x.shape
    n = bsz * seq
    x2 = x.reshape(n, d)
    row = lambda v: v.reshape(1, -1).astype(F32)
    l = 0

    z, h0 = _ln_win(x2, row(ln_in_g), row(ln_in_b), w_in[l])

    per = CB_LRU // LRU_HEAD_DIM
    wcat = jnp.concatenate([_block_diag(lru_w_a[l], per), _block_diag(lru_w_x[l], per)],
                           axis=-1).astype(BF16)
    a_out = _lru_mixer(z, lru_conv_w[l], row(lru_conv_b[l]), wcat, row(lru_b_a[l]),
                       row(lru_b_x[l]), row(lru_lambda[l]), bsz, seq)

    nlb = D_CONV // LANES
    w3 = jnp.pad(conf_conv_w[l], ((0, 32 - CONF_CONV_W), (0, 0)))
    w3 = w3.reshape(32, nlb, LANES).transpose(1, 0, 2)
    cb3 = conf_conv_b[l].reshape(nlb, 1, LANES)
    b_out, wo = _conf_mixer(z, w3, cb3, row(conf_ln_g[l]), row(conf_ln_b[l]), w_out[l], bsz, seq)

    wr = jnp.concatenate([router_group_w[l], router_expert_w[l]], axis=1)
    wr = jnp.pad(wr, ((0, 0), (0, ROUTE_W - wr.shape[1])))
    wr_hi = wr.astype(BF16)
    wr_lo = (wr - wr_hi.astype(F32)).astype(BF16)
    wr_cat = jnp.concatenate([wr_hi, wr_lo], axis=1)
    br = jnp.concatenate([router_group_b[l], router_expert_b[l]])
    br = jnp.pad(br, (0, ROUTE_W - br.shape[0])).reshape(1, ROUTE_W)
    h1, h1p, logits = _wout_router(a_out, b_out, h0, wo, row(ln1_g[l]), row(ln1_b[l]), wr_cat, br)

    info, counts = _route(logits)
    idx = info[:, 0:6].astype(jnp.int32)
    r0, r1, e0, e1 = idx[:, 0], idx[:, 1], idx[:, 4], idx[:, 5]

    n_tiles = (n * 2) // TM_EXP + N_EXPERTS
    tile_expert, n_used, next_expert, slot, starts, pad_start, pad_len = _plan(
        counts[0].astype(jnp.int32), n_tiles)

    expert_ids = jnp.arange(N_EXPERTS, dtype=jnp.int32)[None, :]
    start_of = lambda e: jnp.sum(jnp.where(e[:, None] == expert_ids, starts[None, :], 0), axis=1)
    pos0 = (start_of(e0) + r0).astype(jnp.int32)
    pos1 = (start_of(e1) + r1).astype(jnp.int32)

    xs = _dispatch(pos0, pos1, pad_start, pad_len, h1p, n_tiles * TM_EXP)
    shp = (N_EXPERTS, D_MODEL, D_EXPERT)
    ys = _expert_ffn(tile_expert, n_used, next_expert, slot, xs, exp_w_gate[l].reshape(shp),
                     exp_w_up[l].reshape(shp), exp_w_down[l].reshape(N_EXPERTS, D_EXPERT, D_MODEL))
    out = _combine(pos0, pos1, h1, info, row(ln2_g[l]), row(ln2_b[l]), ys)
    return out.reshape(bsz, seq, d)
```

```python
import functools
import math

import jax
import jax.numpy as jnp
from jax import lax
from jax.experimental import pallas as pl
from jax.experimental.pallas import tpu as pltpu

F32 = jnp.float32
BF16 = jnp.bfloat16

D_MODEL = 2048
D_LRU = 1024
D_CONV = 1024
LRU_HEADS = 16
LRU_HEAD_DIM = 64
LRU_C = 8.0
LRU_CONV_W = 4
CONF_CONV_W = 31
N_GROUPS = 4
EXPERTS_PER_GROUP = 8
N_EXPERTS = N_GROUPS * EXPERTS_PER_GROUP
D_EXPERT = 512
LN_EPS = 1e-5
DEPTH = 1
DN_ALPHA = (2 * DEPTH) ** 0.25

LANES = 128
SUBLANES = 8
VMEM_LIMIT = 56 * 1024 * 1024

TM_WIN = 512
TN_WIN = 1024
W_CHUNK = 256
TT_LRU = 1024
CB_LRU = 256
TT_CONF = 256
CONF_HALO = 32
TM_OUT = 512
TM_ROUTE = 512
TM_DISP = 512
TM_EXP = 256
TM_COMB = 256
ROUTE_W = 128


def _sigmoid(x):
    return 0.5 * (jnp.tanh(0.5 * x) + 1.0)


def _ln_rows(x, g, b):
    mu = jnp.mean(x, axis=-1, keepdims=True)
    xc = x - mu
    var = jnp.mean(xc * xc, axis=-1, keepdims=True)
    return xc * lax.rsqrt(var + LN_EPS) * g + b


def _stage_weight(w_hbm, wb_ref, wst_ref, wsem):
    nchunk = wb_ref.shape[1] // W_CHUNK

    def chunk_copy(c):
        return pltpu.make_async_copy(w_hbm.at[:, pl.ds(c * W_CHUNK, W_CHUNK)],
                                     wst_ref.at[c % 2], wsem.at[c % 2])

    chunk_copy(0).start()
    for c in range(nchunk):
        if c + 1 < nchunk:
            chunk_copy(c + 1).start()
        chunk_copy(c).wait()
        wb_ref[:, c * W_CHUNK:(c + 1) * W_CHUNK] = wst_ref[c % 2].astype(BF16)


def _ln_win_kernel(x_ref, g_ref, b_ref, w_hbm, z_ref, h_ref, wb_ref, wst_ref, xn_ref, wsem):
    s = pl.program_id(0)
    par = lax.rem(s, 2)

    @pl.when(s == 0)
    def _():
        xn_ref[1] = jnp.zeros(xn_ref.shape[1:], BF16)
        _stage_weight(w_hbm, wb_ref, wst_ref, wsem)

    rows = 128
    for c in range(TM_WIN // rows):
        rs = slice(c * rows, (c + 1) * rows)
        hn = _ln_rows(x_ref[rs, :], g_ref[...], b_ref[...])
        h_ref[rs, :] = hn
        xn_ref[par, rs, :] = hn.astype(BF16)

    xprev = xn_ref[1 - par]
    for c in range(z_ref.shape[1] // TN_WIN):
        cs = slice(c * TN_WIN, (c + 1) * TN_WIN)
        z_ref[:, cs] = jnp.dot(xprev, wb_ref[:, cs], preferred_element_type=F32).astype(z_ref.dtype)


def _ln_win(x2, g, b, w):
    n = x2.shape[0]
    ncol = w.shape[1]
    nt = n // TM_WIN
    return pl.pallas_call(
        _ln_win_kernel,
        grid=(nt + 1,),
        in_specs=[
            pl.BlockSpec((TM_WIN, D_MODEL), lambda s: (jnp.minimum(s, nt - 1), 0)),
            pl.BlockSpec((1, D_MODEL), lambda s: (0, 0)),
            pl.BlockSpec((1, D_MODEL), lambda s: (0, 0)),
            pl.BlockSpec(memory_space=pl.ANY),
        ],
        out_specs=[
            pl.BlockSpec((TM_WIN, ncol), lambda s: (jnp.maximum(s - 1, 0), 0)),
            pl.BlockSpec((TM_WIN, D_MODEL), lambda s: (jnp.minimum(s, nt - 1), 0)),
        ],
        out_shape=[
            jax.ShapeDtypeStruct((n, ncol), BF16),
            jax.ShapeDtypeStruct((n, D_MODEL), F32),
        ],
        scratch_shapes=[
            pltpu.VMEM((D_MODEL, ncol), BF16),
            pltpu.VMEM((2, D_MODEL, W_CHUNK), F32),
            pltpu.VMEM((2, TM_WIN, D_MODEL), BF16),
            pltpu.SemaphoreType.DMA((2,)),
        ],
        compiler_params=pltpu.CompilerParams(
            dimension_semantics=("arbitrary",),
            vmem_limit_bytes=VMEM_LIMIT),
        name="ln_win",
    )(x2, g, b, w)


def _lru_kernel(zx_ref, zg_ref, cw_ref, cb_ref, wcat_ref, ba_ref, bx_ref, lam_ref,
                o_ref, xs_ref, hp_ref, a_ref, g_ref):
    t = pl.program_id(2)
    tt = TT_LRU

    @pl.when(t == 0)
    def _():
        xs_ref[0:SUBLANES, :] = jnp.zeros((SUBLANES, CB_LRU), F32)
        hp_ref[...] = jnp.zeros_like(hp_ref)

    @pl.when(t > 0)
    def _():
        xs_ref[0:SUBLANES, :] = xs_ref[tt:tt + SUBLANES, :]

    xs_ref[SUBLANES:SUBLANES + tt, :] = zx_ref[...].astype(F32)

    rows = 128
    for rb in range(tt // rows):
        acc = jnp.broadcast_to(cb_ref[...], (rows, CB_LRU))
        for k in range(LRU_CONV_W):
            off = rb * rows + SUBLANES - (LRU_CONV_W - 1) + k
            acc = acc + cw_ref[k:k + 1, :] * xs_ref[off:off + rows, :]
        a_ref[rb * rows:(rb + 1) * rows, :] = acc

    g_ref[...] = jnp.dot(a_ref[...].astype(BF16), wcat_ref[0], preferred_element_type=F32)

    lam = lam_ref[...]
    softplus_neg = jnp.maximum(-lam, 0.0) + jnp.log1p(jnp.exp(-jnp.abs(lam)))
    cvec = -LRU_C * softplus_neg
    ba = ba_ref[...]
    bx = bx_ref[...]
    blk = 64
    row_in_vreg = lax.broadcasted_iota(jnp.int32, (blk, CB_LRU), 0) & (SUBLANES - 1)

    def body(rb, h):
        rs = pl.ds(pl.multiple_of(rb * blk, blk), blk)
        a_in = a_ref[rs, :]
        r = _sigmoid(g_ref[rs, 0:CB_LRU] + ba)
        i = _sigmoid(g_ref[rs, CB_LRU:2 * CB_LRU] + bx)
        log_a = cvec * r
        a = jnp.exp(log_a)
        u = jnp.sqrt(-jnp.tanh(log_a) * (a * a + 1.0)) * (i * a_in)
        for s in (1, 2, 4):
            m = row_in_vreg >= s
            a_sh = jnp.where(m, pltpu.roll(a, s, 0), 1.0)
            u_sh = jnp.where(m, pltpu.roll(u, s, 0), 0.0)
            u = u + a * u_sh
            a = a * a_sh
        outs = []
        for gi in range(blk // SUBLANES):
            ag = a[gi * SUBLANES:(gi + 1) * SUBLANES, :]
            ug = u[gi * SUBLANES:(gi + 1) * SUBLANES, :]
            hg = ug + ag * h
            h = hg[SUBLANES - 1:SUBLANES, :]
            outs.append(hg)
        hblk = jnp.concatenate(outs, axis=0)
        gl = zg_ref[rs, :].astype(F32)
        gelu = 0.5 * gl * (1.0 + jnp.tanh(0.7978845608028654 * (gl + 0.044715 * gl * gl * gl)))
        o_ref[rs, :] = (gelu * hblk).astype(o_ref.dtype)
        return h

    h = lax.fori_loop(0, tt // blk, body, hp_ref[0:1, :])
    hp_ref[...] = jnp.broadcast_to(h, hp_ref.shape)


def _lru_mixer(z, cw, cb, wcat, ba, bx, lam, bsz, seq):
    n = z.shape[0]
    nt = seq // TT_LRU
    ncb = D_LRU // CB_LRU
    row = lambda b, j, t: b * nt + t
    vec = pl.BlockSpec((1, CB_LRU), lambda b, j, t: (0, j))
    return pl.pallas_call(
        _lru_kernel,
        grid=(bsz, ncb, nt),
        in_specs=[
            pl.BlockSpec((TT_LRU, CB_LRU), lambda b, j, t: (row(b, j, t), j)),
            pl.BlockSpec((TT_LRU, CB_LRU), lambda b, j, t: (row(b, j, t), ncb + j)),
            pl.BlockSpec((LRU_CONV_W, CB_LRU), lambda b, j, t: (0, j)),
            vec,
            pl.BlockSpec((1, CB_LRU, 2 * CB_LRU), lambda b, j, t: (j, 0, 0)),
            vec, vec, vec,
        ],
        out_specs=pl.BlockSpec((TT_LRU, CB_LRU), lambda b, j, t: (row(b, j, t), j)),
        out_shape=jax.ShapeDtypeStruct((n, D_LRU), BF16),
        scratch_shapes=[
            pltpu.VMEM((TT_LRU + SUBLANES, CB_LRU), F32),
            pltpu.VMEM((SUBLANES, CB_LRU), F32),
            pltpu.VMEM((TT_LRU, CB_LRU), F32),
            pltpu.VMEM((TT_LRU, 2 * CB_LRU), F32),
        ],
        compiler_params=pltpu.CompilerParams(
            dimension_semantics=("arbitrary", "arbitrary", "arbitrary"),
            vmem_limit_bytes=VMEM_LIMIT),
        name="lru_mixer",
    )(z, z, cw, cb, wcat, ba, bx, lam)


def _conf_kernel(zv_ref, zg_ref, w_ref, cb_ref, lg_ref, lb_ref, wo_ref, o_ref, wob_ref,
                 cs_ref, cv_ref):
    t = pl.program_id(1)
    tt = TT_CONF
    nlb = D_CONV // LANES
    wob_ref[...] = wo_ref[...].astype(BF16)

    @pl.when(t == 0)
    def _():
        cs_ref[:, 0:CONF_HALO, :] = jnp.zeros((nlb, CONF_HALO, LANES), F32)

    @pl.when(t > 0)
    def _():
        cs_ref[:, 0:CONF_HALO, :] = cs_ref[:, tt:tt + CONF_HALO, :]

    for c in range(nlb):
        ls = slice(c * LANES, (c + 1) * LANES)
        v = zv_ref[:, ls].astype(F32)
        g = zg_ref[:, ls].astype(F32)
        cs_ref[c, CONF_HALO:CONF_HALO + tt, :] = v * _sigmoid(g)

    rows = 64
    nrb = tt // rows
    base = CONF_HALO - (CONF_CONV_W - 1)

    def conv_body(c, carry):
        accs = [jnp.broadcast_to(cb_ref[c], (rows, LANES)) for _ in range(nrb)]
        for k in range(CONF_CONV_W):
            wk = w_ref[c, k:k + 1, :]
            for rb in range(nrb):
                off = rb * rows + base + k
                accs[rb] = accs[rb] + wk * cs_ref[c, off:off + rows, :]
        for rb in range(nrb):
            cv_ref[c, rb * rows:(rb + 1) * rows, :] = accs[rb]
        return carry

    lax.fori_loop(0, nlb, conv_body, 0)

    ln_rows = 32
    inv_n = 1.0 / D_CONV
    for rb in range(tt // ln_rows):
        rs = slice(rb * ln_rows, (rb + 1) * ln_rows)
        blk = cv_ref[:, rs, :]
        mu = jnp.sum(jnp.sum(blk, axis=0), axis=-1, keepdims=True) * inv_n
        d = blk - mu[None]
        var = jnp.sum(jnp.sum(d * d, axis=0), axis=-1, keepdims=True) * inv_n
        inv = lax.rsqrt(var + LN_EPS)
        for c in range(nlb):
            ls = slice(c * LANES, (c + 1) * LANES)
            y = d[c] * inv * lg_ref[:, ls] + lb_ref[:, ls]
            o_ref[rs, ls] = (y * _sigmoid(y)).astype(o_ref.dtype)


def _conf_mixer(z, w3, cb3, lg, lb, wo, bsz, seq):
    n = z.shape[0]
    nt = seq // TT_CONF
    nlb = D_CONV // LANES
    wo_rows = wo.shape[0] // (bsz * nt)
    assert wo_rows * bsz * nt == wo.shape[0] and wo_rows % 16 == 0
    return pl.pallas_call(
        _conf_kernel,
        grid=(bsz, nt),
        in_specs=[
            pl.BlockSpec((TT_CONF, D_CONV), lambda b, t: (b * nt + t, 2)),
            pl.BlockSpec((TT_CONF, D_CONV), lambda b, t: (b * nt + t, 3)),
            pl.BlockSpec((nlb, 32, LANES), lambda b, t: (0, 0, 0)),
            pl.BlockSpec((nlb, 1, LANES), lambda b, t: (0, 0, 0)),
            pl.BlockSpec((1, D_CONV), lambda b, t: (0, 0)),
            pl.BlockSpec((1, D_CONV), lambda b, t: (0, 0)),
            pl.BlockSpec((wo_rows, D_MODEL), lambda b, t: (b * nt + t, 0)),
        ],
        out_specs=[
            pl.BlockSpec((TT_CONF, D_CONV), lambda b, t: (b * nt + t, 0)),
            pl.BlockSpec((wo_rows, D_MODEL), lambda b, t: (b * nt + t, 0)),
        ],
        out_shape=[
            jax.ShapeDtypeStruct((n, D_CONV), BF16),
            jax.ShapeDtypeStruct(wo.shape, BF16),
        ],
        scratch_shapes=[
            pltpu.VMEM((nlb, CONF_HALO + TT_CONF, LANES), F32),
            pltpu.VMEM((nlb, TT_CONF, LANES), F32),
        ],
        compiler_params=pltpu.CompilerParams(
            dimension_semantics=("arbitrary", "arbitrary"),
            vmem_limit_bytes=VMEM_LIMIT),
        name="conf_mixer",
    )(z, z, w3, cb3, lg, lb, wo)


def _split_bf16(v):
    hi = v.astype(BF16)
    lo = (v - hi.astype(F32)).astype(BF16)
    return hi, lo


RT = D_MODEL // LANES
PITCH = RT + SUBLANES


def _store_rows(dst_ref, row0, rows, v, stage_ref):
    for s in range(RT):
        stage_ref[pl.ds(s, rows, stride=PITCH), :] = v[:, s * LANES:(s + 1) * LANES]
    staged = stage_ref[0:rows * PITCH, :].reshape(rows, PITCH, LANES)
    dst_ref[row0:row0 + rows] = staged[:, 0:RT, :].astype(BF16)


def _load_rows(src_ref, row0, rows, stage_ref):
    tile = src_ref[row0:row0 + rows].astype(F32)
    tile = jnp.concatenate([tile, jnp.zeros((rows, PITCH - RT, LANES), F32)], axis=1)
    stage_ref[0:rows * PITCH, :] = tile.reshape(rows * PITCH, LANES)
    return [stage_ref[pl.ds(s, rows, stride=PITCH), :] for s in range(RT)]


def _wout_kernel(a_ref, b_ref, h_ref, wa_ref, wb_ref, g1_ref, b1_ref,
                 wr_ref, br_ref, h1_ref, h1r_ref, lg_ref, mix_ref, hl_ref, stage_ref):
    mix_ref[...] = (jnp.dot(a_ref[...], wa_ref[...], preferred_element_type=F32)
                    + jnp.dot(b_ref[...], wb_ref[...], preferred_element_type=F32))
    rows = 64
    for c in range(TM_OUT // rows):
        rs = slice(c * rows, (c + 1) * rows)
        h1 = _ln_rows(DN_ALPHA * h_ref[rs, :] + mix_ref[rs, :], g1_ref[...], b1_ref[...])
        h1_ref[rs, :] = h1
        _store_rows(h1r_ref, c * rows, rows, h1, stage_ref)
        hi, lo = _split_bf16(h1)
        hl_ref[rs, :] = hi
        hl_ref[TM_OUT + c * rows:TM_OUT + (c + 1) * rows, :] = lo
    p = jnp.dot(hl_ref[...], wr_ref[...], preferred_element_type=F32)
    lg_ref[...] = (p[0:TM_OUT, 0:ROUTE_W] + p[0:TM_OUT, ROUTE_W:2 * ROUTE_W]
                   + p[TM_OUT:2 * TM_OUT, 0:ROUTE_W] + br_ref[...])


def _wout_router(a, b, h, wo, g1, b1, wr_cat, br):
    n = h.shape[0]
    assert D_LRU == D_CONV
    full = lambda shape: pl.BlockSpec(shape, lambda i: tuple(0 for _ in shape))
    return pl.pallas_call(
        _wout_kernel,
        grid=(n // TM_OUT,),
        in_specs=[
            pl.BlockSpec((TM_OUT, D_LRU), lambda i: (i, 0)),
            pl.BlockSpec((TM_OUT, D_CONV), lambda i: (i, 0)),
            pl.BlockSpec((TM_OUT, D_MODEL), lambda i: (i, 0)),
            pl.BlockSpec((D_LRU, D_MODEL), lambda i: (0, 0)),
            pl.BlockSpec((D_CONV, D_MODEL), lambda i: (1, 0)),
            full((1, D_MODEL)), full((1, D_MODEL)),
            full((D_MODEL, 2 * ROUTE_W)), full((1, ROUTE_W)),
        ],
        out_specs=[
            pl.BlockSpec((TM_OUT, D_MODEL), lambda i: (i, 0)),
            pl.BlockSpec((TM_OUT, RT, LANES), lambda i: (i, 0, 0)),
            pl.BlockSpec((TM_OUT, ROUTE_W), lambda i: (i, 0)),
        ],
        out_shape=[
            jax.ShapeDtypeStruct((n, D_MODEL), F32),
            jax.ShapeDtypeStruct((n, RT, LANES), BF16),
            jax.ShapeDtypeStruct((n, ROUTE_W), F32),
        ],
        scratch_shapes=[
            pltpu.VMEM((TM_OUT, D_MODEL), F32),
            pltpu.VMEM((2 * TM_OUT, D_MODEL), BF16),
            pltpu.VMEM((64 * PITCH, LANES), F32),
        ],
        compiler_params=pltpu.CompilerParams(
            dimension_semantics=("arbitrary",),
            vmem_limit_bytes=VMEM_LIMIT),
        name="wout_router",
    )(a, b, h, wo, wo, g1, b1, wr_cat, br)


def _route_kernel(lg_ref, info_ref, cnt_ref, run_ref, tri_ref):
    t = pl.program_id(0)
    tm = TM_ROUTE
    l = lg_ref[...]
    lane = lax.broadcasted_iota(jnp.int32, (tm, ROUTE_W), 1)
    neg = jnp.float32(-jnp.inf)
    big = jnp.int32(1 << 20)

    gmask = lane < N_GROUPS
    gmax = jnp.max(jnp.where(gmask, l, neg), axis=-1, keepdims=True)
    gsel = jnp.min(jnp.where(gmask & (l == gmax), lane, big), axis=-1, keepdims=True)
    gsum = jnp.sum(jnp.where(gmask, jnp.exp(l - gmax), 0.0), axis=-1, keepdims=True)
    pg_top = 1.0 / gsum

    lo = N_GROUPS + EXPERTS_PER_GROUP * gsel
    emask = (lane >= lo) & (lane < lo + EXPERTS_PER_GROUP)
    v1 = jnp.max(jnp.where(emask, l, neg), axis=-1, keepdims=True)
    i1 = jnp.min(jnp.where(emask & (l == v1), lane, big), axis=-1, keepdims=True)
    emask2 = emask & (lane != i1)
    v2 = jnp.max(jnp.where(emask2, l, neg), axis=-1, keepdims=True)
    i2 = jnp.min(jnp.where(emask2 & (l == v2), lane, big), axis=-1, keepdims=True)
    e21 = jnp.exp(v2 - v1)
    q1 = pg_top / (1.0 + e21)
    q2 = pg_top * e21 / (1.0 + e21)

    oh1 = (lane == i1).astype(F32)
    oh2 = (lane == i2).astype(F32)
    ohs = oh1 + oh2

    @pl.when(t == 0)
    def _():
        run_ref[...] = jnp.zeros_like(run_ref)
        r_i = lax.broadcasted_iota(jnp.int32, (tm, tm), 0)
        c_i = lax.broadcasted_iota(jnp.int32, (tm, tm), 1)
        tri_ref[...] = (c_i < r_i).astype(BF16)

    cum = jnp.dot(tri_ref[...], ohs.astype(BF16), preferred_element_type=F32)
    basev = run_ref[0:1, :] + cum
    r1 = jnp.sum(oh1 * basev, axis=-1, keepdims=True)
    r2 = jnp.sum(oh2 * basev, axis=-1, keepdims=True)
    run_ref[...] = run_ref[...] + jnp.sum(ohs, axis=0, keepdims=True)
    info = jnp.where(lane == 0, r1, 0.0)
    info = jnp.where(lane == 1, r2, info)
    info = jnp.where(lane == 2, q1, info)
    info = jnp.where(lane == 3, q2, info)
    info = jnp.where(lane == 4, (i1 - N_GROUPS).astype(F32), info)
    info = jnp.where(lane == 5, (i2 - N_GROUPS).astype(F32), info)
    info_ref[...] = info
    cnt_ref[...] = run_ref[...]


def _route(logits):
    n = logits.shape[0]
    nt = n // TM_ROUTE
    return pl.pallas_call(
        _route_kernel,
        grid=(nt,),
        in_specs=[pl.BlockSpec((TM_ROUTE, ROUTE_W), lambda t: (t, 0))],
        out_specs=[
            pl.BlockSpec((TM_ROUTE, ROUTE_W), lambda t: (t, 0)),
            pl.BlockSpec((SUBLANES, ROUTE_W), lambda t: (0, 0)),
        ],
        out_shape=[
            jax.ShapeDtypeStruct((n, ROUTE_W), F32),
            jax.ShapeDtypeStruct((SUBLANES, ROUTE_W), F32),
        ],
        scratch_shapes=[
            pltpu.VMEM((SUBLANES, ROUTE_W), F32),
            pltpu.VMEM((TM_ROUTE, TM_ROUTE), BF16),
        ],
        compiler_params=pltpu.CompilerParams(
            dimension_semantics=("arbitrary",),
            vmem_limit_bytes=VMEM_LIMIT),
        name="route",
    )(logits)


def _plan_kernel(cnt_ref, te_ref, nu_ref, nx_ref, sl_ref, st_ref, ps_ref, pl_ref):
    n_tiles = te_ref.shape[0]
    shift = TM_EXP.bit_length() - 1

    def forward(e, carry):
        tile, parity = carry
        c = cnt_ref[N_GROUPS + e]
        tp = (c + (TM_EXP - 1)) >> shift
        st_ref[e] = tile * TM_EXP
        ps_ref[e] = tile * TM_EXP + c
        pl_ref[e] = tp * TM_EXP - c

        def mark(k, _):
            te_ref[tile + k] = e
            sl_ref[tile + k] = parity
            return 0

        lax.fori_loop(0, tp, mark, 0)
        return tile + tp, jnp.where(tp > 0, 1 - parity, parity)

    used, _ = lax.fori_loop(0, N_EXPERTS, forward, (jnp.int32(0), jnp.int32(0)))
    nu_ref[0] = used

    def backward(j, nxt):
        e = N_EXPERTS - 1 - j
        c = cnt_ref[N_GROUPS + e]
        tp = (c + (TM_EXP - 1)) >> shift
        first = st_ref[e] >> shift

        def mark(k, _):
            nx_ref[first + k] = nxt
            return 0

        lax.fori_loop(0, tp, mark, 0)
        return jnp.where(tp > 0, e, nxt)

    lax.fori_loop(0, N_EXPERTS, backward, jnp.int32(-1))

    last_e = te_ref[jnp.maximum(used - 1, 0)]
    last_s = sl_ref[jnp.maximum(used - 1, 0)]

    def tail(i, _):
        te_ref[i] = last_e
        sl_ref[i] = last_s
        nx_ref[i] = -1
        return 0

    lax.fori_loop(used, n_tiles, tail, 0)


def _plan(counts_i32, n_tiles):
    smem = lambda: pl.BlockSpec(memory_space=pltpu.SMEM)
    i32 = lambda k: jax.ShapeDtypeStruct((k,), jnp.int32)
    return pl.pallas_call(
        _plan_kernel,
        in_specs=[smem()],
        out_specs=[smem() for _ in range(7)],
        out_shape=[i32(n_tiles), i32(1), i32(n_tiles), i32(n_tiles),
                   i32(N_EXPERTS), i32(N_EXPERTS), i32(N_EXPERTS)],
        name="plan",
    )(counts_i32)


def _prow_copy(src_ref, src_row, dst_ref, dst_row, sem, rows=1):
    return pltpu.make_async_copy(src_ref.at[pl.ds(src_row, rows)], dst_ref.at[pl.ds(dst_row, rows)], sem)


ZERO_ROWS = TM_EXP // 2
DMA_UNROLL = 8


def _pad_fill(e, ps_ref, pl_ref, zbuf_ref, xs_ref, zsem, wait):
    ln = pl_ref[e]
    st = ps_ref[e]
    b = 1
    while b <= ZERO_ROWS:
        @pl.when((ln & b) != 0)
        def _(b=b):
            cp = _prow_copy(zbuf_ref, 0, xs_ref, st + (ln & (b - 1)), zsem, rows=b)
            if wait:
                cp.wait()
            else:
                cp.start()
        b *= 2


def _dispatch_kernel(pos0_ref, pos1_ref, ps_ref, pl_ref, h1p_ref, xs_ref, zbuf_ref, sem, zsem):
    i = pl.program_id(0)
    tm = TM_DISP

    @pl.when(i == 0)
    def _():
        zbuf_ref[...] = jnp.zeros_like(zbuf_ref)

        def fill_start(e, _):
            _pad_fill(e, ps_ref, pl_ref, zbuf_ref, xs_ref, zsem, False)
            return 0

        lax.fori_loop(0, N_EXPERTS, fill_start, 0)

    def issue(k, _):
        for u in range(DMA_UNROLL):
            r = k * DMA_UNROLL + u
            tok = i * tm + r
            _prow_copy(h1p_ref, r, xs_ref, pos0_ref[tok], sem).start(priority=0)
            _prow_copy(h1p_ref, r, xs_ref, pos1_ref[tok], sem).start(priority=1)
        return 0

    lax.fori_loop(0, tm // DMA_UNROLL, issue, 0)

    def drain(k, _):
        for u in range(DMA_UNROLL):
            _prow_copy(h1p_ref, 0, xs_ref, 0, sem).wait()
            _prow_copy(h1p_ref, 0, xs_ref, 0, sem).wait()
        return 0

    lax.fori_loop(0, tm // DMA_UNROLL, drain, 0)

    @pl.when(i == pl.num_programs(0) - 1)
    def _():
        def fill_wait(e, _):
            _pad_fill(e, ps_ref, pl_ref, zbuf_ref, xs_ref, zsem, True)
            return 0

        lax.fori_loop(0, N_EXPERTS, fill_wait, 0)


def _dispatch(pos0, pos1, pad_start, pad_len, h1p, n_rows):
    n = h1p.shape[0]
    return pl.pallas_call(
        _dispatch_kernel,
        grid_spec=pltpu.PrefetchScalarGridSpec(
            num_scalar_prefetch=4,
            grid=(n // TM_DISP,),
            in_specs=[pl.BlockSpec((TM_DISP, RT, LANES), lambda i, *_: (i, 0, 0))],
            out_specs=pl.BlockSpec(memory_space=pl.ANY),
            scratch_shapes=[
                pltpu.VMEM((ZERO_ROWS, RT, LANES), BF16),
                pltpu.SemaphoreType.DMA(()),
                pltpu.SemaphoreType.DMA(()),
            ],
        ),
        out_shape=jax.ShapeDtypeStruct((n_rows, RT, LANES), BF16),
        compiler_params=pltpu.CompilerParams(
            dimension_semantics=("arbitrary",),
            vmem_limit_bytes=VMEM_LIMIT),
        name="dispatch",
    )(pos0, pos1, pad_start, pad_len, h1p)


def _expert_kernel(te_ref, nu_ref, nx_ref, sl_ref, xs_ref, wg_hbm, wu_hbm, wd_hbm, ys_ref,
                   wgf_ref, wuf_ref, wdf_ref, wgb_ref, wub_ref, wdb_ref, stage_ref, wsem):
    i = pl.program_id(0)
    used = i < nu_ref[0]
    e = te_ref[i]
    s = sl_ref[i]
    fresh = (i == 0) | (e != te_ref[jnp.maximum(i - 1, 0)])

    def weight_copies(expert, slot):
        return (pltpu.make_async_copy(wg_hbm.at[expert], wgf_ref.at[slot], wsem.at[slot, 0]),
                pltpu.make_async_copy(wu_hbm.at[expert], wuf_ref.at[slot], wsem.at[slot, 1]),
                pltpu.make_async_copy(wd_hbm.at[expert], wdf_ref.at[slot], wsem.at[slot, 2]))

    @pl.when(i == 0)
    def _():
        for cp in weight_copies(e, s):
            cp.start(priority=1)

    @pl.when(used & fresh)
    def _():
        for cp in weight_copies(e, s):
            cp.wait()

        @pl.when(nx_ref[i] >= 0)
        def _():
            for cp in weight_copies(nx_ref[i], 1 - s):
                cp.start(priority=1)

        wgb_ref[...] = wgf_ref[s].astype(BF16)
        wub_ref[...] = wuf_ref[s].astype(BF16)
        wdb_ref[...] = wdf_ref[s].astype(BF16)

    @pl.when(used)
    def _():
        x = jnp.concatenate([p.astype(BF16) for p in _load_rows(xs_ref, 0, TM_EXP, stage_ref)],
                            axis=1)
        g = jnp.dot(x, wgb_ref[...], preferred_element_type=F32)
        u = jnp.dot(x, wub_ref[...], preferred_element_type=F32)
        h = (g * _sigmoid(g) * u).astype(BF16)
        y = jnp.dot(h, wdb_ref[...], preferred_element_type=F32)
        _store_rows(ys_ref, 0, TM_EXP, y, stage_ref)


def _expert_ffn(tile_expert, n_used, next_expert, slot, xs, wg, wu, wd):
    n_rows = xs.shape[0]
    n_tiles = n_rows // TM_EXP

    def row_map(i, te, nu, nx, sl):
        return (jnp.minimum(i, nu[0] - 1), 0, 0)

    return pl.pallas_call(
        _expert_kernel,
        grid_spec=pltpu.PrefetchScalarGridSpec(
            num_scalar_prefetch=4,
            grid=(n_tiles,),
            in_specs=[
                pl.BlockSpec((TM_EXP, RT, LANES), row_map),
                pl.BlockSpec(memory_space=pl.ANY),
                pl.BlockSpec(memory_space=pl.ANY),
                pl.BlockSpec(memory_space=pl.ANY),
            ],
            out_specs=pl.BlockSpec((TM_EXP, RT, LANES), row_map),
            scratch_shapes=[
                pltpu.VMEM((2, D_MODEL, D_EXPERT), F32),
                pltpu.VMEM((2, D_MODEL, D_EXPERT), F32),
                pltpu.VMEM((2, D_EXPERT, D_MODEL), F32),
                pltpu.VMEM((D_MODEL, D_EXPERT), BF16),
                pltpu.VMEM((D_MODEL, D_EXPERT), BF16),
                pltpu.VMEM((D_EXPERT, D_MODEL), BF16),
                pltpu.VMEM((TM_EXP * PITCH, LANES), F32),
                pltpu.SemaphoreType.DMA((2, 3)),
            ],
        ),
        out_shape=jax.ShapeDtypeStruct((n_rows, RT, LANES), BF16),
        compiler_params=pltpu.CompilerParams(
            dimension_semantics=("arbitrary",),
            vmem_limit_bytes=VMEM_LIMIT),
        name="expert_ffn",
    )(tile_expert, n_used, next_expert, slot, xs, wg, wu, wd)


def _combine_kernel(pos0_ref, pos1_ref, h1_ref, info_ref, g2_ref, b2_ref, ys_ref, o_ref,
                    ybuf_ref, stage_ref, sem):
    i = pl.program_id(0)
    tm = TM_COMB
    par = lax.rem(i, 2)

    def issue(step, parity):
        def body(k, _):
            for u in range(DMA_UNROLL):
                r = k * DMA_UNROLL + u
                tok = step * tm + r
                _prow_copy(ys_ref, pos0_ref[tok], ybuf_ref.at[parity, 0], r,
                           sem.at[parity]).start(priority=0)
                _prow_copy(ys_ref, pos1_ref[tok], ybuf_ref.at[parity, 1], r,
                           sem.at[parity]).start(priority=1)
            return 0
        lax.fori_loop(0, tm // DMA_UNROLL, body, 0)

    def drain(parity):
        def body(k, _):
            for u in range(DMA_UNROLL):
                _prow_copy(ys_ref, 0, ybuf_ref.at[parity, 0], 0, sem.at[parity]).wait()
                _prow_copy(ys_ref, 0, ybuf_ref.at[parity, 1], 0, sem.at[parity]).wait()
            return 0
        lax.fori_loop(0, tm // DMA_UNROLL, body, 0)

    @pl.when(i == 0)
    def _():
        issue(0, 0)

    @pl.when(i + 1 < pl.num_programs(0))
    def _():
        issue(i + 1, 1 - par)

    drain(par)

    rows = 64
    for c in range(tm // rows):
        rs = slice(c * rows, (c + 1) * rows)
        q1 = info_ref[rs, 2:3]
        q2 = info_ref[rs, 3:4]
        y0 = _load_rows(ybuf_ref.at[par, 0], c * rows, rows, stage_ref)
        y1 = _load_rows(ybuf_ref.at[par, 1], c * rows, rows, stage_ref)
        ffn = jnp.concatenate([q1 * a + q2 * b for a, b in zip(y0, y1)], axis=1)
        o_ref[rs, :] = _ln_rows(DN_ALPHA * h1_ref[rs, :] + ffn, g2_ref[...], b2_ref[...])


def _combine(pos0, pos1, h1, info, g2, b2, ys):
    n = h1.shape[0]
    return pl.pallas_call(
        _combine_kernel,
        grid_spec=pltpu.PrefetchScalarGridSpec(
            num_scalar_prefetch=2,
            grid=(n // TM_COMB,),
            in_specs=[
                pl.BlockSpec((TM_COMB, D_MODEL), lambda i, *_: (i, 0)),
                pl.BlockSpec((TM_COMB, ROUTE_W), lambda i, *_: (i, 0)),
                pl.BlockSpec((1, D_MODEL), lambda i, *_: (0, 0)),
                pl.BlockSpec((1, D_MODEL), lambda i, *_: (0, 0)),
                pl.BlockSpec(memory_space=pl.ANY),
            ],
            out_specs=pl.BlockSpec((TM_COMB, D_MODEL), lambda i, *_: (i, 0)),
            scratch_shapes=[
                pltpu.VMEM((2, 2, TM_COMB, RT, LANES), BF16),
                pltpu.VMEM((64 * PITCH, LANES), F32),
                pltpu.SemaphoreType.DMA((2,)),
            ],
        ),
        out_shape=jax.ShapeDtypeStruct((n, D_MODEL), F32),
        compiler_params=pltpu.CompilerParams(
            dimension_semantics=("arbitrary",),
            vmem_limit_bytes=VMEM_LIMIT),
        name="combine",
    )(pos0, pos1, h1, info, g2, b2, ys)


def _block_diag(w, per):
    h, hd, _ = w.shape
    wg = w.reshape(h // per, per, hd, hd)
    eye = jnp.eye(per, dtype=w.dtype)
    return jnp.einsum("gpij,pq->gpiqj", wg, eye).reshape(h // per, per * hd, per * hd)


def kernel(x, ln_in_g, ln_in_b, w_in, lru_conv_w, lru_conv_b, lru_w_a, lru_b_a, lru_w_x, lru_b_x,
           lru_lambda, conf_conv_w, conf_conv_b, conf_ln_g, conf_ln_b, w_out, ln1_g, ln1_b,
           router_group_w, router_group_b, router_expert_w, router_expert_b, exp_w_gate, exp_w_up,
           exp_w_down, ln2_g, ln2_b):
    bsz, seq, d = x.shape
    n = bsz * seq
    x2 = x.reshape(n, d)
    row = lambda v: v.reshape(1, -1).astype(F32)
    l = 0

    z, h0 = _ln_win(x2, row(ln_in_g), row(ln_in_b), w_in[l])

    per = CB_LRU // LRU_HEAD_DIM
    wcat = jnp.concatenate([_block_diag(lru_w_a[l], per), _block_diag(lru_w_x[l], per)],
                           axis=-1).astype(BF16)
    a_out = _lru_mixer(z, lru_conv_w[l], row(lru_conv_b[l]), wcat, row(lru_b_a[l]),
                       row(lru_b_x[l]), row(lru_lambda[l]), bsz, seq)

    nlb = D_CONV // LANES
    w3 = jnp.pad(conf_conv_w[l], ((0, 32 - CONF_CONV_W), (0, 0)))
    w3 = w3.reshape(32, nlb, LANES).transpose(1, 0, 2)
    cb3 = conf_conv_b[l].reshape(nlb, 1, LANES)
    b_out, wo = _conf_mixer(z, w3, cb3, row(conf_ln_g[l]), row(conf_ln_b[l]), w_out[l], bsz, seq)

    wr = jnp.concatenate([router_group_w[l], router_expert_w[l]], axis=1)
    wr = jnp.pad(wr, ((0, 0), (0, ROUTE_W - wr.shape[1])))
    wr_hi = wr.astype(BF16)
    wr_lo = (wr - wr_hi.astype(F32)).astype(BF16)
    wr_cat = jnp.concatenate([wr_hi, wr_lo], axis=1)
    br = jnp.concatenate([router_group_b[l], router_expert_b[l]])
    br = jnp.pad(br, (0, ROUTE_W - br.shape[0])).reshape(1, ROUTE_W)
    h1, h1p, logits = _wout_router(a_out, b_out, h0, wo, row(ln1_g[l]), row(ln1_b[l]), wr_cat, br)

    info, counts = _route(logits)
    idx = info[:, 0:6].astype(jnp.int32)
    r0, r1, e0, e1 = idx[:, 0], idx[:, 1], idx[:, 4], idx[:, 5]

    n_tiles = (n * 2) // TM_EXP + N_EXPERTS
    tile_expert, n_used, next_expert, slot, starts, pad_start, pad_len = _plan(
        counts[0].astype(jnp.int32), n_tiles)

    expert_ids = jnp.arange(N_EXPERTS, dtype=jnp.int32)[None, :]
    start_of = lambda e: jnp.sum(jnp.where(e[:, None] == expert_ids, starts[None, :], 0), axis=1)
    pos0 = (start_of(e0) + r0).astype(jnp.int32)
    pos1 = (start_of(e1) + r1).astype(jnp.int32)

    xs = _dispatch(pos0, pos1, pad_start, pad_len, h1p, n_tiles * TM_EXP)
    shp = (N_EXPERTS, D_MODEL, D_EXPERT)
    ys = _expert_ffn(tile_expert, n_used, next_expert, slot, xs, exp_w_gate[l].reshape(shp),
                     exp_w_up[l].reshape(shp), exp_w_down[l].reshape(N_EXPERTS, D_EXPERT, D_MODEL))
    out = _combine(pos0, pos1, h1, info, row(ln2_g[l]), row(ln2_b[l]), ys)
    return out.reshape(bsz, seq, d)
```

```python
import functools
import math

import jax
import jax.numpy as jnp
from jax import lax
from jax.experimental import pallas as pl
from jax.experimental.pallas import tpu as pltpu

F32 = jnp.float32
BF16 = jnp.bfloat16

D_MODEL = 2048
D_LRU = 1024
D_CONV = 1024
LRU_HEADS = 16
LRU_HEAD_DIM = 64
LRU_C = 8.0
LRU_CONV_W = 4
CONF_CONV_W = 31
N_GROUPS = 4
EXPERTS_PER_GROUP = 8
N_EXPERTS = N_GROUPS * EXPERTS_PER_GROUP
D_EXPERT = 512
LN_EPS = 1e-5
DEPTH = 1
DN_ALPHA = (2 * DEPTH) ** 0.25

LANES = 128
SUBLANES = 8
VMEM_LIMIT = 56 * 1024 * 1024

TM_WIN = 512
TN_WIN = 1024
W_CHUNK = 256
TT_LRU = 2048
CB_LRU = 256
TT_CONF = 256
CONF_HALO = 32
TM_OUT = 512
TM_ROUTE = 512
TM_DISP = 1024
TM_EXP = 256
TM_COMB = 256
ROUTE_W = 128


def _sigmoid(x):
    return 0.5 * (jnp.tanh(0.5 * x) + 1.0)


def _ln_rows(x, g, b):
    mu = jnp.mean(x, axis=-1, keepdims=True)
    xc = x - mu
    var = jnp.mean(xc * xc, axis=-1, keepdims=True)
    return xc * lax.rsqrt(var + LN_EPS) * g + b


def _stage_weight(w_hbm, wb_ref, wst_ref, wsem):
    nchunk = wb_ref.shape[1] // W_CHUNK

    def chunk_copy(c):
        return pltpu.make_async_copy(w_hbm.at[:, pl.ds(c * W_CHUNK, W_CHUNK)],
                                     wst_ref.at[c % 2], wsem.at[c % 2])

    chunk_copy(0).start()
    for c in range(nchunk):
        if c + 1 < nchunk:
            chunk_copy(c + 1).start()
        chunk_copy(c).wait()
        wb_ref[:, c * W_CHUNK:(c + 1) * W_CHUNK] = wst_ref[c % 2].astype(BF16)


def _ln_win_kernel(x_ref, g_ref, b_ref, w_hbm, z_ref, h_ref, wb_ref, wst_ref, xn_ref, wsem):
    s = pl.program_id(0)
    par = lax.rem(s, 2)

    @pl.when(s == 0)
    def _():
        xn_ref[1] = jnp.zeros(xn_ref.shape[1:], BF16)
        _stage_weight(w_hbm, wb_ref, wst_ref, wsem)

    rows = 128
    for c in range(TM_WIN // rows):
        rs = slice(c * rows, (c + 1) * rows)
        hn = _ln_rows(x_ref[rs, :], g_ref[...], b_ref[...])
        h_ref[rs, :] = hn
        xn_ref[par, rs, :] = hn.astype(BF16)

    xprev = xn_ref[1 - par]
    for c in range(z_ref.shape[1] // TN_WIN):
        cs = slice(c * TN_WIN, (c + 1) * TN_WIN)
        z_ref[:, cs] = jnp.dot(xprev, wb_ref[:, cs], preferred_element_type=F32).astype(z_ref.dtype)


def _ln_win(x2, g, b, w):
    n = x2.shape[0]
    ncol = w.shape[1]
    nt = n // TM_WIN
    return pl.pallas_call(
        _ln_win_kernel,
        grid=(nt + 1,),
        in_specs=[
            pl.BlockSpec((TM_WIN, D_MODEL), lambda s: (jnp.minimum(s, nt - 1), 0)),
            pl.BlockSpec((1, D_MODEL), lambda s: (0, 0)),
            pl.BlockSpec((1, D_MODEL), lambda s: (0, 0)),
            pl.BlockSpec(memory_space=pl.ANY),
        ],
        out_specs=[
            pl.BlockSpec((TM_WIN, ncol), lambda s: (jnp.maximum(s - 1, 0), 0)),
            pl.BlockSpec((TM_WIN, D_MODEL), lambda s: (jnp.minimum(s, nt - 1), 0)),
        ],
        out_shape=[
            jax.ShapeDtypeStruct((n, ncol), BF16),
            jax.ShapeDtypeStruct((n, D_MODEL), F32),
        ],
        scratch_shapes=[
            pltpu.VMEM((D_MODEL, ncol), BF16),
            pltpu.VMEM((2, D_MODEL, W_CHUNK), F32),
            pltpu.VMEM((2, TM_WIN, D_MODEL), BF16),
            pltpu.SemaphoreType.DMA((2,)),
        ],
        compiler_params=pltpu.CompilerParams(
            dimension_semantics=("arbitrary",),
            vmem_limit_bytes=VMEM_LIMIT),
        name="ln_win",
    )(x2, g, b, w)


def _lru_kernel(zx_ref, zg_ref, cw_ref, cb_ref, wcat_ref, ba_ref, bx_ref, lam_ref,
                o_ref, xs_ref, hp_ref, a_ref, g_ref):
    t = pl.program_id(2)
    tt = TT_LRU

    @pl.when(t == 0)
    def _():
        xs_ref[0:SUBLANES, :] = jnp.zeros((SUBLANES, CB_LRU), F32)
        hp_ref[...] = jnp.zeros_like(hp_ref)

    @pl.when(t > 0)
    def _():
        xs_ref[0:SUBLANES, :] = xs_ref[tt:tt + SUBLANES, :]

    xs_ref[SUBLANES:SUBLANES + tt, :] = zx_ref[...].astype(F32)

    rows = 128
    for rb in range(tt // rows):
        acc = jnp.broadcast_to(cb_ref[...], (rows, CB_LRU))
        for k in range(LRU_CONV_W):
            off = rb * rows + SUBLANES - (LRU_CONV_W - 1) + k
            acc = acc + cw_ref[k:k + 1, :] * xs_ref[off:off + rows, :]
        a_ref[rb * rows:(rb + 1) * rows, :] = acc

    g_ref[...] = jnp.dot(a_ref[...].astype(BF16), wcat_ref[0], preferred_element_type=F32)

    lam = lam_ref[...]
    softplus_neg = jnp.maximum(-lam, 0.0) + jnp.log1p(jnp.exp(-jnp.abs(lam)))
    cvec = -LRU_C * softplus_neg
    ba = ba_ref[...]
    bx = bx_ref[...]
    blk = 64
    row_in_vreg = lax.broadcasted_iota(jnp.int32, (blk, CB_LRU), 0) & (SUBLANES - 1)

    def body(rb, h):
        rs = pl.ds(pl.multiple_of(rb * blk, blk), blk)
        a_in = a_ref[rs, :]
        r = _sigmoid(g_ref[rs, 0:CB_LRU] + ba)
        i = _sigmoid(g_ref[rs, CB_LRU:2 * CB_LRU] + bx)
        log_a = cvec * r
        a = jnp.exp(log_a)
        u = jnp.sqrt(-jnp.tanh(log_a) * (a * a + 1.0)) * (i * a_in)
        for s in (1, 2, 4):
            m = row_in_vreg >= s
            a_sh = jnp.where(m, pltpu.roll(a, s, 0), 1.0)
            u_sh = jnp.where(m, pltpu.roll(u, s, 0), 0.0)
            u = u + a * u_sh
            a = a * a_sh
        outs = []
        for gi in range(blk // SUBLANES):
            ag = a[gi * SUBLANES:(gi + 1) * SUBLANES, :]
            ug = u[gi * SUBLANES:(gi + 1) * SUBLANES, :]
            hg = ug + ag * h
            h = hg[SUBLANES - 1:SUBLANES, :]
            outs.append(hg)
        hblk = jnp.concatenate(outs, axis=0)
        gl = zg_ref[rs, :].astype(F32)
        gelu = 0.5 * gl * (1.0 + jnp.tanh(0.7978845608028654 * (gl + 0.044715 * gl * gl * gl)))
        o_ref[rs, :] = (gelu * hblk).astype(o_ref.dtype)
        return h

    h = lax.fori_loop(0, tt // blk, body, hp_ref[0:1, :])
    hp_ref[...] = jnp.broadcast_to(h, hp_ref.shape)


def _lru_mixer(z, cw, cb, wcat, ba, bx, lam, bsz, seq):
    n = z.shape[0]
    nt = seq // TT_LRU
    ncb = D_LRU // CB_LRU
    row = lambda b, j, t: b * nt + t
    vec = pl.BlockSpec((1, CB_LRU), lambda b, j, t: (0, j))
    return pl.pallas_call(
        _lru_kernel,
        grid=(bsz, ncb, nt),
        in_specs=[
            pl.BlockSpec((TT_LRU, CB_LRU), lambda b, j, t: (row(b, j, t), j)),
            pl.BlockSpec((TT_LRU, CB_LRU), lambda b, j, t: (row(b, j, t), ncb + j)),
            pl.BlockSpec((LRU_CONV_W, CB_LRU), lambda b, j, t: (0, j)),
            vec,
            pl.BlockSpec((1, CB_LRU, 2 * CB_LRU), lambda b, j, t: (j, 0, 0)),
            vec, vec, vec,
        ],
        out_specs=pl.BlockSpec((TT_LRU, CB_LRU), lambda b, j, t: (row(b, j, t), j)),
        out_shape=jax.ShapeDtypeStruct((n, D_LRU), BF16),
        scratch_shapes=[
            pltpu.VMEM((TT_LRU + SUBLANES, CB_LRU), F32),
            pltpu.VMEM((SUBLANES, CB_LRU), F32),
            pltpu.VMEM((TT_LRU, CB_LRU), F32),
            pltpu.VMEM((TT_LRU, 2 * CB_LRU), F32),
        ],
        compiler_params=pltpu.CompilerParams(
            dimension_semantics=("arbitrary", "arbitrary", "arbitrary"),
            vmem_limit_bytes=VMEM_LIMIT),
        name="lru_mixer",
    )(z, z, cw, cb, wcat, ba, bx, lam)


def _conf_kernel(zv_ref, zg_ref, w_ref, cb_ref, lg_ref, lb_ref, wo_ref, o_ref, wob_ref,
                 cs_ref, cv_ref):
    t = pl.program_id(1)
    tt = TT_CONF
    nlb = D_CONV // LANES
    wob_ref[...] = wo_ref[...].astype(BF16)

    @pl.when(t == 0)
    def _():
        cs_ref[:, 0:CONF_HALO, :] = jnp.zeros((nlb, CONF_HALO, LANES), F32)

    @pl.when(t > 0)
    def _():
        cs_ref[:, 0:CONF_HALO, :] = cs_ref[:, tt:tt + CONF_HALO, :]

    for c in range(nlb):
        ls = slice(c * LANES, (c + 1) * LANES)
        v = zv_ref[:, ls].astype(F32)
        g = zg_ref[:, ls].astype(F32)
        cs_ref[c, CONF_HALO:CONF_HALO + tt, :] = v * _sigmoid(g)

    rows = 64
    nrb = tt // rows
    base = CONF_HALO - (CONF_CONV_W - 1)

    def conv_body(c, carry):
        accs = [jnp.broadcast_to(cb_ref[c], (rows, LANES)) for _ in range(nrb)]
        for k in range(CONF_CONV_W):
            wk = w_ref[c, k:k + 1, :]
            for rb in range(nrb):
                off = rb * rows + base + k
                accs[rb] = accs[rb] + wk * cs_ref[c, off:off + rows, :]
        for rb in range(nrb):
            cv_ref[c, rb * rows:(rb + 1) * rows, :] = accs[rb]
        return carry

    lax.fori_loop(0, nlb, conv_body, 0)

    ln_rows = 32
    inv_n = 1.0 / D_CONV
    for rb in range(tt // ln_rows):
        rs = slice(rb * ln_rows, (rb + 1) * ln_rows)
        blk = cv_ref[:, rs, :]
        mu = jnp.sum(jnp.sum(blk, axis=0), axis=-1, keepdims=True) * inv_n
        d = blk - mu[None]
        var = jnp.sum(jnp.sum(d * d, axis=0), axis=-1, keepdims=True) * inv_n
        inv = lax.rsqrt(var + LN_EPS)
        for c in range(nlb):
            ls = slice(c * LANES, (c + 1) * LANES)
            y = d[c] * inv * lg_ref[:, ls] + lb_ref[:, ls]
            o_ref[rs, ls] = (y * _sigmoid(y)).astype(o_ref.dtype)


def _conf_mixer(z, w3, cb3, lg, lb, wo, bsz, seq):
    n = z.shape[0]
    nt = seq // TT_CONF
    nlb = D_CONV // LANES
    wo_rows = wo.shape[0] // (bsz * nt)
    assert wo_rows * bsz * nt == wo.shape[0] and wo_rows % 16 == 0
    return pl.pallas_call(
        _conf_kernel,
        grid=(bsz, nt),
        in_specs=[
            pl.BlockSpec((TT_CONF, D_CONV), lambda b, t: (b * nt + t, 2)),
            pl.BlockSpec((TT_CONF, D_CONV), lambda b, t: (b * nt + t, 3)),
            pl.BlockSpec((nlb, 32, LANES), lambda b, t: (0, 0, 0)),
            pl.BlockSpec((nlb, 1, LANES), lambda b, t: (0, 0, 0)),
            pl.BlockSpec((1, D_CONV), lambda b, t: (0, 0)),
            pl.BlockSpec((1, D_CONV), lambda b, t: (0, 0)),
            pl.BlockSpec((wo_rows, D_MODEL), lambda b, t: (b * nt + t, 0)),
        ],
        out_specs=[
            pl.BlockSpec((TT_CONF, D_CONV), lambda b, t: (b * nt + t, 0)),
            pl.BlockSpec((wo_rows, D_MODEL), lambda b, t: (b * nt + t, 0)),
        ],
        out_shape=[
            jax.ShapeDtypeStruct((n, D_CONV), BF16),
            jax.ShapeDtypeStruct(wo.shape, BF16),
        ],
        scratch_shapes=[
            pltpu.VMEM((nlb, CONF_HALO + TT_CONF, LANES), F32),
            pltpu.VMEM((nlb, TT_CONF, LANES), F32),
        ],
        compiler_params=pltpu.CompilerParams(
            dimension_semantics=("arbitrary", "arbitrary"),
            vmem_limit_bytes=VMEM_LIMIT),
        name="conf_mixer",
    )(z, z, w3, cb3, lg, lb, wo)


def _split_bf16(v):
    hi = v.astype(BF16)
    lo = (v - hi.astype(F32)).astype(BF16)
    return hi, lo


RT = D_MODEL // LANES
PITCH = RT + SUBLANES


def _store_rows(dst_ref, row0, rows, v, stage_ref):
    for s in range(RT):
        stage_ref[pl.ds(s, rows, stride=PITCH), :] = v[:, s * LANES:(s + 1) * LANES]
    staged = stage_ref[0:rows * PITCH, :].reshape(rows, PITCH, LANES)
    dst_ref[row0:row0 + rows] = staged[:, 0:RT, :].astype(BF16)


def _load_rows(src_ref, row0, rows, stage_ref):
    tile = src_ref[row0:row0 + rows].astype(F32)
    tile = jnp.concatenate([tile, jnp.zeros((rows, PITCH - RT, LANES), F32)], axis=1)
    stage_ref[0:rows * PITCH, :] = tile.reshape(rows * PITCH, LANES)
    return [stage_ref[pl.ds(s, rows, stride=PITCH), :] for s in range(RT)]


def _wout_kernel(a_ref, b_ref, h_ref, wa_ref, wb_ref, g1_ref, b1_ref,
                 wr_ref, br_ref, h1_ref, h1r_ref, lg_ref, mixa_ref, mixb_ref, hl_ref, stage_ref):
    half = TM_OUT // 2
    for mix_ref, r0 in ((mixa_ref, 0), (mixb_ref, half)):
        hs = slice(r0, r0 + half)
        mix_ref[...] = (jnp.dot(a_ref[hs, :], wa_ref[...], preferred_element_type=F32)
                        + jnp.dot(b_ref[hs, :], wb_ref[...], preferred_element_type=F32))
    rows = 64
    for c in range(TM_OUT // rows):
        rs = slice(c * rows, (c + 1) * rows)
        mix_ref, r0 = (mixa_ref, 0) if c * rows < half else (mixb_ref, half)
        mix = mix_ref[c * rows - r0:(c + 1) * rows - r0, :]
        h1 = _ln_rows(DN_ALPHA * h_ref[rs, :] + mix, g1_ref[...], b1_ref[...])
        h1_ref[rs, :] = h1
        _store_rows(h1r_ref, c * rows, rows, h1, stage_ref)
        hi, lo = _split_bf16(h1)
        hl_ref[rs, :] = hi
        hl_ref[TM_OUT + c * rows:TM_OUT + (c + 1) * rows, :] = lo
    p = jnp.dot(hl_ref[...], wr_ref[...], preferred_element_type=F32)
    lg_ref[...] = (p[0:TM_OUT, 0:ROUTE_W] + p[0:TM_OUT, ROUTE_W:2 * ROUTE_W]
                   + p[TM_OUT:2 * TM_OUT, 0:ROUTE_W] + br_ref[...])


def _wout_router(a, b, h, wo, g1, b1, wr_cat, br):
    n = h.shape[0]
    assert D_LRU == D_CONV
    full = lambda shape: pl.BlockSpec(shape, lambda i: tuple(0 for _ in shape))
    return pl.pallas_call(
        _wout_kernel,
        grid=(n // TM_OUT,),
        in_specs=[
            pl.BlockSpec((TM_OUT, D_LRU), lambda i: (i, 0)),
            pl.BlockSpec((TM_OUT, D_CONV), lambda i: (i, 0)),
            pl.BlockSpec((TM_OUT, D_MODEL), lambda i: (i, 0)),
            pl.BlockSpec((D_LRU, D_MODEL), lambda i: (0, 0)),
            pl.BlockSpec((D_CONV, D_MODEL), lambda i: (1, 0)),
            full((1, D_MODEL)), full((1, D_MODEL)),
            full((D_MODEL, 2 * ROUTE_W)), full((1, ROUTE_W)),
        ],
        out_specs=[
            pl.BlockSpec((TM_OUT, D_MODEL), lambda i: (i, 0)),
            pl.BlockSpec((TM_OUT, RT, LANES), lambda i: (i, 0, 0)),
            pl.BlockSpec((TM_OUT, ROUTE_W), lambda i: (i, 0)),
        ],
        out_shape=[
            jax.ShapeDtypeStruct((n, D_MODEL), F32),
            jax.ShapeDtypeStruct((n, RT, LANES), BF16),
            jax.ShapeDtypeStruct((n, ROUTE_W), F32),
        ],
        scratch_shapes=[
            pltpu.VMEM((TM_OUT // 2, D_MODEL), F32),
            pltpu.VMEM((TM_OUT // 2, D_MODEL), F32),
            pltpu.VMEM((2 * TM_OUT, D_MODEL), BF16),
            pltpu.VMEM((64 * PITCH, LANES), F32),
        ],
        compiler_params=pltpu.CompilerParams(
            dimension_semantics=("arbitrary",),
            vmem_limit_bytes=VMEM_LIMIT),
        name="wout_router",
    )(a, b, h, wo, wo, g1, b1, wr_cat, br)


def _route_kernel(lg_ref, info_ref, cnt_ref, run_ref, tri_ref):
    t = pl.program_id(0)
    tm = TM_ROUTE
    l = lg_ref[...]
    lane = lax.broadcasted_iota(jnp.int32, (tm, ROUTE_W), 1)
    neg = jnp.float32(-jnp.inf)
    big = jnp.int32(1 << 20)

    gmask = lane < N_GROUPS
    gmax = jnp.max(jnp.where(gmask, l, neg), axis=-1, keepdims=True)
    gsel = jnp.min(jnp.where(gmask & (l == gmax), lane, big), axis=-1, keepdims=True)
    gsum = jnp.sum(jnp.where(gmask, jnp.exp(l - gmax), 0.0), axis=-1, keepdims=True)
    pg_top = 1.0 / gsum

    lo = N_GROUPS + EXPERTS_PER_GROUP * gsel
    emask = (lane >= lo) & (lane < lo + EXPERTS_PER_GROUP)
    v1 = jnp.max(jnp.where(emask, l, neg), axis=-1, keepdims=True)
    i1 = jnp.min(jnp.where(emask & (l == v1), lane, big), axis=-1, keepdims=True)
    emask2 = emask & (lane != i1)
    v2 = jnp.max(jnp.where(emask2, l, neg), axis=-1, keepdims=True)
    i2 = jnp.min(jnp.where(emask2 & (l == v2), lane, big), axis=-1, keepdims=True)
    e21 = jnp.exp(v2 - v1)
    q1 = pg_top / (1.0 + e21)
    q2 = pg_top * e21 / (1.0 + e21)

    oh1 = (lane == i1).astype(F32)
    oh2 = (lane == i2).astype(F32)
    ohs = oh1 + oh2

    @pl.when(t == 0)
    def _():
        run_ref[...] = jnp.zeros_like(run_ref)
        r_i = lax.broadcasted_iota(jnp.int32, (tm, tm), 0)
        c_i = lax.broadcasted_iota(jnp.int32, (tm, tm), 1)
        tri_ref[...] = (c_i < r_i).astype(BF16)

    cum = jnp.dot(tri_ref[...], ohs.astype(BF16), preferred_element_type=F32)
    basev = run_ref[0:1, :] + cum
    r1 = jnp.sum(oh1 * basev, axis=-1, keepdims=True)
    r2 = jnp.sum(oh2 * basev, axis=-1, keepdims=True)
    run_ref[...] = run_ref[...] + jnp.sum(ohs, axis=0, keepdims=True)
    info = jnp.where(lane == 0, r1, 0.0)
    info = jnp.where(lane == 1, r2, info)
    info = jnp.where(lane == 2, q1, info)
    info = jnp.where(lane == 3, q2, info)
    info = jnp.where(lane == 4, (i1 - N_GROUPS).astype(F32), info)
    info = jnp.where(lane == 5, (i2 - N_GROUPS).astype(F32), info)
    info_ref[...] = info
    cnt_ref[...] = run_ref[...]


def _route(logits):
    n = logits.shape[0]
    nt = n // TM_ROUTE
    return pl.pallas_call(
        _route_kernel,
        grid=(nt,),
        in_specs=[pl.BlockSpec((TM_ROUTE, ROUTE_W), lambda t: (t, 0))],
        out_specs=[
            pl.BlockSpec((TM_ROUTE, ROUTE_W), lambda t: (t, 0)),
            pl.BlockSpec((SUBLANES, ROUTE_W), lambda t: (0, 0)),
        ],
        out_shape=[
            jax.ShapeDtypeStruct((n, ROUTE_W), F32),
            jax.ShapeDtypeStruct((SUBLANES, ROUTE_W), F32),
        ],
        scratch_shapes=[
            pltpu.VMEM((SUBLANES, ROUTE_W), F32),
            pltpu.VMEM((TM_ROUTE, TM_ROUTE), BF16),
        ],
        compiler_params=pltpu.CompilerParams(
            dimension_semantics=("arbitrary",),
            vmem_limit_bytes=VMEM_LIMIT),
        name="route",
    )(logits)


def _plan_kernel(cnt_ref, te_ref, nu_ref, nx_ref, sl_ref, st_ref, ps_ref, pl_ref):
    n_tiles = te_ref.shape[0]
    shift = TM_EXP.bit_length() - 1

    def forward(e, carry):
        tile, parity = carry
        c = cnt_ref[N_GROUPS + e]
        tp = (c + (TM_EXP - 1)) >> shift
        st_ref[e] = tile * TM_EXP
        ps_ref[e] = tile * TM_EXP + c
        pl_ref[e] = tp * TM_EXP - c

        def mark(k, _):
            te_ref[tile + k] = e
            sl_ref[tile + k] = parity
            return 0

        lax.fori_loop(0, tp, mark, 0)
        return tile + tp, jnp.where(tp > 0, 1 - parity, parity)

    used, _ = lax.fori_loop(0, N_EXPERTS, forward, (jnp.int32(0), jnp.int32(0)))
    nu_ref[0] = used

    def backward(j, nxt):
        e = N_EXPERTS - 1 - j
        c = cnt_ref[N_GROUPS + e]
        tp = (c + (TM_EXP - 1)) >> shift
        first = st_ref[e] >> shift

        def mark(k, _):
            nx_ref[first + k] = nxt
            return 0

        lax.fori_loop(0, tp, mark, 0)
        return jnp.where(tp > 0, e, nxt)

    lax.fori_loop(0, N_EXPERTS, backward, jnp.int32(-1))

    last_e = te_ref[jnp.maximum(used - 1, 0)]
    last_s = sl_ref[jnp.maximum(used - 1, 0)]

    def tail(i, _):
        te_ref[i] = last_e
        sl_ref[i] = last_s
        nx_ref[i] = -1
        return 0

    lax.fori_loop(used, n_tiles, tail, 0)


def _plan(counts_i32, n_tiles):
    smem = lambda: pl.BlockSpec(memory_space=pltpu.SMEM)
    i32 = lambda k: jax.ShapeDtypeStruct((k,), jnp.int32)
    return pl.pallas_call(
        _plan_kernel,
        in_specs=[smem()],
        out_specs=[smem() for _ in range(7)],
        out_shape=[i32(n_tiles), i32(1), i32(n_tiles), i32(n_tiles),
                   i32(N_EXPERTS), i32(N_EXPERTS), i32(N_EXPERTS)],
        name="plan",
    )(counts_i32)


def _prow_copy(src_ref, src_row, dst_ref, dst_row, sem, rows=1):
    return pltpu.make_async_copy(src_ref.at[pl.ds(src_row, rows)], dst_ref.at[pl.ds(dst_row, rows)], sem)


ZERO_ROWS = TM_EXP // 2
DMA_UNROLL = 8


def _pad_fill(e, ps_ref, pl_ref, zbuf_ref, xs_ref, zsem, wait):
    ln = pl_ref[e]
    st = ps_ref[e]
    b = 1
    while b <= ZERO_ROWS:
        @pl.when((ln & b) != 0)
        def _(b=b):
            cp = _prow_copy(zbuf_ref, 0, xs_ref, st + (ln & (b - 1)), zsem, rows=b)
            if wait:
                cp.wait()
            else:
                cp.start()
        b *= 2


def _dispatch_kernel(pos0_ref, pos1_ref, ps_ref, pl_ref, h1p_ref, xs_ref, zbuf_ref, sem, zsem):
    i = pl.program_id(0)
    tm = TM_DISP

    @pl.when(i == 0)
    def _():
        zbuf_ref[...] = jnp.zeros_like(zbuf_ref)

        def fill_start(e, _):
            _pad_fill(e, ps_ref, pl_ref, zbuf_ref, xs_ref, zsem, False)
            return 0

        lax.fori_loop(0, N_EXPERTS, fill_start, 0)

    def issue(k, _):
        for u in range(DMA_UNROLL):
            r = k * DMA_UNROLL + u
            tok = i * tm + r
            _prow_copy(h1p_ref, r, xs_ref, pos0_ref[tok], sem).start(priority=0)
            _prow_copy(h1p_ref, r, xs_ref, pos1_ref[tok], sem).start(priority=1)
        return 0

    lax.fori_loop(0, tm // DMA_UNROLL, issue, 0)

    def drain(k, _):
        for u in range(DMA_UNROLL):
            _prow_copy(h1p_ref, 0, xs_ref, 0, sem).wait()
            _prow_copy(h1p_ref, 0, xs_ref, 0, sem).wait()
        return 0

    lax.fori_loop(0, tm // DMA_UNROLL, drain, 0)

    @pl.when(i == pl.num_programs(0) - 1)
    def _():
        def fill_wait(e, _):
            _pad_fill(e, ps_ref, pl_ref, zbuf_ref, xs_ref, zsem, True)
            return 0

        lax.fori_loop(0, N_EXPERTS, fill_wait, 0)


def _dispatch(pos0, pos1, pad_start, pad_len, h1p, n_rows):
    n = h1p.shape[0]
    return pl.pallas_call(
        _dispatch_kernel,
        grid_spec=pltpu.PrefetchScalarGridSpec(
            num_scalar_prefetch=4,
            grid=(n // TM_DISP,),
            in_specs=[pl.BlockSpec((TM_DISP, RT, LANES), lambda i, *_: (i, 0, 0))],
            out_specs=pl.BlockSpec(memory_space=pl.ANY),
            scratch_shapes=[
                pltpu.VMEM((ZERO_ROWS, RT, LANES), BF16),
                pltpu.SemaphoreType.DMA(()),
                pltpu.SemaphoreType.DMA(()),
            ],
        ),
        out_shape=jax.ShapeDtypeStruct((n_rows, RT, LANES), BF16),
        compiler_params=pltpu.CompilerParams(
            dimension_semantics=("arbitrary",),
            vmem_limit_bytes=VMEM_LIMIT),
        name="dispatch",
    )(pos0, pos1, pad_start, pad_len, h1p)


def _expert_kernel(te_ref, nu_ref, nx_ref, sl_ref, xs_ref, wg_hbm, wu_hbm, wd_hbm, ys_ref,
                   wgf_ref, wuf_ref, wdf_ref, wgb_ref, wub_ref, wdb_ref, stage_ref, wsem):
    i = pl.program_id(0)
    used = i < nu_ref[0]
    e = te_ref[i]
    s = sl_ref[i]
    fresh = (i == 0) | (e != te_ref[jnp.maximum(i - 1, 0)])

    def weight_copies(expert, slot):
        return (pltpu.make_async_copy(wg_hbm.at[expert], wgf_ref.at[slot], wsem.at[slot, 0]),
                pltpu.make_async_copy(wu_hbm.at[expert], wuf_ref.at[slot], wsem.at[slot, 1]),
                pltpu.make_async_copy(wd_hbm.at[expert], wdf_ref.at[slot], wsem.at[slot, 2]))

    @pl.when(i == 0)
    def _():
        for cp in weight_copies(e, s):
            cp.start(priority=1)

    @pl.when(used & fresh)
    def _():
        for cp in weight_copies(e, s):
            cp.wait()

        @pl.when(nx_ref[i] >= 0)
        def _():
            for cp in weight_copies(nx_ref[i], 1 - s):
                cp.start(priority=1)

        wgb_ref[...] = wgf_ref[s].astype(BF16)
        wub_ref[...] = wuf_ref[s].astype(BF16)
        wdb_ref[...] = wdf_ref[s].astype(BF16)

    @pl.when(used)
    def _():
        x = jnp.concatenate([p.astype(BF16) for p in _load_rows(xs_ref, 0, TM_EXP, stage_ref)],
                            axis=1)
        g = jnp.dot(x, wgb_ref[...], preferred_element_type=F32)
        u = jnp.dot(x, wub_ref[...], preferred_element_type=F32)
        h = (g * _sigmoid(g) * u).astype(BF16)
        y = jnp.dot(h, wdb_ref[...], preferred_element_type=F32)
        _store_rows(ys_ref, 0, TM_EXP, y, stage_ref)


def _expert_ffn(tile_expert, n_used, next_expert, slot, xs, wg, wu, wd):
    n_rows = xs.shape[0]
    n_tiles = n_rows // TM_EXP

    def row_map(i, te, nu, nx, sl):
        return (jnp.minimum(i, nu[0] - 1), 0, 0)

    return pl.pallas_call(
        _expert_kernel,
        grid_spec=pltpu.PrefetchScalarGridSpec(
            num_scalar_prefetch=4,
            grid=(n_tiles,),
            in_specs=[
                pl.BlockSpec((TM_EXP, RT, LANES), row_map),
                pl.BlockSpec(memory_space=pl.ANY),
                pl.BlockSpec(memory_space=pl.ANY),
                pl.BlockSpec(memory_space=pl.ANY),
            ],
            out_specs=pl.BlockSpec((TM_EXP, RT, LANES), row_map),
            scratch_shapes=[
                pltpu.VMEM((2, D_MODEL, D_EXPERT), F32),
                pltpu.VMEM((2, D_MODEL, D_EXPERT), F32),
                pltpu.VMEM((2, D_EXPERT, D_MODEL), F32),
                pltpu.VMEM((D_MODEL, D_EXPERT), BF16),
                pltpu.VMEM((D_MODEL, D_EXPERT), BF16),
                pltpu.VMEM((D_EXPERT, D_MODEL), BF16),
                pltpu.VMEM((TM_EXP * PITCH, LANES), F32),
                pltpu.SemaphoreType.DMA((2, 3)),
            ],
        ),
        out_shape=jax.ShapeDtypeStruct((n_rows, RT, LANES), BF16),
        compiler_params=pltpu.CompilerParams(
            dimension_semantics=("arbitrary",),
            vmem_limit_bytes=VMEM_LIMIT),
        name="expert_ffn",
    )(tile_expert, n_used, next_expert, slot, xs, wg, wu, wd)


def _combine_kernel(pos0_ref, pos1_ref, h1_ref, info_ref, g2_ref, b2_ref, ys_ref, o_ref,
                    ybuf_ref, stage_ref, sem):
    i = pl.program_id(0)
    tm = TM_COMB
    par = lax.rem(i, 2)

    def issue(step, parity):
        def body(k, _):
            for u in range(DMA_UNROLL):
                r = k * DMA_UNROLL + u
                tok = step * tm + r
                _prow_copy(ys_ref, pos0_ref[tok], ybuf_ref.at[parity, 0], r,
                           sem.at[parity]).start(priority=0)
                _prow_copy(ys_ref, pos1_ref[tok], ybuf_ref.at[parity, 1], r,
                           sem.at[parity]).start(priority=1)
            return 0
        lax.fori_loop(0, tm // DMA_UNROLL, body, 0)

    def drain(parity):
        def body(k, _):
            for u in range(DMA_UNROLL):
                _prow_copy(ys_ref, 0, ybuf_ref.at[parity, 0], 0, sem.at[parity]).wait()
                _prow_copy(ys_ref, 0, ybuf_ref.at[parity, 1], 0, sem.at[parity]).wait()
            return 0
        lax.fori_loop(0, tm // DMA_UNROLL, body, 0)

    @pl.when(i == 0)
    def _():
        issue(0, 0)

    @pl.when(i + 1 < pl.num_programs(0))
    def _():
        issue(i + 1, 1 - par)

    drain(par)

    rows = 64
    for c in range(tm // rows):
        rs = slice(c * rows, (c + 1) * rows)
        q1 = info_ref[rs, 2:3]
        q2 = info_ref[rs, 3:4]
        y0 = _load_rows(ybuf_ref.at[par, 0], c * rows, rows, stage_ref)
        y1 = _load_rows(ybuf_ref.at[par, 1], c * rows, rows, stage_ref)
        ffn = jnp.concatenate([q1 * a + q2 * b for a, b in zip(y0, y1)], axis=1)
        o_ref[rs, :] = _ln_rows(DN_ALPHA * h1_ref[rs, :] + ffn, g2_ref[...], b2_ref[...])


def _combine(pos0, pos1, h1, info, g2, b2, ys):
    n = h1.shape[0]
    return pl.pallas_call(
        _combine_kernel,
        grid_spec=pltpu.PrefetchScalarGridSpec(
            num_scalar_prefetch=2,
            grid=(n // TM_COMB,),
            in_specs=[
                pl.BlockSpec((TM_COMB, D_MODEL), lambda i, *_: (i, 0)),
                pl.BlockSpec((TM_COMB, ROUTE_W), lambda i, *_: (i, 0)),
                pl.BlockSpec((1, D_MODEL), lambda i, *_: (0, 0)),
                pl.BlockSpec((1, D_MODEL), lambda i, *_: (0, 0)),
                pl.BlockSpec(memory_space=pl.ANY),
            ],
            out_specs=pl.BlockSpec((TM_COMB, D_MODEL), lambda i, *_: (i, 0)),
            scratch_shapes=[
                pltpu.VMEM((2, 2, TM_COMB, RT, LANES), BF16),
                pltpu.VMEM((64 * PITCH, LANES), F32),
                pltpu.SemaphoreType.DMA((2,)),
            ],
        ),
        out_shape=jax.ShapeDtypeStruct((n, D_MODEL), F32),
        compiler_params=pltpu.CompilerParams(
            dimension_semantics=("arbitrary",),
            vmem_limit_bytes=VMEM_LIMIT),
        name="combine",
    )(pos0, pos1, h1, info, g2, b2, ys)


def _block_diag(w, per):
    h, hd, _ = w.shape
    wg = w.reshape(h // per, per, hd, hd)
    eye = jnp.eye(per, dtype=w.dtype)
    return jnp.einsum("gpij,pq->gpiqj", wg, eye).reshape(h // per, per * hd, per * hd)


def kernel(x, ln_in_g, ln_in_b, w_in, lru_conv_w, lru_conv_b, lru_w_a, lru_b_a, lru_w_x, lru_b_x,
           lru_lambda, conf_conv_w, conf_conv_b, conf_ln_g, conf_ln_b, w_out, ln1_g, ln1_b,
           router_group_w, router_group_b, router_expert_w, router_expert_b, exp_w_gate, exp_w_up,
           exp_w_down, ln2_g, ln2_b):
    bsz, seq, d = x.shape
    n = bsz * seq
    x2 = x.reshape(n, d)
    row = lambda v: v.reshape(1, -1).astype(F32)
    l = 0

    z, h0 = _ln_win(x2, row(ln_in_g), row(ln_in_b), w_in[l])

    per = CB_LRU // LRU_HEAD_DIM
    wcat = jnp.concatenate([_block_diag(lru_w_a[l], per), _block_diag(lru_w_x[l], per)],
                           axis=-1).astype(BF16)
    a_out = _lru_mixer(z, lru_conv_w[l], row(lru_conv_b[l]), wcat, row(lru_b_a[l]),
                       row(lru_b_x[l]), row(lru_lambda[l]), bsz, seq)

    nlb = D_CONV // LANES
    w3 = jnp.pad(conf_conv_w[l], ((0, 32 - CONF_CONV_W), (0, 0)))
    w3 = w3.reshape(32, nlb, LANES).transpose(1, 0, 2)
    cb3 = conf_conv_b[l].reshape(nlb, 1, LANES)
    b_out, wo = _conf_mixer(z, w3, cb3, row(conf_ln_g[l]), row(conf_ln_b[l]), w_out[l], bsz, seq)

    wr = jnp.concatenate([router_group_w[l], router_expert_w[l]], axis=1)
    wr = jnp.pad(wr, ((0, 0), (0, ROUTE_W - wr.shape[1])))
    wr_hi = wr.astype(BF16)
    wr_lo = (wr - wr_hi.astype(F32)).astype(BF16)
    wr_cat = jnp.concatenate([wr_hi, wr_lo], axis=1)
    br = jnp.concatenate([router_group_b[l], router_expert_b[l]])
    br = jnp.pad(br, (0, ROUTE_W - br.shape[0])).reshape(1, ROUTE_W)
    h1, h1p, logits = _wout_router(a_out, b_out, h0, wo, row(ln1_g[l]), row(ln1_b[l]), wr_cat, br)

    info, counts = _route(logits)
    idx = info[:, 0:6].astype(jnp.int32)
    r0, r1, e0, e1 = idx[:, 0], idx[:, 1], idx[:, 4], idx[:, 5]

    n_tiles = (n * 2) // TM_EXP + N_EXPERTS
    tile_expert, n_used, next_expert, slot, starts, pad_start, pad_len = _plan(
        counts[0].astype(jnp.int32), n_tiles)

    expert_ids = jnp.arange(N_EXPERTS, dtype=jnp.int32)[None, :]
    start_of = lambda e: jnp.sum(jnp.where(e[:, None] == expert_ids, starts[None, :], 0), axis=1)
    pos0 = (start_of(e0) + r0).astype(jnp.int32)
    pos1 = (start_of(e1) + r1).astype(jnp.int32)

    xs = _dispatch(pos0, pos1, pad_start, pad_len, h1p, n_tiles * TM_EXP)
    shp = (N_EXPERTS, D_MODEL, D_EXPERT)
    ys = _expert_ffn(tile_expert, n_used, next_expert, slot, xs, exp_w_gate[l].reshape(shp),
                     exp_w_up[l].reshape(shp), exp_w_down[l].reshape(N_EXPERTS, D_EXPERT, D_MODEL))
    out = _combine(pos0, pos1, h1, info, row(ln2_g[l]), row(ln2_b[l]), ys)
    return out.reshape(bsz, seq, d)
```

```python
import functools
import math

import jax
import jax.numpy as jnp
from jax import lax
from jax.experimental import pallas as pl
from jax.experimental.pallas import tpu as pltpu

F32 = jnp.float32
BF16 = jnp.bfloat16

D_MODEL = 2048
D_LRU = 1024
D_CONV = 1024
LRU_HEADS = 16
LRU_HEAD_DIM = 64
LRU_C = 8.0
LRU_CONV_W = 4
CONF_CONV_W = 31
N_GROUPS = 4
EXPERTS_PER_GROUP = 8
N_EXPERTS = N_GROUPS * EXPERTS_PER_GROUP
D_EXPERT = 512
LN_EPS = 1e-5
DEPTH = 1
DN_ALPHA = (2 * DEPTH) ** 0.25

LANES = 128
SUBLANES = 8
VMEM_LIMIT = 56 * 1024 * 1024

TM_WIN = 512
TN_WIN = 1024
W_CHUNK = 256
TT_LRU = 4096
CB_LRU = 256
TT_CONF = 256
CONF_HALO = 32
TM_OUT = 512
TM_ROUTE = 512
TM_DISP = 1024
TM_EXP = 256
TM_COMB = 256
ROUTE_W = 128


def _sigmoid(x):
    return 0.5 * (jnp.tanh(0.5 * x) + 1.0)


def _ln_rows(x, g, b):
    mu = jnp.mean(x, axis=-1, keepdims=True)
    xc = x - mu
    var = jnp.mean(xc * xc, axis=-1, keepdims=True)
    return xc * lax.rsqrt(var + LN_EPS) * g + b


def _stage_weight(w_hbm, wb_ref, wst_ref, wsem):
    nchunk = wb_ref.shape[1] // W_CHUNK

    def chunk_copy(c):
        return pltpu.make_async_copy(w_hbm.at[:, pl.ds(c * W_CHUNK, W_CHUNK)],
                                     wst_ref.at[c % 2], wsem.at[c % 2])

    chunk_copy(0).start()
    for c in range(nchunk):
        if c + 1 < nchunk:
            chunk_copy(c + 1).start()
        chunk_copy(c).wait()
        wb_ref[:, c * W_CHUNK:(c + 1) * W_CHUNK] = wst_ref[c % 2].astype(BF16)


def _ln_win_kernel(x_ref, g_ref, b_ref, w_hbm, z_ref, h_ref, wb_ref, wst_ref, xn_ref, wsem):
    s = pl.program_id(0)
    par = lax.rem(s, 2)

    @pl.when(s == 0)
    def _():
        xn_ref[1] = jnp.zeros(xn_ref.shape[1:], BF16)
        _stage_weight(w_hbm, wb_ref, wst_ref, wsem)

    rows = 128
    for c in range(TM_WIN // rows):
        rs = slice(c * rows, (c + 1) * rows)
        hn = _ln_rows(x_ref[rs, :], g_ref[...], b_ref[...])
        h_ref[rs, :] = hn
        xn_ref[par, rs, :] = hn.astype(BF16)

    xprev = xn_ref[1 - par]
    for c in range(z_ref.shape[1] // TN_WIN):
        cs = slice(c * TN_WIN, (c + 1) * TN_WIN)
        z_ref[:, cs] = jnp.dot(xprev, wb_ref[:, cs], preferred_element_type=F32).astype(z_ref.dtype)


def _ln_win(x2, g, b, w):
    n = x2.shape[0]
    ncol = w.shape[1]
    nt = n // TM_WIN
    return pl.pallas_call(
        _ln_win_kernel,
        grid=(nt + 1,),
        in_specs=[
            pl.BlockSpec((TM_WIN, D_MODEL), lambda s: (jnp.minimum(s, nt - 1), 0)),
            pl.BlockSpec((1, D_MODEL), lambda s: (0, 0)),
            pl.BlockSpec((1, D_MODEL), lambda s: (0, 0)),
            pl.BlockSpec(memory_space=pl.ANY),
        ],
        out_specs=[
            pl.BlockSpec((TM_WIN, ncol), lambda s: (jnp.maximum(s - 1, 0), 0)),
            pl.BlockSpec((TM_WIN, D_MODEL), lambda s: (jnp.minimum(s, nt - 1), 0)),
        ],
        out_shape=[
            jax.ShapeDtypeStruct((n, ncol), BF16),
            jax.ShapeDtypeStruct((n, D_MODEL), F32),
        ],
        scratch_shapes=[
            pltpu.VMEM((D_MODEL, ncol), BF16),
            pltpu.VMEM((2, D_MODEL, W_CHUNK), F32),
            pltpu.VMEM((2, TM_WIN, D_MODEL), BF16),
            pltpu.SemaphoreType.DMA((2,)),
        ],
        compiler_params=pltpu.CompilerParams(
            dimension_semantics=("arbitrary",),
            vmem_limit_bytes=VMEM_LIMIT),
        name="ln_win",
    )(x2, g, b, w)


def _lru_kernel(zx_ref, zg_ref, cw_ref, cb_ref, wcat_ref, ba_ref, bx_ref, lam_ref,
                o_ref, xs_ref, hp_ref, a_ref, g_ref):
    t = pl.program_id(2)
    tt = TT_LRU

    @pl.when(t == 0)
    def _():
        xs_ref[0:SUBLANES, :] = jnp.zeros((SUBLANES, CB_LRU), F32)
        hp_ref[...] = jnp.zeros_like(hp_ref)

    @pl.when(t > 0)
    def _():
        xs_ref[0:SUBLANES, :] = xs_ref[tt:tt + SUBLANES, :]

    xs_ref[SUBLANES:SUBLANES + tt, :] = zx_ref[...].astype(F32)

    rows = 128
    for rb in range(tt // rows):
        acc = jnp.broadcast_to(cb_ref[...], (rows, CB_LRU))
        for k in range(LRU_CONV_W):
            off = rb * rows + SUBLANES - (LRU_CONV_W - 1) + k
            acc = acc + cw_ref[k:k + 1, :] * xs_ref[off:off + rows, :]
        a_ref[rb * rows:(rb + 1) * rows, :] = acc

    g_ref[...] = jnp.dot(a_ref[...].astype(BF16), wcat_ref[0], preferred_element_type=F32)

    lam = lam_ref[...]
    softplus_neg = jnp.maximum(-lam, 0.0) + jnp.log1p(jnp.exp(-jnp.abs(lam)))
    cvec = -LRU_C * softplus_neg
    ba = ba_ref[...]
    bx = bx_ref[...]
    blk = 64
    row_in_vreg = lax.broadcasted_iota(jnp.int32, (blk, CB_LRU), 0) & (SUBLANES - 1)

    def body(rb, h):
        rs = pl.ds(pl.multiple_of(rb * blk, blk), blk)
        a_in = a_ref[rs, :]
        r = _sigmoid(g_ref[rs, 0:CB_LRU] + ba)
        i = _sigmoid(g_ref[rs, CB_LRU:2 * CB_LRU] + bx)
        log_a = cvec * r
        a = jnp.exp(log_a)
        m2 = -jnp.tanh(log_a) * (a * a + 1.0)
        u = jnp.where(m2 > 0.0, m2 * lax.rsqrt(m2), 0.0) * (i * a_in)
        for s in (1, 2, 4):
            m = row_in_vreg >= s
            a_sh = jnp.where(m, pltpu.roll(a, s, 0), 1.0)
            u_sh = jnp.where(m, pltpu.roll(u, s, 0), 0.0)
            u = u + a * u_sh
            a = a * a_sh
        outs = []
        for gi in range(blk // SUBLANES):
            ag = a[gi * SUBLANES:(gi + 1) * SUBLANES, :]
            ug = u[gi * SUBLANES:(gi + 1) * SUBLANES, :]
            hg = ug + ag * h
            h = hg[SUBLANES - 1:SUBLANES, :]
            outs.append(hg)
        hblk = jnp.concatenate(outs, axis=0)
        gl = zg_ref[rs, :].astype(F32)
        gelu = 0.5 * gl * (1.0 + jnp.tanh(0.7978845608028654 * (gl + 0.044715 * gl * gl * gl)))
        o_ref[rs, :] = (gelu * hblk).astype(o_ref.dtype)
        return h

    h = lax.fori_loop(0, tt // blk, body, hp_ref[0:1, :])
    hp_ref[...] = jnp.broadcast_to(h, hp_ref.shape)


def _lru_mixer(z, cw, cb, wcat, ba, bx, lam, bsz, seq):
    n = z.shape[0]
    nt = seq // TT_LRU
    ncb = D_LRU // CB_LRU
    row = lambda b, j, t: b * nt + t
    vec = pl.BlockSpec((1, CB_LRU), lambda b, j, t: (0, j))
    return pl.pallas_call(
        _lru_kernel,
        grid=(bsz, ncb, nt),
        in_specs=[
            pl.BlockSpec((TT_LRU, CB_LRU), lambda b, j, t: (row(b, j, t), j)),
            pl.BlockSpec((TT_LRU, CB_LRU), lambda b, j, t: (row(b, j, t), ncb + j)),
            pl.BlockSpec((LRU_CONV_W, CB_LRU), lambda b, j, t: (0, j)),
            vec,
            pl.BlockSpec((1, CB_LRU, 2 * CB_LRU), lambda b, j, t: (j, 0, 0)),
            vec, vec, vec,
        ],
        out_specs=pl.BlockSpec((TT_LRU, CB_LRU), lambda b, j, t: (row(b, j, t), j)),
        out_shape=jax.ShapeDtypeStruct((n, D_LRU), BF16),
        scratch_shapes=[
            pltpu.VMEM((TT_LRU + SUBLANES, CB_LRU), F32),
            pltpu.VMEM((SUBLANES, CB_LRU), F32),
            pltpu.VMEM((TT_LRU, CB_LRU), F32),
            pltpu.VMEM((TT_LRU, 2 * CB_LRU), F32),
        ],
        compiler_params=pltpu.CompilerParams(
            dimension_semantics=("arbitrary", "arbitrary", "arbitrary"),
            vmem_limit_bytes=VMEM_LIMIT),
        name="lru_mixer",
    )(z, z, cw, cb, wcat, ba, bx, lam)


def _conf_kernel(zv_ref, zg_ref, w_ref, cb_ref, lg_ref, lb_ref, wo_ref, o_ref, wob_ref,
                 cs_ref, cv_ref):
    t = pl.program_id(1)
    tt = TT_CONF
    nlb = D_CONV // LANES
    wob_ref[...] = wo_ref[...].astype(BF16)

    @pl.when(t == 0)
    def _():
        cs_ref[:, 0:CONF_HALO, :] = jnp.zeros((nlb, CONF_HALO, LANES), F32)

    @pl.when(t > 0)
    def _():
        cs_ref[:, 0:CONF_HALO, :] = cs_ref[:, tt:tt + CONF_HALO, :]

    for c in range(nlb):
        ls = slice(c * LANES, (c + 1) * LANES)
        v = zv_ref[:, ls].astype(F32)
        g = zg_ref[:, ls].astype(F32)
        cs_ref[c, CONF_HALO:CONF_HALO + tt, :] = v * _sigmoid(g)

    rows = 64
    nrb = tt // rows
    base = CONF_HALO - (CONF_CONV_W - 1)

    def conv_body(c, carry):
        accs = [jnp.broadcast_to(cb_ref[c], (rows, LANES)) for _ in range(nrb)]
        for k in range(CONF_CONV_W):
            wk = w_ref[c, k:k + 1, :]
            for rb in range(nrb):
                off = rb * rows + base + k
                accs[rb] = accs[rb] + wk * cs_ref[c, off:off + rows, :]
        for rb in range(nrb):
            cv_ref[c, rb * rows:(rb + 1) * rows, :] = accs[rb]
        return carry

    lax.fori_loop(0, nlb, conv_body, 0)

    ln_rows = 32
    inv_n = 1.0 / D_CONV
    for rb in range(tt // ln_rows):
        rs = slice(rb * ln_rows, (rb + 1) * ln_rows)
        blk = cv_ref[:, rs, :]
        mu = jnp.sum(jnp.sum(blk, axis=0), axis=-1, keepdims=True) * inv_n
        d = blk - mu[None]
        var = jnp.sum(jnp.sum(d * d, axis=0), axis=-1, keepdims=True) * inv_n
        inv = lax.rsqrt(var + LN_EPS)
        for c in range(nlb):
            ls = slice(c * LANES, (c + 1) * LANES)
            y = d[c] * inv * lg_ref[:, ls] + lb_ref[:, ls]
            o_ref[rs, ls] = (y * _sigmoid(y)).astype(o_ref.dtype)


def _conf_mixer(z, w3, cb3, lg, lb, wo, bsz, seq):
    n = z.shape[0]
    nt = seq // TT_CONF
    nlb = D_CONV // LANES
    wo_rows = wo.shape[0] // (bsz * nt)
    assert wo_rows * bsz * nt == wo.shape[0] and wo_rows % 16 == 0
    return pl.pallas_call(
        _conf_kernel,
        grid=(bsz, nt),
        in_specs=[
            pl.BlockSpec((TT_CONF, D_CONV), lambda b, t: (b * nt + t, 2)),
            pl.BlockSpec((TT_CONF, D_CONV), lambda b, t: (b * nt + t, 3)),
            pl.BlockSpec((nlb, 32, LANES), lambda b, t: (0, 0, 0)),
            pl.BlockSpec((nlb, 1, LANES), lambda b, t: (0, 0, 0)),
            pl.BlockSpec((1, D_CONV), lambda b, t: (0, 0)),
            pl.BlockSpec((1, D_CONV), lambda b, t: (0, 0)),
            pl.BlockSpec((wo_rows, D_MODEL), lambda b, t: (b * nt + t, 0)),
        ],
        out_specs=[
            pl.BlockSpec((TT_CONF, D_CONV), lambda b, t: (b * nt + t, 0)),
            pl.BlockSpec((wo_rows, D_MODEL), lambda b, t: (b * nt + t, 0)),
        ],
        out_shape=[
            jax.ShapeDtypeStruct((n, D_CONV), BF16),
            jax.ShapeDtypeStruct(wo.shape, BF16),
        ],
        scratch_shapes=[
            pltpu.VMEM((nlb, CONF_HALO + TT_CONF, LANES), F32),
            pltpu.VMEM((nlb, TT_CONF, LANES), F32),
        ],
        compiler_params=pltpu.CompilerParams(
            dimension_semantics=("arbitrary", "arbitrary"),
            vmem_limit_bytes=VMEM_LIMIT),
        name="conf_mixer",
    )(z, z, w3, cb3, lg, lb, wo)


def _split_bf16(v):
    hi = v.astype(BF16)
    lo = (v - hi.astype(F32)).astype(BF16)
    return hi, lo


RT = D_MODEL // LANES
PITCH = RT + SUBLANES


def _store_rows(dst_ref, row0, rows, v, stage_ref):
    for s in range(RT):
        stage_ref[pl.ds(s, rows, stride=PITCH), :] = v[:, s * LANES:(s + 1) * LANES]
    staged = stage_ref[0:rows * PITCH, :].reshape(rows, PITCH, LANES)
    dst_ref[row0:row0 + rows] = staged[:, 0:RT, :].astype(BF16)


def _load_rows(src_ref, row0, rows, stage_ref):
    tile = src_ref[row0:row0 + rows].astype(F32)
    tile = jnp.concatenate([tile, jnp.zeros((rows, PITCH - RT, LANES), F32)], axis=1)
    stage_ref[0:rows * PITCH, :] = tile.reshape(rows * PITCH, LANES)
    return [stage_ref[pl.ds(s, rows, stride=PITCH), :] for s in range(RT)]


def _wout_kernel(a_ref, b_ref, h_ref, wa_ref, wb_ref, g1_ref, b1_ref,
                 wr_ref, br_ref, h1_ref, h1r_ref, lg_ref, mixa_ref, mixb_ref, hl_ref, stage_ref):
    half = TM_OUT // 2
    for mix_ref, r0 in ((mixa_ref, 0), (mixb_ref, half)):
        hs = slice(r0, r0 + half)
        mix_ref[...] = (jnp.dot(a_ref[hs, :], wa_ref[...], preferred_element_type=F32)
                        + jnp.dot(b_ref[hs, :], wb_ref[...], preferred_element_type=F32))
    rows = 64
    for c in range(TM_OUT // rows):
        rs = slice(c * rows, (c + 1) * rows)
        mix_ref, r0 = (mixa_ref, 0) if c * rows < half else (mixb_ref, half)
        mix = mix_ref[c * rows - r0:(c + 1) * rows - r0, :]
        h1 = _ln_rows(DN_ALPHA * h_ref[rs, :] + mix, g1_ref[...], b1_ref[...])
        h1_ref[rs, :] = h1
        _store_rows(h1r_ref, c * rows, rows, h1, stage_ref)
        hi, lo = _split_bf16(h1)
        hl_ref[rs, :] = hi
        hl_ref[TM_OUT + c * rows:TM_OUT + (c + 1) * rows, :] = lo
    p = jnp.dot(hl_ref[...], wr_ref[...], preferred_element_type=F32)
    lg_ref[...] = (p[0:TM_OUT, 0:ROUTE_W] + p[0:TM_OUT, ROUTE_W:2 * ROUTE_W]
                   + p[TM_OUT:2 * TM_OUT, 0:ROUTE_W] + br_ref[...])


def _wout_router(a, b, h, wo, g1, b1, wr_cat, br):
    n = h.shape[0]
    assert D_LRU == D_CONV
    full = lambda shape: pl.BlockSpec(shape, lambda i: tuple(0 for _ in shape))
    return pl.pallas_call(
        _wout_kernel,
        grid=(n // TM_OUT,),
        in_specs=[
            pl.BlockSpec((TM_OUT, D_LRU), lambda i: (i, 0)),
            pl.BlockSpec((TM_OUT, D_CONV), lambda i: (i, 0)),
            pl.BlockSpec((TM_OUT, D_MODEL), lambda i: (i, 0)),
            pl.BlockSpec((D_LRU, D_MODEL), lambda i: (0, 0)),
            pl.BlockSpec((D_CONV, D_MODEL), lambda i: (1, 0)),
            full((1, D_MODEL)), full((1, D_MODEL)),
            full((D_MODEL, 2 * ROUTE_W)), full((1, ROUTE_W)),
        ],
        out_specs=[
            pl.BlockSpec((TM_OUT, D_MODEL), lambda i: (i, 0)),
            pl.BlockSpec((TM_OUT, RT, LANES), lambda i: (i, 0, 0)),
            pl.BlockSpec((TM_OUT, ROUTE_W), lambda i: (i, 0)),
        ],
        out_shape=[
            jax.ShapeDtypeStruct((n, D_MODEL), F32),
            jax.ShapeDtypeStruct((n, RT, LANES), BF16),
            jax.ShapeDtypeStruct((n, ROUTE_W), F32),
        ],
        scratch_shapes=[
            pltpu.VMEM((TM_OUT // 2, D_MODEL), F32),
            pltpu.VMEM((TM_OUT // 2, D_MODEL), F32),
            pltpu.VMEM((2 * TM_OUT, D_MODEL), BF16),
            pltpu.VMEM((64 * PITCH, LANES), F32),
        ],
        compiler_params=pltpu.CompilerParams(
            dimension_semantics=("arbitrary",),
            vmem_limit_bytes=VMEM_LIMIT),
        name="wout_router",
    )(a, b, h, wo, wo, g1, b1, wr_cat, br)


def _route_kernel(lg_ref, info_ref, cnt_ref, run_ref, tri_ref):
    t = pl.program_id(0)
    tm = TM_ROUTE
    l = lg_ref[...]
    lane = lax.broadcasted_iota(jnp.int32, (tm, ROUTE_W), 1)
    neg = jnp.float32(-jnp.inf)
    big = jnp.int32(1 << 20)

    gmask = lane < N_GROUPS
    gmax = jnp.max(jnp.where(gmask, l, neg), axis=-1, keepdims=True)
    gsel = jnp.min(jnp.where(gmask & (l == gmax), lane, big), axis=-1, keepdims=True)
    gsum = jnp.sum(jnp.where(gmask, jnp.exp(l - gmax), 0.0), axis=-1, keepdims=True)
    pg_top = 1.0 / gsum

    lo = N_GROUPS + EXPERTS_PER_GROUP * gsel
    emask = (lane >= lo) & (lane < lo + EXPERTS_PER_GROUP)
    v1 = jnp.max(jnp.where(emask, l, neg), axis=-1, keepdims=True)
    i1 = jnp.min(jnp.where(emask & (l == v1), lane, big), axis=-1, keepdims=True)
    emask2 = emask & (lane != i1)
    v2 = jnp.max(jnp.where(emask2, l, neg), axis=-1, keepdims=True)
    i2 = jnp.min(jnp.where(emask2 & (l == v2), lane, big), axis=-1, keepdims=True)
    e21 = jnp.exp(v2 - v1)
    q1 = pg_top / (1.0 + e21)
    q2 = pg_top * e21 / (1.0 + e21)

    oh1 = (lane == i1).astype(F32)
    oh2 = (lane == i2).astype(F32)
    ohs = oh1 + oh2

    @pl.when(t == 0)
    def _():
        run_ref[...] = jnp.zeros_like(run_ref)
        r_i = lax.broadcasted_iota(jnp.int32, (tm, tm), 0)
        c_i = lax.broadcasted_iota(jnp.int32, (tm, tm), 1)
        tri_ref[...] = (c_i < r_i).astype(BF16)

    cum = jnp.dot(tri_ref[...], ohs.astype(BF16), preferred_element_type=F32)
    basev = run_ref[0:1, :] + cum
    r1 = jnp.sum(oh1 * basev, axis=-1, keepdims=True)
    r2 = jnp.sum(oh2 * basev, axis=-1, keepdims=True)
    run_ref[...] = run_ref[...] + jnp.sum(ohs, axis=0, keepdims=True)
    info = jnp.where(lane == 0, r1, 0.0)
    info = jnp.where(lane == 1, r2, info)
    info = jnp.where(lane == 2, q1, info)
    info = jnp.where(lane == 3, q2, info)
    info = jnp.where(lane == 4, (i1 - N_GROUPS).astype(F32), info)
    info = jnp.where(lane == 5, (i2 - N_GROUPS).astype(F32), info)
    info_ref[...] = info
    cnt_ref[...] = run_ref[...]


def _route(logits):
    n = logits.shape[0]
    nt = n // TM_ROUTE
    return pl.pallas_call(
        _route_kernel,
        grid=(nt,),
        in_specs=[pl.BlockSpec((TM_ROUTE, ROUTE_W), lambda t: (t, 0))],
        out_specs=[
            pl.BlockSpec((TM_ROUTE, ROUTE_W), lambda t: (t, 0)),
            pl.BlockSpec((SUBLANES, ROUTE_W), lambda t: (0, 0)),
        ],
        out_shape=[
            jax.ShapeDtypeStruct((n, ROUTE_W), F32),
            jax.ShapeDtypeStruct((SUBLANES, ROUTE_W), F32),
        ],
        scratch_shapes=[
            pltpu.VMEM((SUBLANES, ROUTE_W), F32),
            pltpu.VMEM((TM_ROUTE, TM_ROUTE), BF16),
        ],
        compiler_params=pltpu.CompilerParams(
            dimension_semantics=("arbitrary",),
            vmem_limit_bytes=VMEM_LIMIT),
        name="route",
    )(logits)


def _plan_kernel(cnt_ref, te_ref, nu_ref, nx_ref, sl_ref, st_ref, ps_ref, pl_ref):
    n_tiles = te_ref.shape[0]
    shift = TM_EXP.bit_length() - 1

    def forward(e, carry):
        tile, parity = carry
        c = cnt_ref[N_GROUPS + e]
        tp = (c + (TM_EXP - 1)) >> shift
        st_ref[e] = tile * TM_EXP
        ps_ref[e] = tile * TM_EXP + c
        pl_ref[e] = tp * TM_EXP - c

        def mark(k, _):
            te_ref[tile + k] = e
            few = (c - k * TM_EXP) <= TM_EXP // 2
            sl_ref[tile + k] = parity + jnp.where(few, 2, 0)
            return 0

        lax.fori_loop(0, tp, mark, 0)
        return tile + tp, jnp.where(tp > 0, 1 - parity, parity)

    used, _ = lax.fori_loop(0, N_EXPERTS, forward, (jnp.int32(0), jnp.int32(0)))
    nu_ref[0] = used

    def backward(j, nxt):
        e = N_EXPERTS - 1 - j
        c = cnt_ref[N_GROUPS + e]
        tp = (c + (TM_EXP - 1)) >> shift
        first = st_ref[e] >> shift

        def mark(k, _):
            nx_ref[first + k] = nxt
            return 0

        lax.fori_loop(0, tp, mark, 0)
        return jnp.where(tp > 0, e, nxt)

    lax.fori_loop(0, N_EXPERTS, backward, jnp.int32(-1))

    last_e = te_ref[jnp.maximum(used - 1, 0)]
    last_s = sl_ref[jnp.maximum(used - 1, 0)]

    def tail(i, _):
        te_ref[i] = last_e
        sl_ref[i] = last_s
        nx_ref[i] = -1
        return 0

    lax.fori_loop(used, n_tiles, tail, 0)


def _plan(counts_i32, n_tiles):
    smem = lambda: pl.BlockSpec(memory_space=pltpu.SMEM)
    i32 = lambda k: jax.ShapeDtypeStruct((k,), jnp.int32)
    return pl.pallas_call(
        _plan_kernel,
        in_specs=[smem()],
        out_specs=[smem() for _ in range(7)],
        out_shape=[i32(n_tiles), i32(1), i32(n_tiles), i32(n_tiles),
                   i32(N_EXPERTS), i32(N_EXPERTS), i32(N_EXPERTS)],
        name="plan",
    )(counts_i32)


def _prow_copy(src_ref, src_row, dst_ref, dst_row, sem, rows=1):
    return pltpu.make_async_copy(src_ref.at[pl.ds(src_row, rows)], dst_ref.at[pl.ds(dst_row, rows)], sem)


ZERO_ROWS = TM_EXP // 2
DMA_UNROLL = 8


def _pad_fill(e, ps_ref, pl_ref, zbuf_ref, xs_ref, zsem, wait):
    ln = pl_ref[e]
    st = ps_ref[e]
    b = 1
    while b <= ZERO_ROWS:
        @pl.when((ln & b) != 0)
        def _(b=b):
            cp = _prow_copy(zbuf_ref, 0, xs_ref, st + (ln & (b - 1)), zsem, rows=b)
            if wait:
                cp.wait()
            else:
                cp.start()
        b *= 2


def _dispatch_kernel(pos0_ref, pos1_ref, ps_ref, pl_ref, h1p_ref, xs_ref, zbuf_ref, sem, zsem):
    i = pl.program_id(0)
    tm = TM_DISP

    @pl.when(i == 0)
    def _():
        zbuf_ref[...] = jnp.zeros_like(zbuf_ref)

        def fill_start(e, _):
            _pad_fill(e, ps_ref, pl_ref, zbuf_ref, xs_ref, zsem, False)
            return 0

        lax.fori_loop(0, N_EXPERTS, fill_start, 0)

    def issue(k, _):
        for u in range(DMA_UNROLL):
            r = k * DMA_UNROLL + u
            tok = i * tm + r
            _prow_copy(h1p_ref, r, xs_ref, pos0_ref[tok], sem).start(priority=0)
            _prow_copy(h1p_ref, r, xs_ref, pos1_ref[tok], sem).start(priority=1)
        return 0

    lax.fori_loop(0, tm // DMA_UNROLL, issue, 0)

    def drain(k, _):
        for u in range(DMA_UNROLL):
            _prow_copy(h1p_ref, 0, xs_ref, 0, sem).wait()
            _prow_copy(h1p_ref, 0, xs_ref, 0, sem).wait()
        return 0

    lax.fori_loop(0, tm // DMA_UNROLL, drain, 0)

    @pl.when(i == pl.num_programs(0) - 1)
    def _():
        def fill_wait(e, _):
            _pad_fill(e, ps_ref, pl_ref, zbuf_ref, xs_ref, zsem, True)
            return 0

        lax.fori_loop(0, N_EXPERTS, fill_wait, 0)


def _dispatch(pos0, pos1, pad_start, pad_len, h1p, n_rows):
    n = h1p.shape[0]
    return pl.pallas_call(
        _dispatch_kernel,
        grid_spec=pltpu.PrefetchScalarGridSpec(
            num_scalar_prefetch=4,
            grid=(n // TM_DISP,),
            in_specs=[pl.BlockSpec((TM_DISP, RT, LANES), lambda i, *_: (i, 0, 0))],
            out_specs=pl.BlockSpec(memory_space=pl.ANY),
            scratch_shapes=[
                pltpu.VMEM((ZERO_ROWS, RT, LANES), BF16),
                pltpu.SemaphoreType.DMA(()),
                pltpu.SemaphoreType.DMA(()),
            ],
        ),
        out_shape=jax.ShapeDtypeStruct((n_rows, RT, LANES), BF16),
        compiler_params=pltpu.CompilerParams(
            dimension_semantics=("arbitrary",),
            vmem_limit_bytes=VMEM_LIMIT),
        name="dispatch",
    )(pos0, pos1, pad_start, pad_len, h1p)


def _expert_kernel(te_ref, nu_ref, nx_ref, sl_ref, xs_ref, wg_hbm, wu_hbm, wd_hbm, ys_ref,
                   wgf_ref, wuf_ref, wdf_ref, wgb_ref, wub_ref, wdb_ref, stage_ref, wsem):
    i = pl.program_id(0)
    used = i < nu_ref[0]
    e = te_ref[i]
    s = sl_ref[i] & 1
    few = (sl_ref[i] & 2) != 0
    fresh = (i == 0) | (e != te_ref[jnp.maximum(i - 1, 0)])

    def weight_copies(expert, slot):
        return (pltpu.make_async_copy(wg_hbm.at[expert], wgf_ref.at[slot], wsem.at[slot, 0]),
                pltpu.make_async_copy(wu_hbm.at[expert], wuf_ref.at[slot], wsem.at[slot, 1]),
                pltpu.make_async_copy(wd_hbm.at[expert], wdf_ref.at[slot], wsem.at[slot, 2]))

    @pl.when(i == 0)
    def _():
        for cp in weight_copies(e, s):
            cp.start(priority=1)

    @pl.when(used & fresh)
    def _():
        for cp in weight_copies(e, s):
            cp.wait()

        @pl.when(nx_ref[i] >= 0)
        def _():
            for cp in weight_copies(nx_ref[i], 1 - s):
                cp.start(priority=1)

        wgb_ref[...] = wgf_ref[s].astype(BF16)
        wub_ref[...] = wuf_ref[s].astype(BF16)
        wdb_ref[...] = wdf_ref[s].astype(BF16)

    def ffn(rows):
        x = jnp.concatenate([p.astype(BF16) for p in _load_rows(xs_ref, 0, rows, stage_ref)],
                            axis=1)
        g = jnp.dot(x, wgb_ref[...], preferred_element_type=F32)
        u = jnp.dot(x, wub_ref[...], preferred_element_type=F32)
        h = (g * _sigmoid(g) * u).astype(BF16)
        y = jnp.dot(h, wdb_ref[...], preferred_element_type=F32)
        _store_rows(ys_ref, 0, rows, y, stage_ref)

    @pl.when(used & jnp.logical_not(few))
    def _():
        ffn(TM_EXP)

    @pl.when(used & few)
    def _():
        ffn(TM_EXP // 2)


def _expert_ffn(tile_expert, n_used, next_expert, slot, xs, wg, wu, wd):
    n_rows = xs.shape[0]
    n_tiles = n_rows // TM_EXP

    def row_map(i, te, nu, nx, sl):
        return (jnp.minimum(i, nu[0] - 1), 0, 0)

    return pl.pallas_call(
        _expert_kernel,
        grid_spec=pltpu.PrefetchScalarGridSpec(
            num_scalar_prefetch=4,
            grid=(n_tiles,),
            in_specs=[
                pl.BlockSpec((TM_EXP, RT, LANES), row_map),
                pl.BlockSpec(memory_space=pl.ANY),
                pl.BlockSpec(memory_space=pl.ANY),
                pl.BlockSpec(memory_space=pl.ANY),
            ],
            out_specs=pl.BlockSpec((TM_EXP, RT, LANES), row_map),
            scratch_shapes=[
                pltpu.VMEM((2, D_MODEL, D_EXPERT), F32),
                pltpu.VMEM((2, D_MODEL, D_EXPERT), F32),
                pltpu.VMEM((2, D_EXPERT, D_MODEL), F32),
                pltpu.VMEM((D_MODEL, D_EXPERT), BF16),
                pltpu.VMEM((D_MODEL, D_EXPERT), BF16),
                pltpu.VMEM((D_EXPERT, D_MODEL), BF16),
                pltpu.VMEM((TM_EXP * PITCH, LANES), F32),
                pltpu.SemaphoreType.DMA((2, 3)),
            ],
        ),
        out_shape=jax.ShapeDtypeStruct((n_rows, RT, LANES), BF16),
        compiler_params=pltpu.CompilerParams(
            dimension_semantics=("arbitrary",),
            vmem_limit_bytes=VMEM_LIMIT),
        name="expert_ffn",
    )(tile_expert, n_used, next_expert, slot, xs, wg, wu, wd)


def _combine_kernel(pos0_ref, pos1_ref, h1_ref, info_ref, g2_ref, b2_ref, ys_ref, o_ref,
                    ybuf_ref, stage_ref, sem):
    i = pl.program_id(0)
    tm = TM_COMB
    par = lax.rem(i, 2)

    def issue(step, parity):
        def body(k, _):
            for u in range(DMA_UNROLL):
                r = k * DMA_UNROLL + u
                tok = step * tm + r
                _prow_copy(ys_ref, pos0_ref[tok], ybuf_ref.at[parity, 0], r,
                           sem.at[parity]).start(priority=0)
                _prow_copy(ys_ref, pos1_ref[tok], ybuf_ref.at[parity, 1], r,
                           sem.at[parity]).start(priority=1)
            return 0
        lax.fori_loop(0, tm // DMA_UNROLL, body, 0)

    def drain(parity):
        def body(k, _):
            for u in range(DMA_UNROLL):
                _prow_copy(ys_ref, 0, ybuf_ref.at[parity, 0], 0, sem.at[parity]).wait()
                _prow_copy(ys_ref, 0, ybuf_ref.at[parity, 1], 0, sem.at[parity]).wait()
            return 0
        lax.fori_loop(0, tm // DMA_UNROLL, body, 0)

    @pl.when(i == 0)
    def _():
        issue(0, 0)

    @pl.when(i + 1 < pl.num_programs(0))
    def _():
        issue(i + 1, 1 - par)

    drain(par)

    rows = 64
    for c in range(tm // rows):
        rs = slice(c * rows, (c + 1) * rows)
        q1 = info_ref[rs, 2:3]
        q2 = info_ref[rs, 3:4]
        y0 = _load_rows(ybuf_ref.at[par, 0], c * rows, rows, stage_ref)
        y1 = _load_rows(ybuf_ref.at[par, 1], c * rows, rows, stage_ref)
        ffn = jnp.concatenate([q1 * a + q2 * b for a, b in zip(y0, y1)], axis=1)
        o_ref[rs, :] = _ln_rows(DN_ALPHA * h1_ref[rs, :] + ffn, g2_ref[...], b2_ref[...])


def _combine(pos0, pos1, h1, info, g2, b2, ys):
    n = h1.shape[0]
    return pl.pallas_call(
        _combine_kernel,
        grid_spec=pltpu.PrefetchScalarGridSpec(
            num_scalar_prefetch=2,
            grid=(n // TM_COMB,),
            in_specs=[
                pl.BlockSpec((TM_COMB, D_MODEL), lambda i, *_: (i, 0)),
                pl.BlockSpec((TM_COMB, ROUTE_W), lambda i, *_: (i, 0)),
                pl.BlockSpec((1, D_MODEL), lambda i, *_: (0, 0)),
                pl.BlockSpec((1, D_MODEL), lambda i, *_: (0, 0)),
                pl.BlockSpec(memory_space=pl.ANY),
            ],
            out_specs=pl.BlockSpec((TM_COMB, D_MODEL), lambda i, *_: (i, 0)),
            scratch_shapes=[
                pltpu.VMEM((2, 2, TM_COMB, RT, LANES), BF16),
                pltpu.VMEM((64 * PITCH, LANES), F32),
                pltpu.SemaphoreType.DMA((2,)),
            ],
        ),
        out_shape=jax.ShapeDtypeStruct((n, D_MODEL), F32),
        compiler_params=pltpu.CompilerParams(
            dimension_semantics=("arbitrary",),
            vmem_limit_bytes=VMEM_LIMIT),
        name="combine",
    )(pos0, pos1, h1, info, g2, b2, ys)


def _block_diag(w, per):
    h, hd, _ = w.shape
    wg = w.reshape(h // per, per, hd, hd)
    eye = jnp.eye(per, dtype=w.dtype)
    return jnp.einsum("gpij,pq->gpiqj", wg, eye).reshape(h // per, per * hd, per * hd)


def kernel(x, ln_in_g, ln_in_b, w_in, lru_conv_w, lru_conv_b, lru_w_a, lru_b_a, lru_w_x, lru_b_x,
           lru_lambda, conf_conv_w, conf_conv_b, conf_ln_g, conf_ln_b, w_out, ln1_g, ln1_b,
           router_group_w, router_group_b, router_expert_w, router_expert_b, exp_w_gate, exp_w_up,
           exp_w_down, ln2_g, ln2_b):
    bsz, seq, d = x.shape
    n = bsz * seq
    x2 = x.reshape(n, d)
    row = lambda v: v.reshape(1, -1).astype(F32)
    l = 0

    z, h0 = _ln_win(x2, row(ln_in_g), row(ln_in_b), w_in[l])

    per = CB_LRU // LRU_HEAD_DIM
    wcat = jnp.concatenate([_block_diag(lru_w_a[l], per), _block_diag(lru_w_x[l], per)],
                           axis=-1).astype(BF16)
    a_out = _lru_mixer(z, lru_conv_w[l], row(lru_conv_b[l]), wcat, row(lru_b_a[l]),
                       row(lru_b_x[l]), row(lru_lambda[l]), bsz, seq)

    nlb = D_CONV // LANES
    w3 = jnp.pad(conf_conv_w[l], ((0, 32 - CONF_CONV_W), (0, 0)))
    w3 = w3.reshape(32, nlb, LANES).transpose(1, 0, 2)
    cb3 = conf_conv_b[l].reshape(nlb, 1, LANES)
    b_out, wo = _conf_mixer(z, w3, cb3, row(conf_ln_g[l]), row(conf_ln_b[l]), w_out[l], bsz, seq)

    wr = jnp.concatenate([router_group_w[l], router_expert_w[l]], axis=1)
    wr = jnp.pad(wr, ((0, 0), (0, ROUTE_W - wr.shape[1])))
    wr_hi = wr.astype(BF16)
    wr_lo = (wr - wr_hi.astype(F32)).astype(BF16)
    wr_cat = jnp.concatenate([wr_hi, wr_lo], axis=1)
    br = jnp.concatenate([router_group_b[l], router_expert_b[l]])
    br = jnp.pad(br, (0, ROUTE_W - br.shape[0])).reshape(1, ROUTE_W)
    h1, h1p, logits = _wout_router(a_out, b_out, h0, wo, row(ln1_g[l]), row(ln1_b[l]), wr_cat, br)

    info, counts = _route(logits)
    idx = info[:, 0:6].astype(jnp.int32)
    r0, r1, e0, e1 = idx[:, 0], idx[:, 1], idx[:, 4], idx[:, 5]

    n_tiles = (n * 2) // TM_EXP + N_EXPERTS
    tile_expert, n_used, next_expert, slot, starts, pad_start, pad_len = _plan(
        counts[0].astype(jnp.int32), n_tiles)

    expert_ids = jnp.arange(N_EXPERTS, dtype=jnp.int32)[None, :]
    start_of = lambda e: jnp.sum(jnp.where(e[:, None] == expert_ids, starts[None, :], 0), axis=1)
    pos0 = (start_of(e0) + r0).astype(jnp.int32)
    pos1 = (start_of(e1) + r1).astype(jnp.int32)

    xs = _dispatch(pos0, pos1, pad_start, pad_len, h1p, n_tiles * TM_EXP)
    shp = (N_EXPERTS, D_MODEL, D_EXPERT)
    ys = _expert_ffn(tile_expert, n_used, next_expert, slot, xs, exp_w_gate[l].reshape(shp),
                     exp_w_up[l].reshape(shp), exp_w_down[l].reshape(N_EXPERTS, D_EXPERT, D_MODEL))
    out = _combine(pos0, pos1, h1, info, row(ln2_g[l]), row(ln2_b[l]), ys)
    return out.reshape(bsz, seq, d)
```

```python
import functools
import math

import jax
import jax.numpy as jnp
from jax import lax
from jax.experimental import pallas as pl
from jax.experimental.pallas import tpu as pltpu

F32 = jnp.float32
BF16 = jnp.bfloat16

D_MODEL = 2048
D_LRU = 1024
D_CONV = 1024
LRU_HEADS = 16
LRU_HEAD_DIM = 64
LRU_C = 8.0
LRU_CONV_W = 4
CONF_CONV_W = 31
N_GROUPS = 4
EXPERTS_PER_GROUP = 8
N_EXPERTS = N_GROUPS * EXPERTS_PER_GROUP
D_EXPERT = 512
LN_EPS = 1e-5
DEPTH = 1
DN_ALPHA = (2 * DEPTH) ** 0.25

LANES = 128
SUBLANES = 8
VMEM_LIMIT = 56 * 1024 * 1024

TM_WIN = 512
TN_WIN = 1024
W_CHUNK = 256
TT_LRU = 4096
CB_LRU = 256
TT_CONF = 256
CONF_HALO = 32
TM_OUT = 512
TM_ROUTE = 512
TM_DISP = 1024
TM_EXP = 512
EXP_PATH_ROWS = 128
TM_COMB = 256
ROUTE_W = 128


def _sigmoid(x):
    return 0.5 * (jnp.tanh(0.5 * x) + 1.0)


def _ln_rows(x, g, b):
    mu = jnp.mean(x, axis=-1, keepdims=True)
    xc = x - mu
    var = jnp.mean(xc * xc, axis=-1, keepdims=True)
    return xc * lax.rsqrt(var + LN_EPS) * g + b


def _stage_weight(w_hbm, wb_ref, wst_ref, wsem):
    nchunk = wb_ref.shape[1] // W_CHUNK

    def chunk_copy(c):
        return pltpu.make_async_copy(w_hbm.at[:, pl.ds(c * W_CHUNK, W_CHUNK)],
                                     wst_ref.at[c % 2], wsem.at[c % 2])

    chunk_copy(0).start()
    for c in range(nchunk):
        if c + 1 < nchunk:
            chunk_copy(c + 1).start()
        chunk_copy(c).wait()
        wb_ref[:, c * W_CHUNK:(c + 1) * W_CHUNK] = wst_ref[c % 2].astype(BF16)


def _ln_win_kernel(x_ref, g_ref, b_ref, w_hbm, z_ref, h_ref, wb_ref, wst_ref, xn_ref, wsem):
    s = pl.program_id(0)
    par = lax.rem(s, 2)

    @pl.when(s == 0)
    def _():
        xn_ref[1] = jnp.zeros(xn_ref.shape[1:], BF16)
        _stage_weight(w_hbm, wb_ref, wst_ref, wsem)

    rows = 128
    for c in range(TM_WIN // rows):
        rs = slice(c * rows, (c + 1) * rows)
        hn = _ln_rows(x_ref[rs, :], g_ref[...], b_ref[...])
        h_ref[rs, :] = hn
        xn_ref[par, rs, :] = hn.astype(BF16)

    xprev = xn_ref[1 - par]
    for c in range(z_ref.shape[1] // TN_WIN):
        cs = slice(c * TN_WIN, (c + 1) * TN_WIN)
        z_ref[:, cs] = jnp.dot(xprev, wb_ref[:, cs], preferred_element_type=F32).astype(z_ref.dtype)


def _ln_win(x2, g, b, w):
    n = x2.shape[0]
    ncol = w.shape[1]
    nt = n // TM_WIN
    return pl.pallas_call(
        _ln_win_kernel,
        grid=(nt + 1,),
        in_specs=[
            pl.BlockSpec((TM_WIN, D_MODEL), lambda s: (jnp.minimum(s, nt - 1), 0)),
            pl.BlockSpec((1, D_MODEL), lambda s: (0, 0)),
            pl.BlockSpec((1, D_MODEL), lambda s: (0, 0)),
            pl.BlockSpec(memory_space=pl.ANY),
        ],
        out_specs=[
            pl.BlockSpec((TM_WIN, ncol), lambda s: (jnp.maximum(s - 1, 0), 0)),
            pl.BlockSpec((TM_WIN, D_MODEL), lambda s: (jnp.minimum(s, nt - 1), 0)),
        ],
        out_shape=[
            jax.ShapeDtypeStruct((n, ncol), BF16),
            jax.ShapeDtypeStruct((n, D_MODEL), F32),
        ],
        scratch_shapes=[
            pltpu.VMEM((D_MODEL, ncol), BF16),
            pltpu.VMEM((2, D_MODEL, W_CHUNK), F32),
            pltpu.VMEM((2, TM_WIN, D_MODEL), BF16),
            pltpu.SemaphoreType.DMA((2,)),
        ],
        compiler_params=pltpu.CompilerParams(
            dimension_semantics=("arbitrary",),
            vmem_limit_bytes=VMEM_LIMIT),
        name="ln_win",
    )(x2, g, b, w)


def _lru_kernel(zx_ref, zg_ref, cw_ref, cb_ref, wcat_ref, ba_ref, bx_ref, lam_ref,
                o_ref, xs_ref, hp_ref, a_ref, g_ref):
    t = pl.program_id(2)
    tt = TT_LRU

    @pl.when(t == 0)
    def _():
        xs_ref[0:SUBLANES, :] = jnp.zeros((SUBLANES, CB_LRU), F32)
        hp_ref[...] = jnp.zeros_like(hp_ref)

    @pl.when(t > 0)
    def _():
        xs_ref[0:SUBLANES, :] = xs_ref[tt:tt + SUBLANES, :]

    xs_ref[SUBLANES:SUBLANES + tt, :] = zx_ref[...].astype(F32)

    rows = 128
    for rb in range(tt // rows):
        acc = jnp.broadcast_to(cb_ref[...], (rows, CB_LRU))
        for k in range(LRU_CONV_W):
            off = rb * rows + SUBLANES - (LRU_CONV_W - 1) + k
            acc = acc + cw_ref[k:k + 1, :] * xs_ref[off:off + rows, :]
        a_ref[rb * rows:(rb + 1) * rows, :] = acc

    g_ref[...] = jnp.dot(a_ref[...].astype(BF16), wcat_ref[0], preferred_element_type=F32)

    lam = lam_ref[...]
    softplus_neg = jnp.maximum(-lam, 0.0) + jnp.log1p(jnp.exp(-jnp.abs(lam)))
    cvec = -LRU_C * softplus_neg
    ba = ba_ref[...]
    bx = bx_ref[...]
    blk = 64
    row_in_vreg = lax.broadcasted_iota(jnp.int32, (blk, CB_LRU), 0) & (SUBLANES - 1)

    def body(rb, h):
        rs = pl.ds(pl.multiple_of(rb * blk, blk), blk)
        a_in = a_ref[rs, :]
        r = _sigmoid(g_ref[rs, 0:CB_LRU] + ba)
        i = _sigmoid(g_ref[rs, CB_LRU:2 * CB_LRU] + bx)
        log_a = cvec * r
        a = jnp.exp(log_a)
        m2 = -jnp.tanh(log_a) * (a * a + 1.0)
        u = jnp.where(m2 > 0.0, m2 * lax.rsqrt(m2), 0.0) * (i * a_in)
        for s in (1, 2, 4):
            m = row_in_vreg >= s
            a_sh = jnp.where(m, pltpu.roll(a, s, 0), 1.0)
            u_sh = jnp.where(m, pltpu.roll(u, s, 0), 0.0)
            u = u + a * u_sh
            a = a * a_sh
        outs = []
        for gi in range(blk // SUBLANES):
            ag = a[gi * SUBLANES:(gi + 1) * SUBLANES, :]
            ug = u[gi * SUBLANES:(gi + 1) * SUBLANES, :]
            hg = ug + ag * h
            h = hg[SUBLANES - 1:SUBLANES, :]
            outs.append(hg)
        hblk = jnp.concatenate(outs, axis=0)
        gl = zg_ref[rs, :].astype(F32)
        gelu = 0.5 * gl * (1.0 + jnp.tanh(0.7978845608028654 * (gl + 0.044715 * gl * gl * gl)))
        o_ref[rs, :] = (gelu * hblk).astype(o_ref.dtype)
        return h

    h = lax.fori_loop(0, tt // blk, body, hp_ref[0:1, :])
    hp_ref[...] = jnp.broadcast_to(h, hp_ref.shape)


def _lru_mixer(z, cw, cb, wcat, ba, bx, lam, bsz, seq):
    n = z.shape[0]
    nt = seq // TT_LRU
    ncb = D_LRU // CB_LRU
    row = lambda b, j, t: b * nt + t
    vec = pl.BlockSpec((1, CB_LRU), lambda b, j, t: (0, j))
    return pl.pallas_call(
        _lru_kernel,
        grid=(bsz, ncb, nt),
        in_specs=[
            pl.BlockSpec((TT_LRU, CB_LRU), lambda b, j, t: (row(b, j, t), j)),
            pl.BlockSpec((TT_LRU, CB_LRU), lambda b, j, t: (row(b, j, t), ncb + j)),
            pl.BlockSpec((LRU_CONV_W, CB_LRU), lambda b, j, t: (0, j)),
            vec,
            pl.BlockSpec((1, CB_LRU, 2 * CB_LRU), lambda b, j, t: (j, 0, 0)),
            vec, vec, vec,
        ],
        out_specs=pl.BlockSpec((TT_LRU, CB_LRU), lambda b, j, t: (row(b, j, t), j)),
        out_shape=jax.ShapeDtypeStruct((n, D_LRU), BF16),
        scratch_shapes=[
            pltpu.VMEM((TT_LRU + SUBLANES, CB_LRU), F32),
            pltpu.VMEM((SUBLANES, CB_LRU), F32),
            pltpu.VMEM((TT_LRU, CB_LRU), F32),
            pltpu.VMEM((TT_LRU, 2 * CB_LRU), F32),
        ],
        compiler_params=pltpu.CompilerParams(
            dimension_semantics=("arbitrary", "arbitrary", "arbitrary"),
            vmem_limit_bytes=VMEM_LIMIT),
        name="lru_mixer",
    )(z, z, cw, cb, wcat, ba, bx, lam)


def _conf_kernel(zv_ref, zg_ref, w_ref, cb_ref, lg_ref, lb_ref, wo_ref, o_ref, wob_ref,
                 cs_ref, cv_ref):
    t = pl.program_id(1)
    tt = TT_CONF
    nlb = D_CONV // LANES
    wob_ref[...] = wo_ref[...].astype(BF16)

    @pl.when(t == 0)
    def _():
        cs_ref[:, 0:CONF_HALO, :] = jnp.zeros((nlb, CONF_HALO, LANES), F32)

    @pl.when(t > 0)
    def _():
        cs_ref[:, 0:CONF_HALO, :] = cs_ref[:, tt:tt + CONF_HALO, :]

    for c in range(nlb):
        ls = slice(c * LANES, (c + 1) * LANES)
        v = zv_ref[:, ls].astype(F32)
        g = zg_ref[:, ls].astype(F32)
        cs_ref[c, CONF_HALO:CONF_HALO + tt, :] = v * _sigmoid(g)

    rows = 64
    nrb = tt // rows
    base = CONF_HALO - (CONF_CONV_W - 1)

    def conv_body(c, carry):
        accs = [jnp.broadcast_to(cb_ref[c], (rows, LANES)) for _ in range(nrb)]
        for k in range(CONF_CONV_W):
            wk = w_ref[c, k:k + 1, :]
            for rb in range(nrb):
                off = rb * rows + base + k
                accs[rb] = accs[rb] + wk * cs_ref[c, off:off + rows, :]
        for rb in range(nrb):
            cv_ref[c, rb * rows:(rb + 1) * rows, :] = accs[rb]
        return carry

    lax.fori_loop(0, nlb, conv_body, 0)

    ln_rows = 32
    inv_n = 1.0 / D_CONV
    for rb in range(tt // ln_rows):
        rs = slice(rb * ln_rows, (rb + 1) * ln_rows)
        blk = cv_ref[:, rs, :]
        mu = jnp.sum(jnp.sum(blk, axis=0), axis=-1, keepdims=True) * inv_n
        d = blk - mu[None]
        var = jnp.sum(jnp.sum(d * d, axis=0), axis=-1, keepdims=True) * inv_n
        inv = lax.rsqrt(var + LN_EPS)
        for c in range(nlb):
            ls = slice(c * LANES, (c + 1) * LANES)
            y = d[c] * inv * lg_ref[:, ls] + lb_ref[:, ls]
            o_ref[rs, ls] = (y * _sigmoid(y)).astype(o_ref.dtype)


def _conf_mixer(z, w3, cb3, lg, lb, wo, bsz, seq):
    n = z.shape[0]
    nt = seq // TT_CONF
    nlb = D_CONV // LANES
    wo_rows = wo.shape[0] // (bsz * nt)
    assert wo_rows * bsz * nt == wo.shape[0] and wo_rows % 16 == 0
    return pl.pallas_call(
        _conf_kernel,
        grid=(bsz, nt),
        in_specs=[
            pl.BlockSpec((TT_CONF, D_CONV), lambda b, t: (b * nt + t, 2)),
            pl.BlockSpec((TT_CONF, D_CONV), lambda b, t: (b * nt + t, 3)),
            pl.BlockSpec((nlb, 32, LANES), lambda b, t: (0, 0, 0)),
            pl.BlockSpec((nlb, 1, LANES), lambda b, t: (0, 0, 0)),
            pl.BlockSpec((1, D_CONV), lambda b, t: (0, 0)),
            pl.BlockSpec((1, D_CONV), lambda b, t: (0, 0)),
            pl.BlockSpec((wo_rows, D_MODEL), lambda b, t: (b * nt + t, 0)),
        ],
        out_specs=[
            pl.BlockSpec((TT_CONF, D_CONV), lambda b, t: (b * nt + t, 0)),
            pl.BlockSpec((wo_rows, D_MODEL), lambda b, t: (b * nt + t, 0)),
        ],
        out_shape=[
            jax.ShapeDtypeStruct((n, D_CONV), BF16),
            jax.ShapeDtypeStruct(wo.shape, BF16),
        ],
        scratch_shapes=[
            pltpu.VMEM((nlb, CONF_HALO + TT_CONF, LANES), F32),
            pltpu.VMEM((nlb, TT_CONF, LANES), F32),
        ],
        compiler_params=pltpu.CompilerParams(
            dimension_semantics=("arbitrary", "arbitrary"),
            vmem_limit_bytes=VMEM_LIMIT),
        name="conf_mixer",
    )(z, z, w3, cb3, lg, lb, wo)


def _split_bf16(v):
    hi = v.astype(BF16)
    lo = (v - hi.astype(F32)).astype(BF16)
    return hi, lo


RT = D_MODEL // LANES
PITCH = RT + SUBLANES


def _store_rows(dst_ref, row0, rows, v, stage_ref):
    for s in range(RT):
        stage_ref[pl.ds(s, rows, stride=PITCH), :] = v[:, s * LANES:(s + 1) * LANES]
    staged = stage_ref[0:rows * PITCH, :].reshape(rows, PITCH, LANES)
    dst_ref[row0:row0 + rows] = staged[:, 0:RT, :].astype(BF16)


def _load_rows(src_ref, row0, rows, stage_ref):
    tile = src_ref[row0:row0 + rows].astype(F32)
    tile = jnp.concatenate([tile, jnp.zeros((rows, PITCH - RT, LANES), F32)], axis=1)
    stage_ref[0:rows * PITCH, :] = tile.reshape(rows * PITCH, LANES)
    return [stage_ref[pl.ds(s, rows, stride=PITCH), :] for s in range(RT)]


def _wout_kernel(a_ref, b_ref, h_ref, wa_ref, wb_ref, g1_ref, b1_ref,
                 wr_ref, br_ref, h1_ref, h1r_ref, lg_ref, mixa_ref, mixb_ref, hl_ref, stage_ref):
    half = TM_OUT // 2
    for mix_ref, r0 in ((mixa_ref, 0), (mixb_ref, half)):
        hs = slice(r0, r0 + half)
        mix_ref[...] = (jnp.dot(a_ref[hs, :], wa_ref[...], preferred_element_type=F32)
                        + jnp.dot(b_ref[hs, :], wb_ref[...], preferred_element_type=F32))
    rows = 64
    for c in range(TM_OUT // rows):
        rs = slice(c * rows, (c + 1) * rows)
        mix_ref, r0 = (mixa_ref, 0) if c * rows < half else (mixb_ref, half)
        mix = mix_ref[c * rows - r0:(c + 1) * rows - r0, :]
        h1 = _ln_rows(DN_ALPHA * h_ref[rs, :] + mix, g1_ref[...], b1_ref[...])
        h1_ref[rs, :] = h1
        _store_rows(h1r_ref, c * rows, rows, h1, stage_ref)
        hi, lo = _split_bf16(h1)
        hl_ref[rs, :] = hi
        hl_ref[TM_OUT + c * rows:TM_OUT + (c + 1) * rows, :] = lo
    p = jnp.dot(hl_ref[...], wr_ref[...], preferred_element_type=F32)
    lg_ref[...] = (p[0:TM_OUT, 0:ROUTE_W] + p[0:TM_OUT, ROUTE_W:2 * ROUTE_W]
                   + p[TM_OUT:2 * TM_OUT, 0:ROUTE_W] + br_ref[...])


def _wout_router(a, b, h, wo, g1, b1, wr_cat, br):
    n = h.shape[0]
    assert D_LRU == D_CONV
    full = lambda shape: pl.BlockSpec(shape, lambda i: tuple(0 for _ in shape))
    return pl.pallas_call(
        _wout_kernel,
        grid=(n // TM_OUT,),
        in_specs=[
            pl.BlockSpec((TM_OUT, D_LRU), lambda i: (i, 0)),
            pl.BlockSpec((TM_OUT, D_CONV), lambda i: (i, 0)),
            pl.BlockSpec((TM_OUT, D_MODEL), lambda i: (i, 0)),
            pl.BlockSpec((D_LRU, D_MODEL), lambda i: (0, 0)),
            pl.BlockSpec((D_CONV, D_MODEL), lambda i: (1, 0)),
            full((1, D_MODEL)), full((1, D_MODEL)),
            full((D_MODEL, 2 * ROUTE_W)), full((1, ROUTE_W)),
        ],
        out_specs=[
            pl.BlockSpec((TM_OUT, D_MODEL), lambda i: (i, 0)),
            pl.BlockSpec((TM_OUT, RT, LANES), lambda i: (i, 0, 0)),
            pl.BlockSpec((TM_OUT, ROUTE_W), lambda i: (i, 0)),
        ],
        out_shape=[
            jax.ShapeDtypeStruct((n, D_MODEL), F32),
            jax.ShapeDtypeStruct((n, RT, LANES), BF16),
            jax.ShapeDtypeStruct((n, ROUTE_W), F32),
        ],
        scratch_shapes=[
            pltpu.VMEM((TM_OUT // 2, D_MODEL), F32),
            pltpu.VMEM((TM_OUT // 2, D_MODEL), F32),
            pltpu.VMEM((2 * TM_OUT, D_MODEL), BF16),
            pltpu.VMEM((64 * PITCH, LANES), F32),
        ],
        compiler_params=pltpu.CompilerParams(
            dimension_semantics=("arbitrary",),
            vmem_limit_bytes=VMEM_LIMIT),
        name="wout_router",
    )(a, b, h, wo, wo, g1, b1, wr_cat, br)


def _route_kernel(lg_ref, info_ref, cnt_ref, run_ref, tri_ref):
    t = pl.program_id(0)
    tm = TM_ROUTE
    l = lg_ref[...]
    lane = lax.broadcasted_iota(jnp.int32, (tm, ROUTE_W), 1)
    neg = jnp.float32(-jnp.inf)
    big = jnp.int32(1 << 20)

    gmask = lane < N_GROUPS
    gmax = jnp.max(jnp.where(gmask, l, neg), axis=-1, keepdims=True)
    gsel = jnp.min(jnp.where(gmask & (l == gmax), lane, big), axis=-1, keepdims=True)
    gsum = jnp.sum(jnp.where(gmask, jnp.exp(l - gmax), 0.0), axis=-1, keepdims=True)
    pg_top = 1.0 / gsum

    lo = N_GROUPS + EXPERTS_PER_GROUP * gsel
    emask = (lane >= lo) & (lane < lo + EXPERTS_PER_GROUP)
    v1 = jnp.max(jnp.where(emask, l, neg), axis=-1, keepdims=True)
    i1 = jnp.min(jnp.where(emask & (l == v1), lane, big), axis=-1, keepdims=True)
    emask2 = emask & (lane != i1)
    v2 = jnp.max(jnp.where(emask2, l, neg), axis=-1, keepdims=True)
    i2 = jnp.min(jnp.where(emask2 & (l == v2), lane, big), axis=-1, keepdims=True)
    e21 = jnp.exp(v2 - v1)
    q1 = pg_top / (1.0 + e21)
    q2 = pg_top * e21 / (1.0 + e21)

    oh1 = (lane == i1).astype(F32)
    oh2 = (lane == i2).astype(F32)
    ohs = oh1 + oh2

    @pl.when(t == 0)
    def _():
        run_ref[...] = jnp.zeros_like(run_ref)
        r_i = lax.broadcasted_iota(jnp.int32, (tm, tm), 0)
        c_i = lax.broadcasted_iota(jnp.int32, (tm, tm), 1)
        tri_ref[...] = (c_i < r_i).astype(BF16)

    cum = jnp.dot(tri_ref[...], ohs.astype(BF16), preferred_element_type=F32)
    basev = run_ref[0:1, :] + cum
    r1 = jnp.sum(oh1 * basev, axis=-1, keepdims=True)
    r2 = jnp.sum(oh2 * basev, axis=-1, keepdims=True)
    run_ref[...] = run_ref[...] + jnp.sum(ohs, axis=0, keepdims=True)
    info = jnp.where(lane == 0, r1, 0.0)
    info = jnp.where(lane == 1, r2, info)
    info = jnp.where(lane == 2, q1, info)
    info = jnp.where(lane == 3, q2, info)
    info = jnp.where(lane == 4, (i1 - N_GROUPS).astype(F32), info)
    info = jnp.where(lane == 5, (i2 - N_GROUPS).astype(F32), info)
    info_ref[...] = info
    cnt_ref[...] = run_ref[...]


def _route(logits):
    n = logits.shape[0]
    nt = n // TM_ROUTE
    return pl.pallas_call(
        _route_kernel,
        grid=(nt,),
        in_specs=[pl.BlockSpec((TM_ROUTE, ROUTE_W), lambda t: (t, 0))],
        out_specs=[
            pl.BlockSpec((TM_ROUTE, ROUTE_W), lambda t: (t, 0)),
            pl.BlockSpec((SUBLANES, ROUTE_W), lambda t: (0, 0)),
        ],
        out_shape=[
            jax.ShapeDtypeStruct((n, ROUTE_W), F32),
            jax.ShapeDtypeStruct((SUBLANES, ROUTE_W), F32),
        ],
        scratch_shapes=[
            pltpu.VMEM((SUBLANES, ROUTE_W), F32),
            pltpu.VMEM((TM_ROUTE, TM_ROUTE), BF16),
        ],
        compiler_params=pltpu.CompilerParams(
            dimension_semantics=("arbitrary",),
            vmem_limit_bytes=VMEM_LIMIT),
        name="route",
    )(logits)


def _plan_kernel(cnt_ref, te_ref, nu_ref, nx_ref, sl_ref, st_ref, ps_ref, pl_ref):
    n_tiles = te_ref.shape[0]
    shift = TM_EXP.bit_length() - 1

    def forward(e, carry):
        tile, parity = carry
        c = cnt_ref[N_GROUPS + e]
        tp = (c + (TM_EXP - 1)) >> shift
        st_ref[e] = tile * TM_EXP
        ps_ref[e] = tile * TM_EXP + c
        pl_ref[e] = tp * TM_EXP - c

        def mark(k, _):
            te_ref[tile + k] = e
            real = jnp.minimum(c - k * TM_EXP, TM_EXP)
            pieces = (real + (EXP_PATH_ROWS - 1)) >> (EXP_PATH_ROWS.bit_length() - 1)
            sl_ref[tile + k] = parity + 2 * (pieces - 1)
            return 0

        lax.fori_loop(0, tp, mark, 0)
        return tile + tp, jnp.where(tp > 0, 1 - parity, parity)

    used, _ = lax.fori_loop(0, N_EXPERTS, forward, (jnp.int32(0), jnp.int32(0)))
    nu_ref[0] = used

    def backward(j, nxt):
        e = N_EXPERTS - 1 - j
        c = cnt_ref[N_GROUPS + e]
        tp = (c + (TM_EXP - 1)) >> shift
        first = st_ref[e] >> shift

        def mark(k, _):
            nx_ref[first + k] = nxt
            return 0

        lax.fori_loop(0, tp, mark, 0)
        return jnp.where(tp > 0, e, nxt)

    lax.fori_loop(0, N_EXPERTS, backward, jnp.int32(-1))

    last_e = te_ref[jnp.maximum(used - 1, 0)]
    last_s = sl_ref[jnp.maximum(used - 1, 0)]

    def tail(i, _):
        te_ref[i] = last_e
        sl_ref[i] = last_s
        nx_ref[i] = -1
        return 0

    lax.fori_loop(used, n_tiles, tail, 0)


def _plan(counts_i32, n_tiles):
    smem = lambda: pl.BlockSpec(memory_space=pltpu.SMEM)
    i32 = lambda k: jax.ShapeDtypeStruct((k,), jnp.int32)
    return pl.pallas_call(
        _plan_kernel,
        in_specs=[smem()],
        out_specs=[smem() for _ in range(7)],
        out_shape=[i32(n_tiles), i32(1), i32(n_tiles), i32(n_tiles),
                   i32(N_EXPERTS), i32(N_EXPERTS), i32(N_EXPERTS)],
        name="plan",
    )(counts_i32)


def _prow_copy(src_ref, src_row, dst_ref, dst_row, sem, rows=1):
    return pltpu.make_async_copy(src_ref.at[pl.ds(src_row, rows)], dst_ref.at[pl.ds(dst_row, rows)], sem)


ZERO_ROWS = TM_EXP // 2
DMA_UNROLL = 8


def _pad_fill(e, ps_ref, pl_ref, zbuf_ref, xs_ref, zsem, wait):
    ln = pl_ref[e]
    st = ps_ref[e]
    b = 1
    while b <= ZERO_ROWS:
        @pl.when((ln & b) != 0)
        def _(b=b):
            cp = _prow_copy(zbuf_ref, 0, xs_ref, st + (ln & (b - 1)), zsem, rows=b)
            if wait:
                cp.wait()
            else:
                cp.start()
        b *= 2


def _dispatch_kernel(pos0_ref, pos1_ref, ps_ref, pl_ref, h1p_ref, xs_ref, zbuf_ref, sem, zsem):
    i = pl.program_id(0)
    tm = TM_DISP

    @pl.when(i == 0)
    def _():
        zbuf_ref[...] = jnp.zeros_like(zbuf_ref)

        def fill_start(e, _):
            _pad_fill(e, ps_ref, pl_ref, zbuf_ref, xs_ref, zsem, False)
            return 0

        lax.fori_loop(0, N_EXPERTS, fill_start, 0)

    def issue(k, _):
        for u in range(DMA_UNROLL):
            r = k * DMA_UNROLL + u
            tok = i * tm + r
            _prow_copy(h1p_ref, r, xs_ref, pos0_ref[tok], sem).start(priority=0)
            _prow_copy(h1p_ref, r, xs_ref, pos1_ref[tok], sem).start(priority=1)
        return 0

    lax.fori_loop(0, tm // DMA_UNROLL, issue, 0)

    def drain(k, _):
        for u in range(DMA_UNROLL):
            _prow_copy(h1p_ref, 0, xs_ref, 0, sem).wait()
            _prow_copy(h1p_ref, 0, xs_ref, 0, sem).wait()
        return 0

    lax.fori_loop(0, tm // DMA_UNROLL, drain, 0)

    @pl.when(i == pl.num_programs(0) - 1)
    def _():
        def fill_wait(e, _):
            _pad_fill(e, ps_ref, pl_ref, zbuf_ref, xs_ref, zsem, True)
            return 0

        lax.fori_loop(0, N_EXPERTS, fill_wait, 0)


def _dispatch(pos0, pos1, pad_start, pad_len, h1p, n_rows):
    n = h1p.shape[0]
    return pl.pallas_call(
        _dispatch_kernel,
        grid_spec=pltpu.PrefetchScalarGridSpec(
            num_scalar_prefetch=4,
            grid=(n // TM_DISP,),
            in_specs=[pl.BlockSpec((TM_DISP, RT, LANES), lambda i, *_: (i, 0, 0))],
            out_specs=pl.BlockSpec(memory_space=pl.ANY),
            scratch_shapes=[
                pltpu.VMEM((ZERO_ROWS, RT, LANES), BF16),
                pltpu.SemaphoreType.DMA(()),
                pltpu.SemaphoreType.DMA(()),
            ],
        ),
        out_shape=jax.ShapeDtypeStruct((n_rows, RT, LANES), BF16),
        compiler_params=pltpu.CompilerParams(
            dimension_semantics=("arbitrary",),
            vmem_limit_bytes=VMEM_LIMIT),
        name="dispatch",
    )(pos0, pos1, pad_start, pad_len, h1p)


def _expert_kernel(te_ref, nu_ref, nx_ref, sl_ref, xs_ref, wg_hbm, wu_hbm, wd_hbm, ys_ref,
                   wgf_ref, wuf_ref, wdf_ref, wgb_ref, wub_ref, wdb_ref, stage_ref, wsem):
    i = pl.program_id(0)
    used = i < nu_ref[0]
    e = te_ref[i]
    s = sl_ref[i] & 1
    pieces = (sl_ref[i] >> 1) + 1
    fresh = (i == 0) | (e != te_ref[jnp.maximum(i - 1, 0)])

    def weight_copies(expert, slot):
        return (pltpu.make_async_copy(wg_hbm.at[expert], wgf_ref.at[slot], wsem.at[slot, 0]),
                pltpu.make_async_copy(wu_hbm.at[expert], wuf_ref.at[slot], wsem.at[slot, 1]),
                pltpu.make_async_copy(wd_hbm.at[expert], wdf_ref.at[slot], wsem.at[slot, 2]))

    @pl.when(i == 0)
    def _():
        for cp in weight_copies(e, s):
            cp.start(priority=1)

    @pl.when(used & fresh)
    def _():
        for cp in weight_copies(e, s):
            cp.wait()

        @pl.when(nx_ref[i] >= 0)
        def _():
            for cp in weight_copies(nx_ref[i], 1 - s):
                cp.start(priority=1)

        wgb_ref[...] = wgf_ref[s].astype(BF16)
        wub_ref[...] = wuf_ref[s].astype(BF16)
        wdb_ref[...] = wdf_ref[s].astype(BF16)

    def ffn(rows):
        x = jnp.concatenate([p.astype(BF16) for p in _load_rows(xs_ref, 0, rows, stage_ref)],
                            axis=1)
        g = jnp.dot(x, wgb_ref[...], preferred_element_type=F32)
        u = jnp.dot(x, wub_ref[...], preferred_element_type=F32)
        h = (g * _sigmoid(g) * u).astype(BF16)
        y = jnp.dot(h, wdb_ref[...], preferred_element_type=F32)
        _store_rows(ys_ref, 0, rows, y, stage_ref)

    for p in range(1, TM_EXP // EXP_PATH_ROWS + 1):
        @pl.when(used & (pieces == p))
        def _(p=p):
            ffn(p * EXP_PATH_ROWS)


def _expert_ffn(tile_expert, n_used, next_expert, slot, xs, wg, wu, wd):
    n_rows = xs.shape[0]
    n_tiles = n_rows // TM_EXP

    def row_map(i, te, nu, nx, sl):
        return (jnp.minimum(i, nu[0] - 1), 0, 0)

    return pl.pallas_call(
        _expert_kernel,
        grid_spec=pltpu.PrefetchScalarGridSpec(
            num_scalar_prefetch=4,
            grid=(n_tiles,),
            in_specs=[
                pl.BlockSpec((TM_EXP, RT, LANES), row_map),
                pl.BlockSpec(memory_space=pl.ANY),
                pl.BlockSpec(memory_space=pl.ANY),
                pl.BlockSpec(memory_space=pl.ANY),
            ],
            out_specs=pl.BlockSpec((TM_EXP, RT, LANES), row_map),
            scratch_shapes=[
                pltpu.VMEM((2, D_MODEL, D_EXPERT), F32),
                pltpu.VMEM((2, D_MODEL, D_EXPERT), F32),
                pltpu.VMEM((2, D_EXPERT, D_MODEL), F32),
                pltpu.VMEM((D_MODEL, D_EXPERT), BF16),
                pltpu.VMEM((D_MODEL, D_EXPERT), BF16),
                pltpu.VMEM((D_EXPERT, D_MODEL), BF16),
                pltpu.VMEM((TM_EXP * PITCH, LANES), F32),
                pltpu.SemaphoreType.DMA((2, 3)),
            ],
        ),
        out_shape=jax.ShapeDtypeStruct((n_rows, RT, LANES), BF16),
        compiler_params=pltpu.CompilerParams(
            dimension_semantics=("arbitrary",),
            vmem_limit_bytes=VMEM_LIMIT),
        name="expert_ffn",
    )(tile_expert, n_used, next_expert, slot, xs, wg, wu, wd)


def _combine_kernel(pos0_ref, pos1_ref, h1_ref, info_ref, g2_ref, b2_ref, ys_ref, o_ref,
                    ybuf_ref, stage_ref, sem):
    i = pl.program_id(0)
    tm = TM_COMB
    par = lax.rem(i, 2)

    def issue(step, parity):
        def body(k, _):
            for u in range(DMA_UNROLL):
                r = k * DMA_UNROLL + u
                tok = step * tm + r
                _prow_copy(ys_ref, pos0_ref[tok], ybuf_ref.at[parity, 0], r,
                           sem.at[parity]).start(priority=0)
                _prow_copy(ys_ref, pos1_ref[tok], ybuf_ref.at[parity, 1], r,
                           sem.at[parity]).start(priority=1)
            return 0
        lax.fori_loop(0, tm // DMA_UNROLL, body, 0)

    def drain(parity):
        def body(k, _):
            for u in range(DMA_UNROLL):
                _prow_copy(ys_ref, 0, ybuf_ref.at[parity, 0], 0, sem.at[parity]).wait()
                _prow_copy(ys_ref, 0, ybuf_ref.at[parity, 1], 0, sem.at[parity]).wait()
            return 0
        lax.fori_loop(0, tm // DMA_UNROLL, body, 0)

    @pl.when(i == 0)
    def _():
        issue(0, 0)

    @pl.when(i + 1 < pl.num_programs(0))
    def _():
        issue(i + 1, 1 - par)

    drain(par)

    rows = 64
    for c in range(tm // rows):
        rs = slice(c * rows, (c + 1) * rows)
        q1 = info_ref[rs, 2:3]
        q2 = info_ref[rs, 3:4]
        y0 = _load_rows(ybuf_ref.at[par, 0], c * rows, rows, stage_ref)
        y1 = _load_rows(ybuf_ref.at[par, 1], c * rows, rows, stage_ref)
        ffn = jnp.concatenate([q1 * a + q2 * b for a, b in zip(y0, y1)], axis=1)
        o_ref[rs, :] = _ln_rows(DN_ALPHA * h1_ref[rs, :] + ffn, g2_ref[...], b2_ref[...])


def _combine(pos0, pos1, h1, info, g2, b2, ys):
    n = h1.shape[0]
    return pl.pallas_call(
        _combine_kernel,
        grid_spec=pltpu.PrefetchScalarGridSpec(
            num_scalar_prefetch=2,
            grid=(n // TM_COMB,),
            in_specs=[
                pl.BlockSpec((TM_COMB, D_MODEL), lambda i, *_: (i, 0)),
                pl.BlockSpec((TM_COMB, ROUTE_W), lambda i, *_: (i, 0)),
                pl.BlockSpec((1, D_MODEL), lambda i, *_: (0, 0)),
                pl.BlockSpec((1, D_MODEL), lambda i, *_: (0, 0)),
                pl.BlockSpec(memory_space=pl.ANY),
            ],
            out_specs=pl.BlockSpec((TM_COMB, D_MODEL), lambda i, *_: (i, 0)),
            scratch_shapes=[
                pltpu.VMEM((2, 2, TM_COMB, RT, LANES), BF16),
                pltpu.VMEM((64 * PITCH, LANES), F32),
                pltpu.SemaphoreType.DMA((2,)),
            ],
        ),
        out_shape=jax.ShapeDtypeStruct((n, D_MODEL), F32),
        compiler_params=pltpu.CompilerParams(
            dimension_semantics=("arbitrary",),
            vmem_limit_bytes=VMEM_LIMIT),
        name="combine",
    )(pos0, pos1, h1, info, g2, b2, ys)


def _block_diag(w, per):
    h, hd, _ = w.shape
    wg = w.reshape(h // per, per, hd, hd)
    eye = jnp.eye(per, dtype=w.dtype)
    return jnp.einsum("gpij,pq->gpiqj", wg, eye).reshape(h // per, per * hd, per * hd)


def kernel(x, ln_in_g, ln_in_b, w_in, lru_conv_w, lru_conv_b, lru_w_a, lru_b_a, lru_w_x, lru_b_x,
           lru_lambda, conf_conv_w, conf_conv_b, conf_ln_g, conf_ln_b, w_out, ln1_g, ln1_b,
           router_group_w, router_group_b, router_expert_w, router_expert_b, exp_w_gate, exp_w_up,
           exp_w_down, ln2_g, ln2_b):
    bsz, seq, d = x.shape
    n = bsz * seq
    x2 = x.reshape(n, d)
    row = lambda v: v.reshape(1, -1).astype(F32)
    l = 0

    z, h0 = _ln_win(x2, row(ln_in_g), row(ln_in_b), w_in[l])

    per = CB_LRU // LRU_HEAD_DIM
    wcat = jnp.concatenate([_block_diag(lru_w_a[l], per), _block_diag(lru_w_x[l], per)],
                           axis=-1).astype(BF16)
    a_out = _lru_mixer(z, lru_conv_w[l], row(lru_conv_b[l]), wcat, row(lru_b_a[l]),
                       row(lru_b_x[l]), row(lru_lambda[l]), bsz, seq)

    nlb = D_CONV // LANES
    w3 = jnp.pad(conf_conv_w[l], ((0, 32 - CONF_CONV_W), (0, 0)))
    w3 = w3.reshape(32, nlb, LANES).transpose(1, 0, 2)
    cb3 = conf_conv_b[l].reshape(nlb, 1, LANES)
    b_out, wo = _conf_mixer(z, w3, cb3, row(conf_ln_g[l]), row(conf_ln_b[l]), w_out[l], bsz, seq)

    wr = jnp.concatenate([router_group_w[l], router_expert_w[l]], axis=1)
    wr = jnp.pad(wr, ((0, 0), (0, ROUTE_W - wr.shape[1])))
    wr_hi = wr.astype(BF16)
    wr_lo = (wr - wr_hi.astype(F32)).astype(BF16)
    wr_cat = jnp.concatenate([wr_hi, wr_lo], axis=1)
    br = jnp.concatenate([router_group_b[l], router_expert_b[l]])
    br = jnp.pad(br, (0, ROUTE_W - br.shape[0])).reshape(1, ROUTE_W)
    h1, h1p, logits = _wout_router(a_out, b_out, h0, wo, row(ln1_g[l]), row(ln1_b[l]), wr_cat, br)

    info, counts = _route(logits)
    idx = info[:, 0:6].astype(jnp.int32)
    r0, r1, e0, e1 = idx[:, 0], idx[:, 1], idx[:, 4], idx[:, 5]

    n_tiles = (n * 2) // TM_EXP + N_EXPERTS
    tile_expert, n_used, next_expert, slot, starts, pad_start, pad_len = _plan(
        counts[0].astype(jnp.int32), n_tiles)

    expert_ids = jnp.arange(N_EXPERTS, dtype=jnp.int32)[None, :]
    start_of = lambda e: jnp.sum(jnp.where(e[:, None] == expert_ids, starts[None, :], 0), axis=1)
    pos0 = (start_of(e0) + r0).astype(jnp.int32)
    pos1 = (start_of(e1) + r1).astype(jnp.int32)

    xs = _dispatch(pos0, pos1, pad_start, pad_len, h1p, n_tiles * TM_EXP)
    shp = (N_EXPERTS, D_MODEL, D_EXPERT)
    ys = _expert_ffn(tile_expert, n_used, next_expert, slot, xs, exp_w_gate[l].reshape(shp),
                     exp_w_up[l].reshape(shp), exp_w_down[l].reshape(N_EXPERTS, D_EXPERT, D_MODEL))
    out = _combine(pos0, pos1, h1, info, row(ln2_g[l]), row(ln2_b[l]), ys)
    return out.reshape(bsz, seq, d)
```

```python
import jax
import jax.numpy as jnp
from jax import lax
from jax.experimental import pallas as pl
from jax.experimental.pallas import tpu as pltpu

F32 = jnp.float32
BF16 = jnp.bfloat16

D_MODEL = 2048
D_LRU = 1024
D_CONV = 1024
LRU_HEADS = 16
LRU_HEAD_DIM = 64
LRU_C = 8.0
LRU_CONV_W = 4
CONF_CONV_W = 31
N_GROUPS = 4
EXPERTS_PER_GROUP = 8
N_EXPERTS = N_GROUPS * EXPERTS_PER_GROUP
D_EXPERT = 512
LN_EPS = 1e-5
DEPTH = 1
DN_ALPHA = (2 * DEPTH) ** 0.25

LANES = 128
SUBLANES = 8
VMEM_LIMIT = 56 * 1024 * 1024

TM_WIN = 512
TN_WIN = 1024
W_CHUNK = 256
TT_LRU = 4096
CB_LRU = 256
TT_CONF = 256
CONF_HALO = 32
TM_OUT = 512
TM_ROUTE = 1024
TM_DISP = 1024
TM_EXP = 256
EXP_PATH_ROWS = 128
TM_COMB = 256
ROUTE_W = 128


def _sigmoid(x):
    return 0.5 * (jnp.tanh(0.5 * x) + 1.0)


def _ln_rows(x, g, b):
    mu = jnp.mean(x, axis=-1, keepdims=True)
    xc = x - mu
    var = jnp.mean(xc * xc, axis=-1, keepdims=True)
    return xc * lax.rsqrt(var + LN_EPS) * g + b


def _stage_weight(w_hbm, wb_ref, wst_ref, wsem):
    nchunk = wb_ref.shape[1] // W_CHUNK

    def chunk_copy(c):
        return pltpu.make_async_copy(w_hbm.at[:, pl.ds(c * W_CHUNK, W_CHUNK)],
                                     wst_ref.at[c % 2], wsem.at[c % 2])

    chunk_copy(0).start()
    for c in range(nchunk):
        if c + 1 < nchunk:
            chunk_copy(c + 1).start()
        chunk_copy(c).wait()
        wb_ref[:, c * W_CHUNK:(c + 1) * W_CHUNK] = wst_ref[c % 2].astype(BF16)


def _ln_win_kernel(x_ref, g_ref, b_ref, w_hbm, z_ref, h_ref, wb_ref, wst_ref, xn_ref, wsem):
    s = pl.program_id(0)
    par = lax.rem(s, 2)

    @pl.when(s == 0)
    def _():
        xn_ref[1] = jnp.zeros(xn_ref.shape[1:], BF16)
        _stage_weight(w_hbm, wb_ref, wst_ref, wsem)

    rows = 128
    for c in range(TM_WIN // rows):
        rs = slice(c * rows, (c + 1) * rows)
        hn = _ln_rows(x_ref[rs, :], g_ref[...], b_ref[...])
        h_ref[rs, :] = hn
        xn_ref[par, rs, :] = hn.astype(BF16)

    xprev = xn_ref[1 - par]
    for c in range(z_ref.shape[1] // TN_WIN):
        cs = slice(c * TN_WIN, (c + 1) * TN_WIN)
        z_ref[:, cs] = jnp.dot(xprev, wb_ref[:, cs], preferred_element_type=F32).astype(z_ref.dtype)


def _ln_win(x2, g, b, w):
    n = x2.shape[0]
    ncol = w.shape[1]
    nt = n // TM_WIN
    return pl.pallas_call(
        _ln_win_kernel,
        grid=(nt + 1,),
        in_specs=[
            pl.BlockSpec((TM_WIN, D_MODEL), lambda s: (jnp.minimum(s, nt - 1), 0)),
            pl.BlockSpec((1, D_MODEL), lambda s: (0, 0)),
            pl.BlockSpec((1, D_MODEL), lambda s: (0, 0)),
            pl.BlockSpec(memory_space=pl.ANY),
        ],
        out_specs=[
            pl.BlockSpec((TM_WIN, ncol), lambda s: (jnp.maximum(s - 1, 0), 0)),
            pl.BlockSpec((TM_WIN, D_MODEL), lambda s: (jnp.minimum(s, nt - 1), 0)),
        ],
        out_shape=[
            jax.ShapeDtypeStruct((n, ncol), BF16),
            jax.ShapeDtypeStruct((n, D_MODEL), F32),
        ],
        scratch_shapes=[
            pltpu.VMEM((D_MODEL, ncol), BF16),
            pltpu.VMEM((2, D_MODEL, W_CHUNK), F32),
            pltpu.VMEM((2, TM_WIN, D_MODEL), BF16),
            pltpu.SemaphoreType.DMA((2,)),
        ],
        compiler_params=pltpu.CompilerParams(
            dimension_semantics=("arbitrary",),
            vmem_limit_bytes=VMEM_LIMIT),
        name="ln_win",
    )(x2, g, b, w)


def _lru_kernel(zx_ref, zg_ref, cw_ref, cb_ref, wcat_ref, ba_ref, bx_ref, lam_ref,
                o_ref, xs_ref, hp_ref, a_ref, g_ref):
    t = pl.program_id(2)
    tt = TT_LRU

    @pl.when(t == 0)
    def _():
        xs_ref[0:SUBLANES, :] = jnp.zeros((SUBLANES, CB_LRU), F32)
        hp_ref[...] = jnp.zeros_like(hp_ref)

    @pl.when(t > 0)
    def _():
        xs_ref[0:SUBLANES, :] = xs_ref[tt:tt + SUBLANES, :]

    xs_ref[SUBLANES:SUBLANES + tt, :] = zx_ref[...].astype(F32)

    rows = 128
    for rb in range(tt // rows):
        acc = jnp.broadcast_to(cb_ref[...], (rows, CB_LRU))
        for k in range(LRU_CONV_W):
            off = rb * rows + SUBLANES - (LRU_CONV_W - 1) + k
            acc = acc + cw_ref[k:k + 1, :] * xs_ref[off:off + rows, :]
        a_ref[rb * rows:(rb + 1) * rows, :] = acc

    g_ref[...] = jnp.dot(a_ref[...].astype(BF16), wcat_ref[0], preferred_element_type=F32)

    lam = lam_ref[...]
    softplus_neg = jnp.maximum(-lam, 0.0) + jnp.log1p(jnp.exp(-jnp.abs(lam)))
    cvec = -LRU_C * softplus_neg
    ba = ba_ref[...]
    bx = bx_ref[...]
    blk = 64
    row_in_vreg = lax.broadcasted_iota(jnp.int32, (blk, CB_LRU), 0) & (SUBLANES - 1)

    def body(rb, h):
        rs = pl.ds(pl.multiple_of(rb * blk, blk), blk)
        a_in = a_ref[rs, :]
        r = _sigmoid(g_ref[rs, 0:CB_LRU] + ba)
        i = _sigmoid(g_ref[rs, CB_LRU:2 * CB_LRU] + bx)
        log_a = cvec * r
        a = jnp.exp(log_a)
        m2 = -jnp.tanh(log_a) * (a * a + 1.0)
        u = jnp.where(m2 > 0.0, m2 * lax.rsqrt(m2), 0.0) * (i * a_in)
        for s in (1, 2, 4):
            m = row_in_vreg >= s
            a_sh = jnp.where(m, pltpu.roll(a, s, 0), 1.0)
            u_sh = jnp.where(m, pltpu.roll(u, s, 0), 0.0)
            u = u + a * u_sh
            a = a * a_sh
        outs = []
        for gi in range(blk // SUBLANES):
            ag = a[gi * SUBLANES:(gi + 1) * SUBLANES, :]
            ug = u[gi * SUBLANES:(gi + 1) * SUBLANES, :]
            hg = ug + ag * h
            h = hg[SUBLANES - 1:SUBLANES, :]
            outs.append(hg)
        hblk = jnp.concatenate(outs, axis=0)
        gl = zg_ref[rs, :].astype(F32)
        gelu = 0.5 * gl * (1.0 + jnp.tanh(0.7978845608028654 * (gl + 0.044715 * gl * gl * gl)))
        o_ref[rs, :] = (gelu * hblk).astype(o_ref.dtype)
        return h

    h = lax.fori_loop(0, tt // blk, body, hp_ref[0:1, :])
    hp_ref[...] = jnp.broadcast_to(h, hp_ref.shape)


def _lru_mixer(z, cw, cb, wcat, ba, bx, lam, bsz, seq):
    n = z.shape[0]
    nt = seq // TT_LRU
    ncb = D_LRU // CB_LRU
    row = lambda b, j, t: b * nt + t
    vec = pl.BlockSpec((1, CB_LRU), lambda b, j, t: (0, j))
    return pl.pallas_call(
        _lru_kernel,
        grid=(bsz, ncb, nt),
        in_specs=[
            pl.BlockSpec((TT_LRU, CB_LRU), lambda b, j, t: (row(b, j, t), j)),
            pl.BlockSpec((TT_LRU, CB_LRU), lambda b, j, t: (row(b, j, t), ncb + j)),
            pl.BlockSpec((LRU_CONV_W, CB_LRU), lambda b, j, t: (0, j)),
            vec,
            pl.BlockSpec((1, CB_LRU, 2 * CB_LRU), lambda b, j, t: (j, 0, 0)),
            vec, vec, vec,
        ],
        out_specs=pl.BlockSpec((TT_LRU, CB_LRU), lambda b, j, t: (row(b, j, t), j)),
        out_shape=jax.ShapeDtypeStruct((n, D_LRU), BF16),
        scratch_shapes=[
            pltpu.VMEM((TT_LRU + SUBLANES, CB_LRU), F32),
            pltpu.VMEM((SUBLANES, CB_LRU), F32),
            pltpu.VMEM((TT_LRU, CB_LRU), F32),
            pltpu.VMEM((TT_LRU, 2 * CB_LRU), F32),
        ],
        compiler_params=pltpu.CompilerParams(
            dimension_semantics=("arbitrary", "arbitrary", "arbitrary"),
            vmem_limit_bytes=VMEM_LIMIT),
        name="lru_mixer",
    )(z, z, cw, cb, wcat, ba, bx, lam)


def _conf_kernel(zv_ref, zg_ref, w_ref, cb_ref, lg_ref, lb_ref, wo_ref, o_ref, wob_ref,
                 cs_ref, cv_ref):
    t = pl.program_id(1)
    tt = TT_CONF
    nlb = D_CONV // LANES
    wob_ref[...] = wo_ref[...].astype(BF16)

    @pl.when(t == 0)
    def _():
        cs_ref[:, 0:CONF_HALO, :] = jnp.zeros((nlb, CONF_HALO, LANES), F32)

    @pl.when(t > 0)
    def _():
        cs_ref[:, 0:CONF_HALO, :] = cs_ref[:, tt:tt + CONF_HALO, :]

    for c in range(nlb):
        ls = slice(c * LANES, (c + 1) * LANES)
        v = zv_ref[:, ls].astype(F32)
        g = zg_ref[:, ls].astype(F32)
        cs_ref[c, CONF_HALO:CONF_HALO + tt, :] = v * _sigmoid(g)

    rows = 64
    nrb = tt // rows
    base = CONF_HALO - (CONF_CONV_W - 1)

    def conv_body(c, carry):
        accs = [jnp.broadcast_to(cb_ref[c], (rows, LANES)) for _ in range(nrb)]
        for k in range(CONF_CONV_W):
            wk = w_ref[c, k:k + 1, :]
            for rb in range(nrb):
                off = rb * rows + base + k
                accs[rb] = accs[rb] + wk * cs_ref[c, off:off + rows, :]
        for rb in range(nrb):
            cv_ref[c, rb * rows:(rb + 1) * rows, :] = accs[rb]
        return carry

    lax.fori_loop(0, nlb, conv_body, 0)

    ln_rows = 32
    inv_n = 1.0 / D_CONV
    for rb in range(tt // ln_rows):
        rs = slice(rb * ln_rows, (rb + 1) * ln_rows)
        blk = cv_ref[:, rs, :]
        mu = jnp.sum(jnp.sum(blk, axis=0), axis=-1, keepdims=True) * inv_n
        d = blk - mu[None]
        var = jnp.sum(jnp.sum(d * d, axis=0), axis=-1, keepdims=True) * inv_n
        inv = lax.rsqrt(var + LN_EPS)
        for c in range(nlb):
            ls = slice(c * LANES, (c + 1) * LANES)
            y = d[c] * inv * lg_ref[:, ls] + lb_ref[:, ls]
            o_ref[rs, ls] = (y * _sigmoid(y)).astype(o_ref.dtype)


def _conf_mixer(z, w3, cb3, lg, lb, wo, bsz, seq):
    n = z.shape[0]
    nt = seq // TT_CONF
    nlb = D_CONV // LANES
    wo_rows = wo.shape[0] // (bsz * nt)
    assert wo_rows * bsz * nt == wo.shape[0] and wo_rows % 16 == 0
    return pl.pallas_call(
        _conf_kernel,
        grid=(bsz, nt),
        in_specs=[
            pl.BlockSpec((TT_CONF, D_CONV), lambda b, t: (b * nt + t, 2)),
            pl.BlockSpec((TT_CONF, D_CONV), lambda b, t: (b * nt + t, 3)),
            pl.BlockSpec((nlb, 32, LANES), lambda b, t: (0, 0, 0)),
            pl.BlockSpec((nlb, 1, LANES), lambda b, t: (0, 0, 0)),
            pl.BlockSpec((1, D_CONV), lambda b, t: (0, 0)),
            pl.BlockSpec((1, D_CONV), lambda b, t: (0, 0)),
            pl.BlockSpec((wo_rows, D_MODEL), lambda b, t: (b * nt + t, 0)),
        ],
        out_specs=[
            pl.BlockSpec((TT_CONF, D_CONV), lambda b, t: (b * nt + t, 0)),
            pl.BlockSpec((wo_rows, D_MODEL), lambda b, t: (b * nt + t, 0)),
        ],
        out_shape=[
            jax.ShapeDtypeStruct((n, D_CONV), BF16),
            jax.ShapeDtypeStruct(wo.shape, BF16),
        ],
        scratch_shapes=[
            pltpu.VMEM((nlb, CONF_HALO + TT_CONF, LANES), F32),
            pltpu.VMEM((nlb, TT_CONF, LANES), F32),
        ],
        compiler_params=pltpu.CompilerParams(
            dimension_semantics=("arbitrary", "arbitrary"),
            vmem_limit_bytes=VMEM_LIMIT),
        name="conf_mixer",
    )(z, z, w3, cb3, lg, lb, wo)


def _split_bf16(v):
    hi = v.astype(BF16)
    lo = (v - hi.astype(F32)).astype(BF16)
    return hi, lo


RT = D_MODEL // LANES
PITCH = RT + SUBLANES


def _store_rows(dst_ref, row0, rows, v, stage_ref):
    for s in range(RT):
        stage_ref[pl.ds(s, rows, stride=PITCH), :] = v[:, s * LANES:(s + 1) * LANES]
    staged = stage_ref[0:rows * PITCH, :].reshape(rows, PITCH, LANES)
    dst_ref[row0:row0 + rows] = staged[:, 0:RT, :].astype(BF16)


def _load_rows(src_ref, row0, rows, stage_ref):
    tile = src_ref[row0:row0 + rows].astype(F32)
    tile = jnp.concatenate([tile, jnp.zeros((rows, PITCH - RT, LANES), F32)], axis=1)
    stage_ref[0:rows * PITCH, :] = tile.reshape(rows * PITCH, LANES)
    return [stage_ref[pl.ds(s, rows, stride=PITCH), :] for s in range(RT)]


def _wout_kernel(a_ref, b_ref, h_ref, wa_ref, wb_ref, g1_ref, b1_ref,
                 wr_ref, br_ref, h1_ref, h1r_ref, lg_ref, mixa_ref, mixb_ref, hl_ref, stage_ref):
    half = TM_OUT // 2
    for mix_ref, r0 in ((mixa_ref, 0), (mixb_ref, half)):
        hs = slice(r0, r0 + half)
        mix_ref[...] = (jnp.dot(a_ref[hs, :], wa_ref[...], preferred_element_type=F32)
                        + jnp.dot(b_ref[hs, :], wb_ref[...], preferred_element_type=F32))
    rows = 64
    for c in range(TM_OUT // rows):
        rs = slice(c * rows, (c + 1) * rows)
        mix_ref, r0 = (mixa_ref, 0) if c * rows < half else (mixb_ref, half)
        mix = mix_ref[c * rows - r0:(c + 1) * rows - r0, :]
        h1 = _ln_rows(DN_ALPHA * h_ref[rs, :] + mix, g1_ref[...], b1_ref[...])
        h1_ref[rs, :] = h1
        _store_rows(h1r_ref, c * rows, rows, h1, stage_ref)
        hi, lo = _split_bf16(h1)
        hl_ref[rs, :] = hi
        hl_ref[TM_OUT + c * rows:TM_OUT + (c + 1) * rows, :] = lo
    p = jnp.dot(hl_ref[...], wr_ref[...], preferred_element_type=F32)
    lg_ref[...] = (p[0:TM_OUT, 0:ROUTE_W] + p[0:TM_OUT, ROUTE_W:2 * ROUTE_W]
                   + p[TM_OUT:2 * TM_OUT, 0:ROUTE_W] + br_ref[...])


def _wout_router(a, b, h, wo, g1, b1, wr_cat, br):
    n = h.shape[0]
    assert D_LRU == D_CONV
    full = lambda shape: pl.BlockSpec(shape, lambda i: tuple(0 for _ in shape))
    return pl.pallas_call(
        _wout_kernel,
        grid=(n // TM_OUT,),
        in_specs=[
            pl.BlockSpec((TM_OUT, D_LRU), lambda i: (i, 0)),
            pl.BlockSpec((TM_OUT, D_CONV), lambda i: (i, 0)),
            pl.BlockSpec((TM_OUT, D_MODEL), lambda i: (i, 0)),
            pl.BlockSpec((D_LRU, D_MODEL), lambda i: (0, 0)),
            pl.BlockSpec((D_CONV, D_MODEL), lambda i: (1, 0)),
            full((1, D_MODEL)), full((1, D_MODEL)),
            full((D_MODEL, 2 * ROUTE_W)), full((1, ROUTE_W)),
        ],
        out_specs=[
            pl.BlockSpec((TM_OUT, D_MODEL), lambda i: (i, 0)),
            pl.BlockSpec((TM_OUT, RT, LANES), lambda i: (i, 0, 0)),
            pl.BlockSpec((TM_OUT, ROUTE_W), lambda i: (i, 0)),
        ],
        out_shape=[
            jax.ShapeDtypeStruct((n, D_MODEL), F32),
            jax.ShapeDtypeStruct((n, RT, LANES), BF16),
            jax.ShapeDtypeStruct((n, ROUTE_W), F32),
        ],
        scratch_shapes=[
            pltpu.VMEM((TM_OUT // 2, D_MODEL), F32),
            pltpu.VMEM((TM_OUT // 2, D_MODEL), F32),
            pltpu.VMEM((2 * TM_OUT, D_MODEL), BF16),
            pltpu.VMEM((64 * PITCH, LANES), F32),
        ],
        compiler_params=pltpu.CompilerParams(
            dimension_semantics=("arbitrary",),
            vmem_limit_bytes=VMEM_LIMIT),
        name="wout_router",
    )(a, b, h, wo, wo, g1, b1, wr_cat, br)


def _route_kernel(lg_ref, info_ref, cnt_ref, run_ref, tri_ref):
    t = pl.program_id(0)
    tm = TM_ROUTE
    l = lg_ref[...]
    lane = lax.broadcasted_iota(jnp.int32, (tm, ROUTE_W), 1)
    neg = jnp.float32(-jnp.inf)
    big = jnp.int32(1 << 20)

    gmask = lane < N_GROUPS
    gmax = jnp.max(jnp.where(gmask, l, neg), axis=-1, keepdims=True)
    gsel = jnp.min(jnp.where(gmask & (l == gmax), lane, big), axis=-1, keepdims=True)
    gsum = jnp.sum(jnp.where(gmask, jnp.exp(l - gmax), 0.0), axis=-1, keepdims=True)
    pg_top = 1.0 / gsum

    lo = N_GROUPS + EXPERTS_PER_GROUP * gsel
    emask = (lane >= lo) & (lane < lo + EXPERTS_PER_GROUP)
    v1 = jnp.max(jnp.where(emask, l, neg), axis=-1, keepdims=True)
    i1 = jnp.min(jnp.where(emask & (l == v1), lane, big), axis=-1, keepdims=True)
    emask2 = emask & (lane != i1)
    v2 = jnp.max(jnp.where(emask2, l, neg), axis=-1, keepdims=True)
    i2 = jnp.min(jnp.where(emask2 & (l == v2), lane, big), axis=-1, keepdims=True)
    e21 = jnp.exp(v2 - v1)
    q1 = pg_top / (1.0 + e21)
    q2 = pg_top * e21 / (1.0 + e21)

    oh1 = (lane == i1).astype(F32)
    oh2 = (lane == i2).astype(F32)
    ohs = oh1 + oh2

    @pl.when(t == 0)
    def _():
        run_ref[...] = jnp.zeros_like(run_ref)
        r_i = lax.broadcasted_iota(jnp.int32, (tm, tm), 0)
        c_i = lax.broadcasted_iota(jnp.int32, (tm, tm), 1)
        tri_ref[...] = (c_i < r_i).astype(BF16)

    cum = jnp.dot(tri_ref[...], ohs.astype(BF16), preferred_element_type=F32)
    basev = run_ref[0:1, :] + cum
    r1 = jnp.sum(oh1 * basev, axis=-1, keepdims=True)
    r2 = jnp.sum(oh2 * basev, axis=-1, keepdims=True)
    run_ref[...] = run_ref[...] + jnp.sum(ohs, axis=0, keepdims=True)
    info = jnp.where(lane == 0, r1, 0.0)
    info = jnp.where(lane == 1, r2, info)
    info = jnp.where(lane == 2, q1, info)
    info = jnp.where(lane == 3, q2, info)
    info = jnp.where(lane == 4, (i1 - N_GROUPS).astype(F32), info)
    info = jnp.where(lane == 5, (i2 - N_GROUPS).astype(F32), info)
    info_ref[...] = info
    cnt_ref[...] = run_ref[...]


def _route(logits):
    n = logits.shape[0]
    nt = n // TM_ROUTE
    return pl.pallas_call(
        _route_kernel,
        grid=(nt,),
        in_specs=[pl.BlockSpec((TM_ROUTE, ROUTE_W), lambda t: (t, 0))],
        out_specs=[
            pl.BlockSpec((TM_ROUTE, ROUTE_W), lambda t: (t, 0)),
            pl.BlockSpec((SUBLANES, ROUTE_W), lambda t: (0, 0)),
        ],
        out_shape=[
            jax.ShapeDtypeStruct((n, ROUTE_W), F32),
            jax.ShapeDtypeStruct((SUBLANES, ROUTE_W), F32),
        ],
        scratch_shapes=[
            pltpu.VMEM((SUBLANES, ROUTE_W), F32),
            pltpu.VMEM((TM_ROUTE, TM_ROUTE), BF16),
        ],
        compiler_params=pltpu.CompilerParams(
            dimension_semantics=("arbitrary",),
            vmem_limit_bytes=VMEM_LIMIT),
        name="route",
    )(logits)


def _plan_kernel(cnt_ref, te_ref, nu_ref, nx_ref, sl_ref, st_ref, ps_ref, pl_ref):
    n_tiles = te_ref.shape[0]
    shift = TM_EXP.bit_length() - 1

    def forward(e, carry):
        tile, parity = carry
        c = cnt_ref[N_GROUPS + e]
        tp = (c + (TM_EXP - 1)) >> shift
        st_ref[e] = tile * TM_EXP
        ps_ref[e] = tile * TM_EXP + c
        pl_ref[e] = tp * TM_EXP - c

        def mark(k, _):
            te_ref[tile + k] = e
            real = jnp.minimum(c - k * TM_EXP, TM_EXP)
            pieces = (real + (EXP_PATH_ROWS - 1)) >> (EXP_PATH_ROWS.bit_length() - 1)
            sl_ref[tile + k] = parity + 2 * (pieces - 1)
            return 0

        lax.fori_loop(0, tp, mark, 0)
        return tile + tp, jnp.where(tp > 0, 1 - parity, parity)

    used, _ = lax.fori_loop(0, N_EXPERTS, forward, (jnp.int32(0), jnp.int32(0)))
    nu_ref[0] = used

    def backward(j, nxt):
        e = N_EXPERTS - 1 - j
        c = cnt_ref[N_GROUPS + e]
        tp = (c + (TM_EXP - 1)) >> shift
        first = st_ref[e] >> shift

        def mark(k, _):
            nx_ref[first + k] = nxt
            return 0

        lax.fori_loop(0, tp, mark, 0)
        return jnp.where(tp > 0, e, nxt)

    lax.fori_loop(0, N_EXPERTS, backward, jnp.int32(-1))

    last_e = te_ref[jnp.maximum(used - 1, 0)]
    last_s = sl_ref[jnp.maximum(used - 1, 0)]

    def tail(i, _):
        te_ref[i] = last_e
        sl_ref[i] = last_s
        nx_ref[i] = -1
        return 0

    lax.fori_loop(used, n_tiles, tail, 0)


def _plan(counts_i32, n_tiles):
    smem = lambda: pl.BlockSpec(memory_space=pltpu.SMEM)
    i32 = lambda k: jax.ShapeDtypeStruct((k,), jnp.int32)
    return pl.pallas_call(
        _plan_kernel,
        in_specs=[smem()],
        out_specs=[smem() for _ in range(7)],
        out_shape=[i32(n_tiles), i32(1), i32(n_tiles), i32(n_tiles),
                   i32(N_EXPERTS), i32(N_EXPERTS), i32(N_EXPERTS)],
        name="plan",
    )(counts_i32)


def _prow_copy(src_ref, src_row, dst_ref, dst_row, sem, rows=1):
    return pltpu.make_async_copy(src_ref.at[pl.ds(src_row, rows)], dst_ref.at[pl.ds(dst_row, rows)], sem)


ZERO_ROWS = TM_EXP // 2
DMA_UNROLL = 8


def _pad_fill(e, ps_ref, pl_ref, zbuf_ref, xs_ref, zsem, wait):
    ln = pl_ref[e]
    st = ps_ref[e]
    b = 1
    while b <= ZERO_ROWS:
        @pl.when((ln & b) != 0)
        def _(b=b):
            cp = _prow_copy(zbuf_ref, 0, xs_ref, st + (ln & (b - 1)), zsem, rows=b)
            if wait:
                cp.wait()
            else:
                cp.start()
        b *= 2


def _dispatch_kernel(pos0_ref, pos1_ref, ps_ref, pl_ref, h1p_ref, xs_ref, zbuf_ref, sem, zsem):
    i = pl.program_id(0)
    tm = TM_DISP

    @pl.when(i == 0)
    def _():
        zbuf_ref[...] = jnp.zeros_like(zbuf_ref)

        def fill_start(e, _):
            _pad_fill(e, ps_ref, pl_ref, zbuf_ref, xs_ref, zsem, False)
            return 0

        lax.fori_loop(0, N_EXPERTS, fill_start, 0)

    def issue(k, _):
        for u in range(DMA_UNROLL):
            r = k * DMA_UNROLL + u
            tok = i * tm + r
            _prow_copy(h1p_ref, r, xs_ref, pos0_ref[tok], sem).start(priority=0)
            _prow_copy(h1p_ref, r, xs_ref, pos1_ref[tok], sem).start(priority=1)
        return 0

    lax.fori_loop(0, tm // DMA_UNROLL, issue, 0)

    def drain(k, _):
        for u in range(DMA_UNROLL):
            _prow_copy(h1p_ref, 0, xs_ref, 0, sem).wait()
            _prow_copy(h1p_ref, 0, xs_ref, 0, sem).wait()
        return 0

    lax.fori_loop(0, tm // DMA_UNROLL, drain, 0)

    @pl.when(i == pl.num_programs(0) - 1)
    def _():
        def fill_wait(e, _):
            _pad_fill(e, ps_ref, pl_ref, zbuf_ref, xs_ref, zsem, True)
            return 0

        lax.fori_loop(0, N_EXPERTS, fill_wait, 0)


def _dispatch(pos0, pos1, pad_start, pad_len, h1p, n_rows):
    n = h1p.shape[0]
    return pl.pallas_call(
        _dispatch_kernel,
        grid_spec=pltpu.PrefetchScalarGridSpec(
            num_scalar_prefetch=4,
            grid=(n // TM_DISP,),
            in_specs=[pl.BlockSpec((TM_DISP, RT, LANES), lambda i, *_: (i, 0, 0))],
            out_specs=pl.BlockSpec(memory_space=pl.ANY),
            scratch_shapes=[
                pltpu.VMEM((ZERO_ROWS, RT, LANES), BF16),
                pltpu.SemaphoreType.DMA(()),
                pltpu.SemaphoreType.DMA(()),
            ],
        ),
        out_shape=jax.ShapeDtypeStruct((n_rows, RT, LANES), BF16),
        compiler_params=pltpu.CompilerParams(
            dimension_semantics=("arbitrary",),
            vmem_limit_bytes=VMEM_LIMIT),
        name="dispatch",
    )(pos0, pos1, pad_start, pad_len, h1p)


def _expert_kernel(te_ref, nu_ref, nx_ref, sl_ref, xs_ref, wg_hbm, wu_hbm, wd_hbm, ys_ref,
                   wgf_ref, wuf_ref, wdf_ref, wgb_ref, wub_ref, wdb_ref, stage_ref, wsem):
    i = pl.program_id(0)
    used = i < nu_ref[0]
    e = te_ref[i]
    s = sl_ref[i] & 1
    pieces = (sl_ref[i] >> 1) + 1
    fresh = (i == 0) | (e != te_ref[jnp.maximum(i - 1, 0)])

    def weight_copies(expert, slot):
        return (pltpu.make_async_copy(wg_hbm.at[expert], wgf_ref.at[slot], wsem.at[slot, 0]),
                pltpu.make_async_copy(wu_hbm.at[expert], wuf_ref.at[slot], wsem.at[slot, 1]),
                pltpu.make_async_copy(wd_hbm.at[expert], wdf_ref.at[slot], wsem.at[slot, 2]))

    @pl.when(i == 0)
    def _():
        for cp in weight_copies(e, s):
            cp.start(priority=1)

    @pl.when(used & fresh)
    def _():
        for cp in weight_copies(e, s):
            cp.wait()

        @pl.when(nx_ref[i] >= 0)
        def _():
            for cp in weight_copies(nx_ref[i], 1 - s):
                cp.start(priority=1)

        wgb_ref[...] = wgf_ref[s].astype(BF16)
        wub_ref[...] = wuf_ref[s].astype(BF16)
        wdb_ref[...] = wdf_ref[s].astype(BF16)

    def ffn(rows):
        x = jnp.concatenate([p.astype(BF16) for p in _load_rows(xs_ref, 0, rows, stage_ref)],
                            axis=1)
        g = jnp.dot(x, wgb_ref[...], preferred_element_type=F32)
        u = jnp.dot(x, wub_ref[...], preferred_element_type=F32)
        h = (g * _sigmoid(g) * u).astype(BF16)
        y = jnp.dot(h, wdb_ref[...], preferred_element_type=F32)
        _store_rows(ys_ref, 0, rows, y, stage_ref)

    for p in range(1, TM_EXP // EXP_PATH_ROWS + 1):
        @pl.when(used & (pieces == p))
        def _(p=p):
            ffn(p * EXP_PATH_ROWS)


def _expert_ffn(tile_expert, n_used, next_expert, slot, xs, wg, wu, wd):
    n_rows = xs.shape[0]
    n_tiles = n_rows // TM_EXP

    def row_map(i, te, nu, nx, sl):
        return (jnp.minimum(i, nu[0] - 1), 0, 0)

    return pl.pallas_call(
        _expert_kernel,
        grid_spec=pltpu.PrefetchScalarGridSpec(
            num_scalar_prefetch=4,
            grid=(n_tiles,),
            in_specs=[
                pl.BlockSpec((TM_EXP, RT, LANES), row_map),
                pl.BlockSpec(memory_space=pl.ANY),
                pl.BlockSpec(memory_space=pl.ANY),
                pl.BlockSpec(memory_space=pl.ANY),
            ],
            out_specs=pl.BlockSpec((TM_EXP, RT, LANES), row_map),
            scratch_shapes=[
                pltpu.VMEM((2, D_MODEL, D_EXPERT), F32),
                pltpu.VMEM((2, D_MODEL, D_EXPERT), F32),
                pltpu.VMEM((2, D_EXPERT, D_MODEL), F32),
                pltpu.VMEM((D_MODEL, D_EXPERT), BF16),
                pltpu.VMEM((D_MODEL, D_EXPERT), BF16),
                pltpu.VMEM((D_EXPERT, D_MODEL), BF16),
                pltpu.VMEM((TM_EXP * PITCH, LANES), F32),
                pltpu.SemaphoreType.DMA((2, 3)),
            ],
        ),
        out_shape=jax.ShapeDtypeStruct((n_rows, RT, LANES), BF16),
        compiler_params=pltpu.CompilerParams(
            dimension_semantics=("arbitrary",),
            vmem_limit_bytes=VMEM_LIMIT),
        name="expert_ffn",
    )(tile_expert, n_used, next_expert, slot, xs, wg, wu, wd)


def _combine_kernel(pos0_ref, pos1_ref, h1_ref, info_ref, g2_ref, b2_ref, ys_ref, o_ref,
                    ybuf_ref, stage_ref, sem):
    i = pl.program_id(0)
    tm = TM_COMB
    par = lax.rem(i, 2)

    def issue(step, parity):
        def body(k, _):
            for u in range(DMA_UNROLL):
                r = k * DMA_UNROLL + u
                tok = step * tm + r
                _prow_copy(ys_ref, pos0_ref[tok], ybuf_ref.at[parity, 0], r,
                           sem.at[parity]).start(priority=0)
                _prow_copy(ys_ref, pos1_ref[tok], ybuf_ref.at[parity, 1], r,
                           sem.at[parity]).start(priority=1)
            return 0
        lax.fori_loop(0, tm // DMA_UNROLL, body, 0)

    def drain(parity):
        def body(k, _):
            for u in range(DMA_UNROLL):
                _prow_copy(ys_ref, 0, ybuf_ref.at[parity, 0], 0, sem.at[parity]).wait()
                _prow_copy(ys_ref, 0, ybuf_ref.at[parity, 1], 0, sem.at[parity]).wait()
            return 0
        lax.fori_loop(0, tm // DMA_UNROLL, body, 0)

    @pl.when(i == 0)
    def _():
        issue(0, 0)

    @pl.when(i + 1 < pl.num_programs(0))
    def _():
        issue(i + 1, 1 - par)

    drain(par)

    rows = 64
    for c in range(tm // rows):
        rs = slice(c * rows, (c + 1) * rows)
        q1 = info_ref[rs, 2:3]
        q2 = info_ref[rs, 3:4]
        y0 = _load_rows(ybuf_ref.at[par, 0], c * rows, rows, stage_ref)
        y1 = _load_rows(ybuf_ref.at[par, 1], c * rows, rows, stage_ref)
        ffn = jnp.concatenate([q1 * a + q2 * b for a, b in zip(y0, y1)], axis=1)
        o_ref[rs, :] = _ln_rows(DN_ALPHA * h1_ref[rs, :] + ffn, g2_ref[...], b2_ref[...])


def _combine(pos0, pos1, h1, info, g2, b2, ys):
    n = h1.shape[0]
    return pl.pallas_call(
        _combine_kernel,
        grid_spec=pltpu.PrefetchScalarGridSpec(
            num_scalar_prefetch=2,
            grid=(n // TM_COMB,),
            in_specs=[
                pl.BlockSpec((TM_COMB, D_MODEL), lambda i, *_: (i, 0)),
                pl.BlockSpec((TM_COMB, ROUTE_W), lambda i, *_: (i, 0)),
                pl.BlockSpec((1, D_MODEL), lambda i, *_: (0, 0)),
                pl.BlockSpec((1, D_MODEL), lambda i, *_: (0, 0)),
                pl.BlockSpec(memory_space=pl.ANY),
            ],
            out_specs=pl.BlockSpec((TM_COMB, D_MODEL), lambda i, *_: (i, 0)),
            scratch_shapes=[
                pltpu.VMEM((2, 2, TM_COMB, RT, LANES), BF16),
                pltpu.VMEM((64 * PITCH, LANES), F32),
                pltpu.SemaphoreType.DMA((2,)),
            ],
        ),
        out_shape=jax.ShapeDtypeStruct((n, D_MODEL), F32),
        compiler_params=pltpu.CompilerParams(
            dimension_semantics=("arbitrary",),
            vmem_limit_bytes=VMEM_LIMIT),
        name="combine",
    )(pos0, pos1, h1, info, g2, b2, ys)


def _block_diag(w, per):
    h, hd, _ = w.shape
    wg = w.reshape(h // per, per, hd, hd)
    eye = jnp.eye(per, dtype=w.dtype)
    return jnp.einsum("gpij,pq->gpiqj", wg, eye).reshape(h // per, per * hd, per * hd)


def kernel(x, ln_in_g, ln_in_b, w_in, lru_conv_w, lru_conv_b, lru_w_a, lru_b_a, lru_w_x, lru_b_x,
           lru_lambda, conf_conv_w, conf_conv_b, conf_ln_g, conf_ln_b, w_out, ln1_g, ln1_b,
           router_group_w, router_group_b, router_expert_w, router_expert_b, exp_w_gate, exp_w_up,
           exp_w_down, ln2_g, ln2_b):
    bsz, seq, d = x.shape
    n = bsz * seq
    x2 = x.reshape(n, d)
    row = lambda v: v.reshape(1, -1).astype(F32)
    l = 0

    z, h0 = _ln_win(x2, row(ln_in_g), row(ln_in_b), w_in[l])

    per = CB_LRU // LRU_HEAD_DIM
    wcat = jnp.concatenate([_block_diag(lru_w_a[l], per), _block_diag(lru_w_x[l], per)],
                           axis=-1).astype(BF16)
    a_out = _lru_mixer(z, lru_conv_w[l], row(lru_conv_b[l]), wcat, row(lru_b_a[l]),
                       row(lru_b_x[l]), row(lru_lambda[l]), bsz, seq)

    nlb = D_CONV // LANES
    w3 = jnp.pad(conf_conv_w[l], ((0, 32 - CONF_CONV_W), (0, 0)))
    w3 = w3.reshape(32, nlb, LANES).transpose(1, 0, 2)
    cb3 = conf_conv_b[l].reshape(nlb, 1, LANES)
    b_out, wo = _conf_mixer(z, w3, cb3, row(conf_ln_g[l]), row(conf_ln_b[l]), w_out[l], bsz, seq)

    wr = jnp.concatenate([router_group_w[l], router_expert_w[l]], axis=1)
    wr = jnp.pad(wr, ((0, 0), (0, ROUTE_W - wr.shape[1])))
    wr_hi = wr.astype(BF16)
    wr_lo = (wr - wr_hi.astype(F32)).astype(BF16)
    wr_cat = jnp.concatenate([wr_hi, wr_lo], axis=1)
    br = jnp.concatenate([router_group_b[l], router_expert_b[l]])
    br = jnp.pad(br, (0, ROUTE_W - br.shape[0])).reshape(1, ROUTE_W)
    h1, h1p, logits = _wout_router(a_out, b_out, h0, wo, row(ln1_g[l]), row(ln1_b[l]), wr_cat, br)

    info, counts = _route(logits)
    idx = info[:, 0:6].astype(jnp.int32)
    r0, r1, e0, e1 = idx[:, 0], idx[:, 1], idx[:, 4], idx[:, 5]

    n_tiles = (n * 2) // TM_EXP + N_EXPERTS
    tile_expert, n_used, next_expert, slot, starts, pad_start, pad_len = _plan(
        counts[0].astype(jnp.int32), n_tiles)

    expert_ids = jnp.arange(N_EXPERTS, dtype=jnp.int32)[None, :]
    start_of = lambda e: jnp.sum(jnp.where(e[:, None] == expert_ids, starts[None, :], 0), axis=1)
    pos0 = (start_of(e0) + r0).astype(jnp.int32)
    pos1 = (start_of(e1) + r1).astype(jnp.int32)

    xs = _dispatch(pos0, pos1, pad_start, pad_len, h1p, n_tiles * TM_EXP)
    shp = (N_EXPERTS, D_MODEL, D_EXPERT)
    ys = _expert_ffn(tile_expert, n_used, next_expert, slot, xs, exp_w_gate[l].reshape(shp),
                     exp_w_up[l].reshape(shp), exp_w_down[l].reshape(N_EXPERTS, D_EXPERT, D_MODEL))
    out = _combine(pos0, pos1, h1, info, row(ln2_g[l]), row(ln2_b[l]), ys)
    return out.reshape(bsz, seq, d)
```

```python
import jax
import jax.numpy as jnp
from jax import lax
from jax.experimental import pallas as pl
from jax.experimental.pallas import tpu as pltpu

F32 = jnp.float32
BF16 = jnp.bfloat16

D_MODEL = 2048
D_LRU = 1024
D_CONV = 1024
LRU_HEADS = 16
LRU_HEAD_DIM = 64
LRU_C = 8.0
LRU_CONV_W = 4
CONF_CONV_W = 31
N_GROUPS = 4
EXPERTS_PER_GROUP = 8
N_EXPERTS = N_GROUPS * EXPERTS_PER_GROUP
D_EXPERT = 512
LN_EPS = 1e-5
DEPTH = 1
DN_ALPHA = (2 * DEPTH) ** 0.25

LANES = 128
SUBLANES = 8
VMEM_LIMIT = 56 * 1024 * 1024

TM_WIN = 512
TN_WIN = 1024
W_CHUNK = 256
TT_LRU = 4096
CB_LRU = 256
TT_CONF = 256
CONF_HALO = 32
TM_OUT = 512
TM_ROUTE = 1024
TM_DISP = 1024
TM_EXP = 256
EXP_PATH_ROWS = 128
TM_COMB = 256
ROUTE_W = 128


def _sigmoid(x):
    return 0.5 * (jnp.tanh(0.5 * x) + 1.0)


def _ln_rows(x, g, b):
    mu = jnp.mean(x, axis=-1, keepdims=True)
    xc = x - mu
    var = jnp.mean(xc * xc, axis=-1, keepdims=True)
    return xc * lax.rsqrt(var + LN_EPS) * g + b


def _stage_weight(w_hbm, wb_ref, wst_ref, wsem):
    nchunk = wb_ref.shape[1] // W_CHUNK

    def chunk_copy(c):
        return pltpu.make_async_copy(w_hbm.at[:, pl.ds(c * W_CHUNK, W_CHUNK)],
                                     wst_ref.at[c % 2], wsem.at[c % 2])

    chunk_copy(0).start()
    for c in range(nchunk):
        if c + 1 < nchunk:
            chunk_copy(c + 1).start()
        chunk_copy(c).wait()
        wb_ref[:, c * W_CHUNK:(c + 1) * W_CHUNK] = wst_ref[c % 2].astype(BF16)


def _ln_win_kernel(x_ref, g_ref, b_ref, w_hbm, z_ref, h_ref, wb_ref, wst_ref, xn_ref, wsem):
    s = pl.program_id(0)
    par = lax.rem(s, 2)

    @pl.when(s == 0)
    def _():
        xn_ref[1] = jnp.zeros(xn_ref.shape[1:], BF16)
        _stage_weight(w_hbm, wb_ref, wst_ref, wsem)

    rows = 128
    for c in range(TM_WIN // rows):
        rs = slice(c * rows, (c + 1) * rows)
        hn = _ln_rows(x_ref[rs, :], g_ref[...], b_ref[...])
        h_ref[rs, :] = hn
        xn_ref[par, rs, :] = hn.astype(BF16)

    xprev = xn_ref[1 - par]
    for c in range(z_ref.shape[1] // TN_WIN):
        cs = slice(c * TN_WIN, (c + 1) * TN_WIN)
        z_ref[:, cs] = jnp.dot(xprev, wb_ref[:, cs], preferred_element_type=F32).astype(z_ref.dtype)


def _ln_win(x2, g, b, w):
    n = x2.shape[0]
    ncol = w.shape[1]
    nt = n // TM_WIN
    return pl.pallas_call(
        _ln_win_kernel,
        grid=(nt + 1,),
        in_specs=[
            pl.BlockSpec((TM_WIN, D_MODEL), lambda s: (jnp.minimum(s, nt - 1), 0)),
            pl.BlockSpec((1, D_MODEL), lambda s: (0, 0)),
            pl.BlockSpec((1, D_MODEL), lambda s: (0, 0)),
            pl.BlockSpec(memory_space=pl.ANY),
        ],
        out_specs=[
            pl.BlockSpec((TM_WIN, ncol), lambda s: (jnp.maximum(s - 1, 0), 0)),
            pl.BlockSpec((TM_WIN, D_MODEL), lambda s: (jnp.minimum(s, nt - 1), 0)),
        ],
        out_shape=[
            jax.ShapeDtypeStruct((n, ncol), BF16),
            jax.ShapeDtypeStruct((n, D_MODEL), F32),
        ],
        scratch_shapes=[
            pltpu.VMEM((D_MODEL, ncol), BF16),
            pltpu.VMEM((2, D_MODEL, W_CHUNK), F32),
            pltpu.VMEM((2, TM_WIN, D_MODEL), BF16),
            pltpu.SemaphoreType.DMA((2,)),
        ],
        compiler_params=pltpu.CompilerParams(
            dimension_semantics=("arbitrary",),
            vmem_limit_bytes=VMEM_LIMIT),
        name="ln_win",
    )(x2, g, b, w)


def _lru_kernel(zx_ref, zg_ref, cw_ref, cb_ref, wcat_ref, ba_ref, bx_ref, lam_ref,
                o_ref, xs_ref, hp_ref, a_ref, g_ref):
    t = pl.program_id(2)
    tt = TT_LRU

    @pl.when(t == 0)
    def _():
        xs_ref[0:SUBLANES, :] = jnp.zeros((SUBLANES, CB_LRU), F32)
        hp_ref[...] = jnp.zeros_like(hp_ref)

    @pl.when(t > 0)
    def _():
        xs_ref[0:SUBLANES, :] = xs_ref[tt:tt + SUBLANES, :]

    xs_ref[SUBLANES:SUBLANES + tt, :] = zx_ref[...].astype(F32)

    rows = 128
    for rb in range(tt // rows):
        acc = jnp.broadcast_to(cb_ref[...], (rows, CB_LRU))
        for k in range(LRU_CONV_W):
            off = rb * rows + SUBLANES - (LRU_CONV_W - 1) + k
            acc = acc + cw_ref[k:k + 1, :] * xs_ref[off:off + rows, :]
        a_ref[rb * rows:(rb + 1) * rows, :] = acc

    g_ref[...] = jnp.dot(a_ref[...].astype(BF16), wcat_ref[0], preferred_element_type=F32)

    lam = lam_ref[...]
    softplus_neg = jnp.maximum(-lam, 0.0) + jnp.log1p(jnp.exp(-jnp.abs(lam)))
    cvec = -LRU_C * softplus_neg
    ba = ba_ref[...]
    bx = bx_ref[...]
    blk = 64
    row_in_vreg = lax.broadcasted_iota(jnp.int32, (blk, CB_LRU), 0) & (SUBLANES - 1)

    def body(rb, h):
        rs = pl.ds(pl.multiple_of(rb * blk, blk), blk)
        a_in = a_ref[rs, :]
        r = _sigmoid(g_ref[rs, 0:CB_LRU] + ba)
        i = _sigmoid(g_ref[rs, CB_LRU:2 * CB_LRU] + bx)
        log_a = cvec * r
        a = jnp.exp(log_a)
        m2 = -jnp.tanh(log_a) * (a * a + 1.0)
        u = jnp.where(m2 > 0.0, m2 * lax.rsqrt(m2), 0.0) * (i * a_in)
        for s in (1, 2, 4):
            m = row_in_vreg >= s
            a_sh = jnp.where(m, pltpu.roll(a, s, 0), 1.0)
            u_sh = jnp.where(m, pltpu.roll(u, s, 0), 0.0)
            u = u + a * u_sh
            a = a * a_sh
        outs = []
        for gi in range(blk // SUBLANES):
            ag = a[gi * SUBLANES:(gi + 1) * SUBLANES, :]
            ug = u[gi * SUBLANES:(gi + 1) * SUBLANES, :]
            hg = ug + ag * h
            h = hg[SUBLANES - 1:SUBLANES, :]
            outs.append(hg)
        hblk = jnp.concatenate(outs, axis=0)
        gl = zg_ref[rs, :].astype(F32)
        gelu = 0.5 * gl * (1.0 + jnp.tanh(0.7978845608028654 * (gl + 0.044715 * gl * gl * gl)))
        o_ref[rs, :] = (gelu * hblk).astype(o_ref.dtype)
        return h

    h = lax.fori_loop(0, tt // blk, body, hp_ref[0:1, :])
    hp_ref[...] = jnp.broadcast_to(h, hp_ref.shape)


def _lru_mixer(z, cw, cb, wcat, ba, bx, lam, bsz, seq):
    n = z.shape[0]
    nt = seq // TT_LRU
    ncb = D_LRU // CB_LRU
    row = lambda b, j, t: b * nt + t
    vec = pl.BlockSpec((1, CB_LRU), lambda b, j, t: (0, j))
    return pl.pallas_call(
        _lru_kernel,
        grid=(bsz, ncb, nt),
        in_specs=[
            pl.BlockSpec((TT_LRU, CB_LRU), lambda b, j, t: (row(b, j, t), j)),
            pl.BlockSpec((TT_LRU, CB_LRU), lambda b, j, t: (row(b, j, t), ncb + j)),
            pl.BlockSpec((LRU_CONV_W, CB_LRU), lambda b, j, t: (0, j)),
            vec,
            pl.BlockSpec((1, CB_LRU, 2 * CB_LRU), lambda b, j, t: (j, 0, 0)),
            vec, vec, vec,
        ],
        out_specs=pl.BlockSpec((TT_LRU, CB_LRU), lambda b, j, t: (row(b, j, t), j)),
        out_shape=jax.ShapeDtypeStruct((n, D_LRU), BF16),
        scratch_shapes=[
            pltpu.VMEM((TT_LRU + SUBLANES, CB_LRU), F32),
            pltpu.VMEM((SUBLANES, CB_LRU), F32),
            pltpu.VMEM((TT_LRU, CB_LRU), F32),
            pltpu.VMEM((TT_LRU, 2 * CB_LRU), F32),
        ],
        compiler_params=pltpu.CompilerParams(
            dimension_semantics=("arbitrary", "arbitrary", "arbitrary"),
            vmem_limit_bytes=VMEM_LIMIT),
        name="lru_mixer",
    )(z, z, cw, cb, wcat, ba, bx, lam)


def _conf_kernel(zv_ref, zg_ref, w_ref, cb_ref, lg_ref, lb_ref, wo_ref, o_ref, wob_ref,
                 cs_ref, cv_ref):
    t = pl.program_id(1)
    tt = TT_CONF
    nlb = D_CONV // LANES
    wob_ref[...] = wo_ref[...].astype(BF16)

    @pl.when(t == 0)
    def _():
        cs_ref[:, 0:CONF_HALO, :] = jnp.zeros((nlb, CONF_HALO, LANES), F32)

    @pl.when(t > 0)
    def _():
        cs_ref[:, 0:CONF_HALO, :] = cs_ref[:, tt:tt + CONF_HALO, :]

    for c in range(nlb):
        ls = slice(c * LANES, (c + 1) * LANES)
        v = zv_ref[:, ls].astype(F32)
        g = zg_ref[:, ls].astype(F32)
        cs_ref[c, CONF_HALO:CONF_HALO + tt, :] = v * _sigmoid(g)

    rows = 64
    nrb = tt // rows
    base = CONF_HALO - (CONF_CONV_W - 1)

    def conv_body(c, carry):
        accs = [jnp.broadcast_to(cb_ref[c], (rows, LANES)) for _ in range(nrb)]
        for k in range(CONF_CONV_W):
            wk = w_ref[c, k:k + 1, :]
            for rb in range(nrb):
                off = rb * rows + base + k
                accs[rb] = accs[rb] + wk * cs_ref[c, off:off + rows, :]
        for rb in range(nrb):
            cv_ref[c, rb * rows:(rb + 1) * rows, :] = accs[rb]
        return carry

    lax.fori_loop(0, nlb, conv_body, 0)

    ln_rows = 32
    inv_n = 1.0 / D_CONV
    for rb in range(tt // ln_rows):
        rs = slice(rb * ln_rows, (rb + 1) * ln_rows)
        blk = cv_ref[:, rs, :]
        mu = jnp.sum(jnp.sum(blk, axis=0), axis=-1, keepdims=True) * inv_n
        d = blk - mu[None]
        var = jnp.sum(jnp.sum(d * d, axis=0), axis=-1, keepdims=True) * inv_n
        inv = lax.rsqrt(var + LN_EPS)
        for c in range(nlb):
            ls = slice(c * LANES, (c + 1) * LANES)
            y = d[c] * inv * lg_ref[:, ls] + lb_ref[:, ls]
            o_ref[rs, ls] = (y * _sigmoid(y)).astype(o_ref.dtype)


def _conf_mixer(z, w3, cb3, lg, lb, wo, bsz, seq):
    n = z.shape[0]
    nt = seq // TT_CONF
    nlb = D_CONV // LANES
    wo_rows = wo.shape[0] // (bsz * nt)
    assert wo_rows * bsz * nt == wo.shape[0] and wo_rows % 16 == 0
    return pl.pallas_call(
        _conf_kernel,
        grid=(bsz, nt),
        in_specs=[
            pl.BlockSpec((TT_CONF, D_CONV), lambda b, t: (b * nt + t, 2)),
            pl.BlockSpec((TT_CONF, D_CONV), lambda b, t: (b * nt + t, 3)),
            pl.BlockSpec((nlb, 32, LANES), lambda b, t: (0, 0, 0)),
            pl.BlockSpec((nlb, 1, LANES), lambda b, t: (0, 0, 0)),
            pl.BlockSpec((1, D_CONV), lambda b, t: (0, 0)),
            pl.BlockSpec((1, D_CONV), lambda b, t: (0, 0)),
            pl.BlockSpec((wo_rows, D_MODEL), lambda b, t: (b * nt + t, 0)),
        ],
        out_specs=[
            pl.BlockSpec((TT_CONF, D_CONV), lambda b, t: (b * nt + t, 0)),
            pl.BlockSpec((wo_rows, D_MODEL), lambda b, t: (b * nt + t, 0)),
        ],
        out_shape=[
            jax.ShapeDtypeStruct((n, D_CONV), BF16),
            jax.ShapeDtypeStruct(wo.shape, BF16),
        ],
        scratch_shapes=[
            pltpu.VMEM((nlb, CONF_HALO + TT_CONF, LANES), F32),
            pltpu.VMEM((nlb, TT_CONF, LANES), F32),
        ],
        compiler_params=pltpu.CompilerParams(
            dimension_semantics=("arbitrary", "arbitrary"),
            vmem_limit_bytes=VMEM_LIMIT),
        name="conf_mixer",
    )(z, z, w3, cb3, lg, lb, wo)


def _split_bf16(v):
    hi = v.astype(BF16)
    lo = (v - hi.astype(F32)).astype(BF16)
    return hi, lo


RT = D_MODEL // LANES
PITCH = RT + SUBLANES


def _store_rows(dst_ref, row0, rows, v, stage_ref):
    for s in range(RT):
        stage_ref[pl.ds(s, rows, stride=PITCH), :] = v[:, s * LANES:(s + 1) * LANES]
    staged = stage_ref[0:rows * PITCH, :].reshape(rows, PITCH, LANES)
    dst_ref[row0:row0 + rows] = staged[:, 0:RT, :].astype(BF16)


def _load_rows(src_ref, row0, rows, stage_ref):
    tile = src_ref[row0:row0 + rows].astype(F32)
    tile = jnp.concatenate([tile, jnp.zeros((rows, PITCH - RT, LANES), F32)], axis=1)
    stage_ref[0:rows * PITCH, :] = tile.reshape(rows * PITCH, LANES)
    return [stage_ref[pl.ds(s, rows, stride=PITCH), :] for s in range(RT)]


def _wout_kernel(a_ref, b_ref, h_ref, wa_ref, wb_ref, g1_ref, b1_ref,
                 wr_ref, br_ref, h1_ref, h1r_ref, lg_ref, mixa_ref, mixb_ref, hl_ref, stage_ref):
    half = TM_OUT // 2
    for mix_ref, r0 in ((mixa_ref, 0), (mixb_ref, half)):
        hs = slice(r0, r0 + half)
        mix_ref[...] = (jnp.dot(a_ref[hs, :], wa_ref[...], preferred_element_type=F32)
                        + jnp.dot(b_ref[hs, :], wb_ref[...], preferred_element_type=F32))
    rows = 64
    for c in range(TM_OUT // rows):
        rs = slice(c * rows, (c + 1) * rows)
        mix_ref, r0 = (mixa_ref, 0) if c * rows < half else (mixb_ref, half)
        mix = mix_ref[c * rows - r0:(c + 1) * rows - r0, :]
        h1 = _ln_rows(DN_ALPHA * h_ref[rs, :] + mix, g1_ref[...], b1_ref[...])
        h1_ref[rs, :] = h1
        _store_rows(h1r_ref, c * rows, rows, h1, stage_ref)
        hi, lo = _split_bf16(h1)
        hl_ref[rs, :] = hi
        hl_ref[TM_OUT + c * rows:TM_OUT + (c + 1) * rows, :] = lo
    p = jnp.dot(hl_ref[...], wr_ref[...], preferred_element_type=F32)
    lg_ref[...] = (p[0:TM_OUT, 0:ROUTE_W] + p[0:TM_OUT, ROUTE_W:2 * ROUTE_W]
                   + p[TM_OUT:2 * TM_OUT, 0:ROUTE_W] + br_ref[...])


def _wout_router(a, b, h, wo, g1, b1, wr_cat, br):
    n = h.shape[0]
    assert D_LRU == D_CONV
    full = lambda shape: pl.BlockSpec(shape, lambda i: tuple(0 for _ in shape))
    return pl.pallas_call(
        _wout_kernel,
        grid=(n // TM_OUT,),
        in_specs=[
            pl.BlockSpec((TM_OUT, D_LRU), lambda i: (i, 0)),
            pl.BlockSpec((TM_OUT, D_CONV), lambda i: (i, 0)),
            pl.BlockSpec((TM_OUT, D_MODEL), lambda i: (i, 0)),
            pl.BlockSpec((D_LRU, D_MODEL), lambda i: (0, 0)),
            pl.BlockSpec((D_CONV, D_MODEL), lambda i: (1, 0)),
            full((1, D_MODEL)), full((1, D_MODEL)),
            full((D_MODEL, 2 * ROUTE_W)), full((1, ROUTE_W)),
        ],
        out_specs=[
            pl.BlockSpec((TM_OUT, D_MODEL), lambda i: (i, 0)),
            pl.BlockSpec((TM_OUT, RT, LANES), lambda i: (i, 0, 0)),
            pl.BlockSpec((TM_OUT, ROUTE_W), lambda i: (i, 0)),
        ],
        out_shape=[
            jax.ShapeDtypeStruct((n, D_MODEL), F32),
            jax.ShapeDtypeStruct((n, RT, LANES), BF16),
            jax.ShapeDtypeStruct((n, ROUTE_W), F32),
        ],
        scratch_shapes=[
            pltpu.VMEM((TM_OUT // 2, D_MODEL), F32),
            pltpu.VMEM((TM_OUT // 2, D_MODEL), F32),
            pltpu.VMEM((2 * TM_OUT, D_MODEL), BF16),
            pltpu.VMEM((64 * PITCH, LANES), F32),
        ],
        compiler_params=pltpu.CompilerParams(
            dimension_semantics=("arbitrary",),
            vmem_limit_bytes=VMEM_LIMIT),
        name="wout_router",
    )(a, b, h, wo, wo, g1, b1, wr_cat, br)


def _route_kernel(lg_ref, info_ref, cnt_ref, run_ref, tri_ref):
    t = pl.program_id(0)
    tm = TM_ROUTE
    l = lg_ref[...]
    lane = lax.broadcasted_iota(jnp.int32, (tm, ROUTE_W), 1)
    neg = jnp.float32(-jnp.inf)
    big = jnp.int32(1 << 20)

    gmask = lane < N_GROUPS
    gmax = jnp.max(jnp.where(gmask, l, neg), axis=-1, keepdims=True)
    gsel = jnp.min(jnp.where(gmask & (l == gmax), lane, big), axis=-1, keepdims=True)
    gsum = jnp.sum(jnp.where(gmask, jnp.exp(l - gmax), 0.0), axis=-1, keepdims=True)
    pg_top = 1.0 / gsum

    lo = N_GROUPS + EXPERTS_PER_GROUP * gsel
    emask = (lane >= lo) & (lane < lo + EXPERTS_PER_GROUP)
    v1 = jnp.max(jnp.where(emask, l, neg), axis=-1, keepdims=True)
    i1 = jnp.min(jnp.where(emask & (l == v1), lane, big), axis=-1, keepdims=True)
    emask2 = emask & (lane != i1)
    v2 = jnp.max(jnp.where(emask2, l, neg), axis=-1, keepdims=True)
    i2 = jnp.min(jnp.where(emask2 & (l == v2), lane, big), axis=-1, keepdims=True)
    e21 = jnp.exp(v2 - v1)
    q1 = pg_top / (1.0 + e21)
    q2 = pg_top * e21 / (1.0 + e21)

    oh1 = (lane == i1).astype(F32)
    oh2 = (lane == i2).astype(F32)
    ohs = oh1 + oh2

    @pl.when(t == 0)
    def _():
        run_ref[...] = jnp.zeros_like(run_ref)
        r_i = lax.broadcasted_iota(jnp.int32, (tm, tm), 0)
        c_i = lax.broadcasted_iota(jnp.int32, (tm, tm), 1)
        tri_ref[...] = (c_i < r_i).astype(BF16)

    cum = jnp.dot(tri_ref[...], ohs.astype(BF16), preferred_element_type=F32)
    basev = run_ref[0:1, :] + cum
    r1 = jnp.sum(oh1 * basev, axis=-1, keepdims=True)
    r2 = jnp.sum(oh2 * basev, axis=-1, keepdims=True)
    run_ref[...] = run_ref[...] + jnp.sum(ohs, axis=0, keepdims=True)
    info = jnp.where(lane == 0, r1, 0.0)
    info = jnp.where(lane == 1, r2, info)
    info = jnp.where(lane == 2, q1, info)
    info = jnp.where(lane == 3, q2, info)
    info = jnp.where(lane == 4, (i1 - N_GROUPS).astype(F32), info)
    info = jnp.where(lane == 5, (i2 - N_GROUPS).astype(F32), info)
    info_ref[...] = info
    cnt_ref[...] = run_ref[...]


def _route(logits):
    n = logits.shape[0]
    nt = n // TM_ROUTE
    return pl.pallas_call(
        _route_kernel,
        grid=(nt,),
        in_specs=[pl.BlockSpec((TM_ROUTE, ROUTE_W), lambda t: (t, 0))],
        out_specs=[
            pl.BlockSpec((TM_ROUTE, ROUTE_W), lambda t: (t, 0)),
            pl.BlockSpec((SUBLANES, ROUTE_W), lambda t: (0, 0)),
        ],
        out_shape=[
            jax.ShapeDtypeStruct((n, ROUTE_W), F32),
            jax.ShapeDtypeStruct((SUBLANES, ROUTE_W), F32),
        ],
        scratch_shapes=[
            pltpu.VMEM((SUBLANES, ROUTE_W), F32),
            pltpu.VMEM((TM_ROUTE, TM_ROUTE), BF16),
        ],
        compiler_params=pltpu.CompilerParams(
            dimension_semantics=("arbitrary",),
            vmem_limit_bytes=VMEM_LIMIT),
        name="route",
    )(logits)


def _plan_kernel(cnt_ref, te_ref, nu_ref, nx_ref, sl_ref, st_ref, ps_ref, pl_ref):
    n_tiles = te_ref.shape[0]
    shift = TM_EXP.bit_length() - 1

    def forward(e, carry):
        tile, parity = carry
        c = cnt_ref[N_GROUPS + e]
        tp = (c + (TM_EXP - 1)) >> shift
        st_ref[e] = tile * TM_EXP
        ps_ref[e] = tile * TM_EXP + c
        pl_ref[e] = tp * TM_EXP - c

        def mark(k, _):
            te_ref[tile + k] = e
            real = jnp.minimum(c - k * TM_EXP, TM_EXP)
            pieces = (real + (EXP_PATH_ROWS - 1)) >> (EXP_PATH_ROWS.bit_length() - 1)
            sl_ref[tile + k] = parity + 2 * (pieces - 1)
            return 0

        lax.fori_loop(0, tp, mark, 0)
        return tile + tp, jnp.where(tp > 0, 1 - parity, parity)

    used, _ = lax.fori_loop(0, N_EXPERTS, forward, (jnp.int32(0), jnp.int32(0)))
    nu_ref[0] = used

    def backward(j, nxt):
        e = N_EXPERTS - 1 - j
        c = cnt_ref[N_GROUPS + e]
        tp = (c + (TM_EXP - 1)) >> shift
        first = st_ref[e] >> shift

        def mark(k, _):
            nx_ref[first + k] = nxt
            return 0

        lax.fori_loop(0, tp, mark, 0)
        return jnp.where(tp > 0, e, nxt)

    lax.fori_loop(0, N_EXPERTS, backward, jnp.int32(-1))

    last_e = te_ref[jnp.maximum(used - 1, 0)]
    last_s = sl_ref[jnp.maximum(used - 1, 0)]

    def tail(i, _):
        te_ref[i] = last_e
        sl_ref[i] = last_s
        nx_ref[i] = -1
        return 0

    lax.fori_loop(used, n_tiles, tail, 0)


def _plan(counts_i32, n_tiles):
    smem = lambda: pl.BlockSpec(memory_space=pltpu.SMEM)
    i32 = lambda k: jax.ShapeDtypeStruct((k,), jnp.int32)
    return pl.pallas_call(
        _plan_kernel,
        in_specs=[smem()],
        out_specs=[smem() for _ in range(7)],
        out_shape=[i32(n_tiles), i32(1), i32(n_tiles), i32(n_tiles),
                   i32(N_EXPERTS), i32(N_EXPERTS), i32(N_EXPERTS)],
        name="plan",
    )(counts_i32)


def _prow_copy(src_ref, src_row, dst_ref, dst_row, sem, rows=1):
    return pltpu.make_async_copy(src_ref.at[pl.ds(src_row, rows)], dst_ref.at[pl.ds(dst_row, rows)], sem)


ZERO_ROWS = TM_EXP // 2
DMA_UNROLL = 8


def _pad_fill(e, ps_ref, pl_ref, zbuf_ref, xs_ref, zsem, wait):
    ln = pl_ref[e]
    st = ps_ref[e]
    b = 1
    while b <= ZERO_ROWS:
        @pl.when((ln & b) != 0)
        def _(b=b):
            cp = _prow_copy(zbuf_ref, 0, xs_ref, st + (ln & (b - 1)), zsem, rows=b)
            if wait:
                cp.wait()
            else:
                cp.start()
        b *= 2


def _dispatch_kernel(pos0_ref, pos1_ref, ps_ref, pl_ref, h1p_ref, xs_ref, zbuf_ref, sem, zsem):
    i = pl.program_id(0)
    tm = TM_DISP

    @pl.when(i == 0)
    def _():
        zbuf_ref[...] = jnp.zeros_like(zbuf_ref)

        def fill_start(e, _):
            _pad_fill(e, ps_ref, pl_ref, zbuf_ref, xs_ref, zsem, False)
            return 0

        lax.fori_loop(0, N_EXPERTS, fill_start, 0)

    def issue(k, _):
        for u in range(DMA_UNROLL):
            r = k * DMA_UNROLL + u
            tok = i * tm + r
            _prow_copy(h1p_ref, r, xs_ref, pos0_ref[tok], sem).start(priority=0)
            _prow_copy(h1p_ref, r, xs_ref, pos1_ref[tok], sem).start(priority=1)
        return 0

    lax.fori_loop(0, tm // DMA_UNROLL, issue, 0)

    def drain(k, _):
        for u in range(DMA_UNROLL):
            _prow_copy(h1p_ref, 0, xs_ref, 0, sem).wait()
            _prow_copy(h1p_ref, 0, xs_ref, 0, sem).wait()
        return 0

    lax.fori_loop(0, tm // DMA_UNROLL, drain, 0)

    @pl.when(i == pl.num_programs(0) - 1)
    def _():
        def fill_wait(e, _):
            _pad_fill(e, ps_ref, pl_ref, zbuf_ref, xs_ref, zsem, True)
            return 0

        lax.fori_loop(0, N_EXPERTS, fill_wait, 0)


def _dispatch(pos0, pos1, pad_start, pad_len, h1p, n_rows):
    n = h1p.shape[0]
    return pl.pallas_call(
        _dispatch_kernel,
        grid_spec=pltpu.PrefetchScalarGridSpec(
            num_scalar_prefetch=4,
            grid=(n // TM_DISP,),
            in_specs=[pl.BlockSpec((TM_DISP, RT, LANES), lambda i, *_: (i, 0, 0))],
            out_specs=pl.BlockSpec(memory_space=pl.ANY),
            scratch_shapes=[
                pltpu.VMEM((ZERO_ROWS, RT, LANES), BF16),
                pltpu.SemaphoreType.DMA(()),
                pltpu.SemaphoreType.DMA(()),
            ],
        ),
        out_shape=jax.ShapeDtypeStruct((n_rows, RT, LANES), BF16),
        compiler_params=pltpu.CompilerParams(
            dimension_semantics=("arbitrary",),
            vmem_limit_bytes=VMEM_LIMIT),
        name="dispatch",
    )(pos0, pos1, pad_start, pad_len, h1p)


def _expert_kernel(te_ref, nu_ref, nx_ref, sl_ref, xs_ref, wg_hbm, wu_hbm, wd_hbm, ys_ref,
                   wgf_ref, wuf_ref, wdf_ref, wgb_ref, wub_ref, wdb_ref, stage_ref, wsem):
    i = pl.program_id(0)
    used = i < nu_ref[0]
    e = te_ref[i]
    s = sl_ref[i] & 1
    pieces = (sl_ref[i] >> 1) + 1
    fresh = (i == 0) | (e != te_ref[jnp.maximum(i - 1, 0)])

    def weight_copies(expert, slot):
        return (pltpu.make_async_copy(wg_hbm.at[expert], wgf_ref.at[slot], wsem.at[slot, 0]),
                pltpu.make_async_copy(wu_hbm.at[expert], wuf_ref.at[slot], wsem.at[slot, 1]),
                pltpu.make_async_copy(wd_hbm.at[expert], wdf_ref.at[slot], wsem.at[slot, 2]))

    @pl.when(i == 0)
    def _():
        for cp in weight_copies(e, s):
            cp.start(priority=1)

    @pl.when(used & fresh)
    def _():
        for cp in weight_copies(e, s):
            cp.wait()

        @pl.when(nx_ref[i] >= 0)
        def _():
            for cp in weight_copies(nx_ref[i], 1 - s):
                cp.start(priority=1)

        wgb_ref[...] = wgf_ref[s].astype(BF16)
        wub_ref[...] = wuf_ref[s].astype(BF16)
        wdb_ref[...] = wdf_ref[s].astype(BF16)

    def ffn(rows):
        x = jnp.concatenate([p.astype(BF16) for p in _load_rows(xs_ref, 0, rows, stage_ref)],
                            axis=1)
        g = jnp.dot(x, wgb_ref[...], preferred_element_type=F32)
        u = jnp.dot(x, wub_ref[...], preferred_element_type=F32)
        h = (g * _sigmoid(g) * u).astype(BF16)
        y = jnp.dot(h, wdb_ref[...], preferred_element_type=F32)
        _store_rows(ys_ref, 0, rows, y, stage_ref)

    for p in range(1, TM_EXP // EXP_PATH_ROWS + 1):
        @pl.when(used & (pieces == p))
        def _(p=p):
            ffn(p * EXP_PATH_ROWS)


def _expert_ffn(tile_expert, n_used, next_expert, slot, xs, wg, wu, wd):
    n_rows = xs.shape[0]
    n_tiles = n_rows // TM_EXP

    def row_map(i, te, nu, nx, sl):
        return (jnp.minimum(i, nu[0] - 1), 0, 0)

    return pl.pallas_call(
        _expert_kernel,
        grid_spec=pltpu.PrefetchScalarGridSpec(
            num_scalar_prefetch=4,
            grid=(n_tiles,),
            in_specs=[
                pl.BlockSpec((TM_EXP, RT, LANES), row_map),
                pl.BlockSpec(memory_space=pl.ANY),
                pl.BlockSpec(memory_space=pl.ANY),
                pl.BlockSpec(memory_space=pl.ANY),
            ],
            out_specs=pl.BlockSpec((TM_EXP, RT, LANES), row_map),
            scratch_shapes=[
                pltpu.VMEM((2, D_MODEL, D_EXPERT), F32),
                pltpu.VMEM((2, D_MODEL, D_EXPERT), F32),
                pltpu.VMEM((2, D_EXPERT, D_MODEL), F32),
                pltpu.VMEM((D_MODEL, D_EXPERT), BF16),
                pltpu.VMEM((D_MODEL, D_EXPERT), BF16),
                pltpu.VMEM((D_EXPERT, D_MODEL), BF16),
                pltpu.VMEM((TM_EXP * PITCH, LANES), F32),
                pltpu.SemaphoreType.DMA((2, 3)),
            ],
        ),
        out_shape=jax.ShapeDtypeStruct((n_rows, RT, LANES), BF16),
        compiler_params=pltpu.CompilerParams(
            dimension_semantics=("arbitrary",),
            vmem_limit_bytes=VMEM_LIMIT),
        name="expert_ffn",
    )(tile_expert, n_used, next_expert, slot, xs, wg, wu, wd)


def _combine_kernel(pos0_ref, pos1_ref, h1_ref, info_ref, g2_ref, b2_ref, ys_ref, o_ref,
                    ybuf_ref, stage_ref, sem):
    i = pl.program_id(0)
    tm = TM_COMB
    par = lax.rem(i, 2)

    def issue(step, parity):
        def body(k, _):
            for u in range(DMA_UNROLL):
                r = k * DMA_UNROLL + u
                tok = step * tm + r
                _prow_copy(ys_ref, pos0_ref[tok], ybuf_ref.at[parity, 0], r,
                           sem.at[parity]).start(priority=0)
                _prow_copy(ys_ref, pos1_ref[tok], ybuf_ref.at[parity, 1], r,
                           sem.at[parity]).start(priority=1)
            return 0
        lax.fori_loop(0, tm // DMA_UNROLL, body, 0)

    def drain(parity):
        def body(k, _):
            for u in range(DMA_UNROLL):
                _prow_copy(ys_ref, 0, ybuf_ref.at[parity, 0], 0, sem.at[parity]).wait()
                _prow_copy(ys_ref, 0, ybuf_ref.at[parity, 1], 0, sem.at[parity]).wait()
            return 0
        lax.fori_loop(0, tm // DMA_UNROLL, body, 0)

    @pl.when(i == 0)
    def _():
        issue(0, 0)

    @pl.when(i + 1 < pl.num_programs(0))
    def _():
        issue(i + 1, 1 - par)

    drain(par)

    rows = 64
    for c in range(tm // rows):
        rs = slice(c * rows, (c + 1) * rows)
        q1 = info_ref[rs, 2:3]
        q2 = info_ref[rs, 3:4]
        y0 = _load_rows(ybuf_ref.at[par, 0], c * rows, rows, stage_ref)
        y1 = _load_rows(ybuf_ref.at[par, 1], c * rows, rows, stage_ref)
        ffn = jnp.concatenate([q1 * a + q2 * b for a, b in zip(y0, y1)], axis=1)
        o_ref[rs, :] = _ln_rows(DN_ALPHA * h1_ref[rs, :] + ffn, g2_ref[...], b2_ref[...])


def _combine(pos0, pos1, h1, info, g2, b2, ys):
    n = h1.shape[0]
    return pl.pallas_call(
        _combine_kernel,
        grid_spec=pltpu.PrefetchScalarGridSpec(
            num_scalar_prefetch=2,
            grid=(n // TM_COMB,),
            in_specs=[
                pl.BlockSpec((TM_COMB, D_MODEL), lambda i, *_: (i, 0)),
                pl.BlockSpec((TM_COMB, ROUTE_W), lambda i, *_: (i, 0)),
                pl.BlockSpec((1, D_MODEL), lambda i, *_: (0, 0)),
                pl.BlockSpec((1, D_MODEL), lambda i, *_: (0, 0)),
                pl.BlockSpec(memory_space=pl.ANY),
            ],
            out_specs=pl.BlockSpec((TM_COMB, D_MODEL), lambda i, *_: (i, 0)),
            scratch_shapes=[
                pltpu.VMEM((2, 2, TM_COMB, RT, LANES), BF16),
                pltpu.VMEM((64 * PITCH, LANES), F32),
                pltpu.SemaphoreType.DMA((2,)),
            ],
        ),
        out_shape=jax.ShapeDtypeStruct((n, D_MODEL), F32),
        compiler_params=pltpu.CompilerParams(
            dimension_semantics=("arbitrary",),
            vmem_limit_bytes=VMEM_LIMIT),
        name="combine",
    )(pos0, pos1, h1, info, g2, b2, ys)


def _block_diag(w, per):
    h, hd, _ = w.shape
    wg = w.reshape(h // per, per, hd, hd)
    eye = jnp.eye(per, dtype=w.dtype)
    return jnp.einsum("gpij,pq->gpiqj", wg, eye).reshape(h // per, per * hd, per * hd)


def kernel(x, ln_in_g, ln_in_b, w_in, lru_conv_w, lru_conv_b, lru_w_a, lru_b_a, lru_w_x, lru_b_x,
           lru_lambda, conf_conv_w, conf_conv_b, conf_ln_g, conf_ln_b, w_out, ln1_g, ln1_b,
           router_group_w, router_group_b, router_expert_w, router_expert_b, exp_w_gate, exp_w_up,
           exp_w_down, ln2_g, ln2_b):
    bsz, seq, d = x.shape
    n = bsz * seq
    x2 = x.reshape(n, d)
    row = lambda v: v.reshape(1, -1).astype(F32)
    l = 0

    z, h0 = _ln_win(x2, row(ln_in_g), row(ln_in_b), w_in[l])

    per = CB_LRU // LRU_HEAD_DIM
    wcat = jnp.concatenate([_block_diag(lru_w_a[l], per), _block_diag(lru_w_x[l], per)],
                           axis=-1).astype(BF16)
    a_out = _lru_mixer(z, lru_conv_w[l], row(lru_conv_b[l]), wcat, row(lru_b_a[l]),
                       row(lru_b_x[l]), row(lru_lambda[l]), bsz, seq)

    nlb = D_CONV // LANES
    w3 = jnp.pad(conf_conv_w[l], ((0, 32 - CONF_CONV_W), (0, 0)))
    w3 = w3.reshape(32, nlb, LANES).transpose(1, 0, 2)
    cb3 = conf_conv_b[l].reshape(nlb, 1, LANES)
    b_out, wo = _conf_mixer(z, w3, cb3, row(conf_ln_g[l]), row(conf_ln_b[l]), w_out[l], bsz, seq)

    wr = jnp.concatenate([router_group_w[l], router_expert_w[l]], axis=1)
    wr = jnp.pad(wr, ((0, 0), (0, ROUTE_W - wr.shape[1])))
    wr_hi = wr.astype(BF16)
    wr_lo = (wr - wr_hi.astype(F32)).astype(BF16)
    wr_cat = jnp.concatenate([wr_hi, wr_lo], axis=1)
    br = jnp.concatenate([router_group_b[l], router_expert_b[l]])
    br = jnp.pad(br, (0, ROUTE_W - br.shape[0])).reshape(1, ROUTE_W)
    h1, h1p, logits = _wout_router(a_out, b_out, h0, wo, row(ln1_g[l]), row(ln1_b[l]), wr_cat, br)

    info, counts = _route(logits)
    idx = info[:, 0:6].astype(jnp.int32)
    r0, r1, e0, e1 = idx[:, 0], idx[:, 1], idx[:, 4], idx[:, 5]

    n_tiles = (n * 2) // TM_EXP + N_EXPERTS
    tile_expert, n_used, next_expert, slot, starts, pad_start, pad_len = _plan(
        counts[0].astype(jnp.int32), n_tiles)

    pos0 = (starts[e0] + r0).astype(jnp.int32)
    pos1 = (starts[e1] + r1).astype(jnp.int32)

    xs = _dispatch(pos0, pos1, pad_start, pad_len, h1p, n_tiles * TM_EXP)
    shp = (N_EXPERTS, D_MODEL, D_EXPERT)
    ys = _expert_ffn(tile_expert, n_used, next_expert, slot, xs, exp_w_gate[l].reshape(shp),
                     exp_w_up[l].reshape(shp), exp_w_down[l].reshape(N_EXPERTS, D_EXPERT, D_MODEL))
    out = _combine(pos0, pos1, h1, info, row(ln2_g[l]), row(ln2_b[l]), ys)
    return out.reshape(bsz, seq, d)
```

```python
import jax
import jax.numpy as jnp
from jax import lax
from jax.experimental import pallas as pl
from jax.experimental.pallas import tpu as pltpu

F32 = jnp.float32
BF16 = jnp.bfloat16

D_MODEL = 2048
D_LRU = 1024
D_CONV = 1024
LRU_HEADS = 16
LRU_HEAD_DIM = 64
LRU_C = 8.0
LRU_CONV_W = 4
CONF_CONV_W = 31
N_GROUPS = 4
EXPERTS_PER_GROUP = 8
N_EXPERTS = N_GROUPS * EXPERTS_PER_GROUP
D_EXPERT = 512
LN_EPS = 1e-5
DEPTH = 1
DN_ALPHA = (2 * DEPTH) ** 0.25

LANES = 128
SUBLANES = 8
VMEM_LIMIT = 56 * 1024 * 1024

TM_WIN = 512
TN_WIN = 1024
W_CHUNK = 256
TT_LRU = 4096
CB_LRU = 256
TT_CONF = 256
CONF_HALO = 32
TM_OUT = 512
TM_ROUTE = 1024
TM_DISP = 1024
TM_EXP = 256
EXP_PATH_ROWS = 128
TM_COMB = 256
ROUTE_W = 128


def _sigmoid(x):
    return 0.5 * (jnp.tanh(0.5 * x) + 1.0)


def _ln_rows(x, g, b):
    mu = jnp.mean(x, axis=-1, keepdims=True)
    xc = x - mu
    var = jnp.mean(xc * xc, axis=-1, keepdims=True)
    return xc * lax.rsqrt(var + LN_EPS) * g + b


def _stage_weight(w_hbm, wb_ref, wst_ref, wsem):
    nchunk = wb_ref.shape[1] // W_CHUNK

    def chunk_copy(c):
        return pltpu.make_async_copy(w_hbm.at[:, pl.ds(c * W_CHUNK, W_CHUNK)],
                                     wst_ref.at[c % 2], wsem.at[c % 2])

    chunk_copy(0).start()
    for c in range(nchunk):
        if c + 1 < nchunk:
            chunk_copy(c + 1).start()
        chunk_copy(c).wait()
        wb_ref[:, c * W_CHUNK:(c + 1) * W_CHUNK] = wst_ref[c % 2].astype(BF16)


def _ln_win_kernel(x_ref, g_ref, b_ref, w_hbm, z_ref, h_ref, wb_ref, wst_ref, xn_ref, wsem):
    s = pl.program_id(0)
    par = lax.rem(s, 2)

    @pl.when(s == 0)
    def _():
        xn_ref[1] = jnp.zeros(xn_ref.shape[1:], BF16)
        _stage_weight(w_hbm, wb_ref, wst_ref, wsem)

    rows = 128
    for c in range(TM_WIN // rows):
        rs = slice(c * rows, (c + 1) * rows)
        hn = _ln_rows(x_ref[rs, :], g_ref[...], b_ref[...])
        h_ref[rs, :] = hn
        xn_ref[par, rs, :] = hn.astype(BF16)

    xprev = xn_ref[1 - par]
    for c in range(z_ref.shape[1] // TN_WIN):
        cs = slice(c * TN_WIN, (c + 1) * TN_WIN)
        z_ref[:, cs] = jnp.dot(xprev, wb_ref[:, cs], preferred_element_type=F32).astype(z_ref.dtype)


def _ln_win(x2, g, b, w):
    n = x2.shape[0]
    ncol = w.shape[1]
    nt = n // TM_WIN
    return pl.pallas_call(
        _ln_win_kernel,
        grid=(nt + 1,),
        in_specs=[
            pl.BlockSpec((TM_WIN, D_MODEL), lambda s: (jnp.minimum(s, nt - 1), 0)),
            pl.BlockSpec((1, D_MODEL), lambda s: (0, 0)),
            pl.BlockSpec((1, D_MODEL), lambda s: (0, 0)),
            pl.BlockSpec(memory_space=pl.ANY),
        ],
        out_specs=[
            pl.BlockSpec((TM_WIN, ncol), lambda s: (jnp.maximum(s - 1, 0), 0)),
            pl.BlockSpec((TM_WIN, D_MODEL), lambda s: (jnp.minimum(s, nt - 1), 0)),
        ],
        out_shape=[
            jax.ShapeDtypeStruct((n, ncol), BF16),
            jax.ShapeDtypeStruct((n, D_MODEL), F32),
        ],
        scratch_shapes=[
            pltpu.VMEM((D_MODEL, ncol), BF16),
            pltpu.VMEM((2, D_MODEL, W_CHUNK), F32),
            pltpu.VMEM((2, TM_WIN, D_MODEL), BF16),
            pltpu.SemaphoreType.DMA((2,)),
        ],
        compiler_params=pltpu.CompilerParams(
            dimension_semantics=("arbitrary",),
            vmem_limit_bytes=VMEM_LIMIT),
        name="ln_win",
    )(x2, g, b, w)


def _lru_kernel(zx_ref, zg_ref, cw_ref, cb_ref, wcat_ref, ba_ref, bx_ref, lam_ref,
                o_ref, xs_ref, hp_ref, a_ref, g_ref):
    t = pl.program_id(2)
    tt = TT_LRU

    @pl.when(t == 0)
    def _():
        xs_ref[0:SUBLANES, :] = jnp.zeros((SUBLANES, CB_LRU), F32)
        hp_ref[...] = jnp.zeros_like(hp_ref)

    @pl.when(t > 0)
    def _():
        xs_ref[0:SUBLANES, :] = xs_ref[tt:tt + SUBLANES, :]

    xs_ref[SUBLANES:SUBLANES + tt, :] = zx_ref[...].astype(F32)

    rows = 128
    for rb in range(tt // rows):
        acc = jnp.broadcast_to(cb_ref[...], (rows, CB_LRU))
        for k in range(LRU_CONV_W):
            off = rb * rows + SUBLANES - (LRU_CONV_W - 1) + k
            acc = acc + cw_ref[k:k + 1, :] * xs_ref[off:off + rows, :]
        a_ref[rb * rows:(rb + 1) * rows, :] = acc

    g_ref[...] = jnp.dot(a_ref[...].astype(BF16), wcat_ref[0], preferred_element_type=F32)

    lam = lam_ref[...]
    softplus_neg = jnp.maximum(-lam, 0.0) + jnp.log1p(jnp.exp(-jnp.abs(lam)))
    cvec = -LRU_C * softplus_neg
    ba = ba_ref[...]
    bx = bx_ref[...]
    blk = 64
    row_in_vreg = lax.broadcasted_iota(jnp.int32, (blk, CB_LRU), 0) & (SUBLANES - 1)

    def body(rb, h):
        rs = pl.ds(pl.multiple_of(rb * blk, blk), blk)
        a_in = a_ref[rs, :]
        r = _sigmoid(g_ref[rs, 0:CB_LRU] + ba)
        i = _sigmoid(g_ref[rs, CB_LRU:2 * CB_LRU] + bx)
        log_a = cvec * r
        a = jnp.exp(log_a)
        m2 = -jnp.tanh(log_a) * (a * a + 1.0)
        u = jnp.where(m2 > 0.0, m2 * lax.rsqrt(m2), 0.0) * (i * a_in)
        for s in (1, 2, 4):
            m = row_in_vreg >= s
            a_sh = jnp.where(m, pltpu.roll(a, s, 0), 1.0)
            u_sh = jnp.where(m, pltpu.roll(u, s, 0), 0.0)
            u = u + a * u_sh
            a = a * a_sh
        outs = []
        for gi in range(blk // SUBLANES):
            ag = a[gi * SUBLANES:(gi + 1) * SUBLANES, :]
            ug = u[gi * SUBLANES:(gi + 1) * SUBLANES, :]
            hg = ug + ag * h
            h = hg[SUBLANES - 1:SUBLANES, :]
            outs.append(hg)
        hblk = jnp.concatenate(outs, axis=0)
        gl = zg_ref[rs, :].astype(F32)
        gelu = 0.5 * gl * (1.0 + jnp.tanh(0.7978845608028654 * (gl + 0.044715 * gl * gl * gl)))
        o_ref[rs, :] = (gelu * hblk).astype(o_ref.dtype)
        return h

    h = lax.fori_loop(0, tt // blk, body, hp_ref[0:1, :])
    hp_ref[...] = jnp.broadcast_to(h, hp_ref.shape)


def _lru_mixer(z, cw, cb, wcat, ba, bx, lam, bsz, seq):
    n = z.shape[0]
    nt = seq // TT_LRU
    ncb = D_LRU // CB_LRU
    row = lambda b, j, t: b * nt + t
    vec = pl.BlockSpec((1, CB_LRU), lambda b, j, t: (0, j))
    return pl.pallas_call(
        _lru_kernel,
        grid=(bsz, ncb, nt),
        in_specs=[
            pl.BlockSpec((TT_LRU, CB_LRU), lambda b, j, t: (row(b, j, t), j)),
            pl.BlockSpec((TT_LRU, CB_LRU), lambda b, j, t: (row(b, j, t), ncb + j)),
            pl.BlockSpec((LRU_CONV_W, CB_LRU), lambda b, j, t: (0, j)),
            vec,
            pl.BlockSpec((1, CB_LRU, 2 * CB_LRU), lambda b, j, t: (j, 0, 0)),
            vec, vec, vec,
        ],
        out_specs=pl.BlockSpec((TT_LRU, CB_LRU), lambda b, j, t: (row(b, j, t), j)),
        out_shape=jax.ShapeDtypeStruct((n, D_LRU), BF16),
        scratch_shapes=[
            pltpu.VMEM((TT_LRU + SUBLANES, CB_LRU), F32),
            pltpu.VMEM((SUBLANES, CB_LRU), F32),
            pltpu.VMEM((TT_LRU, CB_LRU), F32),
            pltpu.VMEM((TT_LRU, 2 * CB_LRU), F32),
        ],
        compiler_params=pltpu.CompilerParams(
            dimension_semantics=("arbitrary", "arbitrary", "arbitrary"),
            vmem_limit_bytes=VMEM_LIMIT),
        name="lru_mixer",
    )(z, z, cw, cb, wcat, ba, bx, lam)


def _conf_kernel(zv_ref, zg_ref, w_ref, cb_ref, lg_ref, lb_ref, wo_ref, o_ref, wob_ref,
                 cs_ref, cv_ref):
    t = pl.program_id(1)
    tt = TT_CONF
    nlb = D_CONV // LANES
    wob_ref[...] = wo_ref[...].astype(BF16)

    @pl.when(t == 0)
    def _():
        cs_ref[:, 0:CONF_HALO, :] = jnp.zeros((nlb, CONF_HALO, LANES), F32)

    @pl.when(t > 0)
    def _():
        cs_ref[:, 0:CONF_HALO, :] = cs_ref[:, tt:tt + CONF_HALO, :]

    for c in range(nlb):
        ls = slice(c * LANES, (c + 1) * LANES)
        v = zv_ref[:, ls].astype(F32)
        g = zg_ref[:, ls].astype(F32)
        cs_ref[c, CONF_HALO:CONF_HALO + tt, :] = v * _sigmoid(g)

    rows = 64
    nrb = tt // rows
    base = CONF_HALO - (CONF_CONV_W - 1)

    def conv_body(c, carry):
        accs = [jnp.broadcast_to(cb_ref[c], (rows, LANES)) for _ in range(nrb)]
        for k in range(CONF_CONV_W):
            wk = w_ref[c, k:k + 1, :]
            for rb in range(nrb):
                off = rb * rows + base + k
                accs[rb] = accs[rb] + wk * cs_ref[c, off:off + rows, :]
        for rb in range(nrb):
            cv_ref[c, rb * rows:(rb + 1) * rows, :] = accs[rb]
        return carry

    lax.fori_loop(0, nlb, conv_body, 0)

    ln_rows = 32
    inv_n = 1.0 / D_CONV
    for rb in range(tt // ln_rows):
        rs = slice(rb * ln_rows, (rb + 1) * ln_rows)
        blk = cv_ref[:, rs, :]
        mu = jnp.sum(jnp.sum(blk, axis=0), axis=-1, keepdims=True) * inv_n
        d = blk - mu[None]
        var = jnp.sum(jnp.sum(d * d, axis=0), axis=-1, keepdims=True) * inv_n
        inv = lax.rsqrt(var + LN_EPS)
        for c in range(nlb):
            ls = slice(c * LANES, (c + 1) * LANES)
            y = d[c] * inv * lg_ref[:, ls] + lb_ref[:, ls]
            o_ref[rs, ls] = (y * _sigmoid(y)).astype(o_ref.dtype)


def _conf_mixer(z, w3, cb3, lg, lb, wo, bsz, seq):
    n = z.shape[0]
    nt = seq // TT_CONF
    nlb = D_CONV // LANES
    wo_rows = wo.shape[0] // (bsz * nt)
    assert wo_rows * bsz * nt == wo.shape[0] and wo_rows % 16 == 0
    return pl.pallas_call(
        _conf_kernel,
        grid=(bsz, nt),
        in_specs=[
            pl.BlockSpec((TT_CONF, D_CONV), lambda b, t: (b * nt + t, 2)),
            pl.BlockSpec((TT_CONF, D_CONV), lambda b, t: (b * nt + t, 3)),
            pl.BlockSpec((nlb, 32, LANES), lambda b, t: (0, 0, 0)),
            pl.BlockSpec((nlb, 1, LANES), lambda b, t: (0, 0, 0)),
            pl.BlockSpec((1, D_CONV), lambda b, t: (0, 0)),
            pl.BlockSpec((1, D_CONV), lambda b, t: (0, 0)),
            pl.BlockSpec((wo_rows, D_MODEL), lambda b, t: (b * nt + t, 0)),
        ],
        out_specs=[
            pl.BlockSpec((TT_CONF, D_CONV), lambda b, t: (b * nt + t, 0)),
            pl.BlockSpec((wo_rows, D_MODEL), lambda b, t: (b * nt + t, 0)),
        ],
        out_shape=[
            jax.ShapeDtypeStruct((n, D_CONV), BF16),
            jax.ShapeDtypeStruct(wo.shape, BF16),
        ],
        scratch_shapes=[
            pltpu.VMEM((nlb, CONF_HALO + TT_CONF, LANES), F32),
            pltpu.VMEM((nlb, TT_CONF, LANES), F32),
        ],
        compiler_params=pltpu.CompilerParams(
            dimension_semantics=("arbitrary", "arbitrary"),
            vmem_limit_bytes=VMEM_LIMIT),
        name="conf_mixer",
    )(z, z, w3, cb3, lg, lb, wo)


def _split_bf16(v):
    hi = v.astype(BF16)
    lo = (v - hi.astype(F32)).astype(BF16)
    return hi, lo


RT = D_MODEL // LANES
PITCH = RT + SUBLANES


def _store_rows(dst_ref, row0, rows, v, stage_ref):
    for s in range(RT):
        stage_ref[pl.ds(s, rows, stride=PITCH), :] = v[:, s * LANES:(s + 1) * LANES]
    staged = stage_ref[0:rows * PITCH, :].reshape(rows, PITCH, LANES)
    dst_ref[row0:row0 + rows] = staged[:, 0:RT, :].astype(BF16)


def _load_rows(src_ref, row0, rows, stage_ref):
    tile = src_ref[row0:row0 + rows].astype(F32)
    tile = jnp.concatenate([tile, jnp.zeros((rows, PITCH - RT, LANES), F32)], axis=1)
    stage_ref[0:rows * PITCH, :] = tile.reshape(rows * PITCH, LANES)
    return [stage_ref[pl.ds(s, rows, stride=PITCH), :] for s in range(RT)]


def _wout_kernel(a_ref, b_ref, h_ref, wa_ref, wb_ref, g1_ref, b1_ref,
                 wr_ref, br_ref, h1_ref, h1r_ref, lg_ref, mixa_ref, mixb_ref, hl_ref, stage_ref):
    half = TM_OUT // 2
    for mix_ref, r0 in ((mixa_ref, 0), (mixb_ref, half)):
        hs = slice(r0, r0 + half)
        mix_ref[...] = (jnp.dot(a_ref[hs, :], wa_ref[...], preferred_element_type=F32)
                        + jnp.dot(b_ref[hs, :], wb_ref[...], preferred_element_type=F32))
    rows = 64
    for c in range(TM_OUT // rows):
        rs = slice(c * rows, (c + 1) * rows)
        mix_ref, r0 = (mixa_ref, 0) if c * rows < half else (mixb_ref, half)
        mix = mix_ref[c * rows - r0:(c + 1) * rows - r0, :]
        h1 = _ln_rows(DN_ALPHA * h_ref[rs, :] + mix, g1_ref[...], b1_ref[...])
        h1_ref[rs, :] = h1
        _store_rows(h1r_ref, c * rows, rows, h1, stage_ref)
        hi, lo = _split_bf16(h1)
        hl_ref[rs, :] = hi
        hl_ref[TM_OUT + c * rows:TM_OUT + (c + 1) * rows, :] = lo
    p = jnp.dot(hl_ref[...], wr_ref[...], preferred_element_type=F32)
    lg_ref[...] = (p[0:TM_OUT, 0:ROUTE_W] + p[0:TM_OUT, ROUTE_W:2 * ROUTE_W]
                   + p[TM_OUT:2 * TM_OUT, 0:ROUTE_W] + br_ref[...])


def _wout_router(a, b, h, wo, g1, b1, wr_cat, br):
    n = h.shape[0]
    assert D_LRU == D_CONV
    full = lambda shape: pl.BlockSpec(shape, lambda i: tuple(0 for _ in shape))
    return pl.pallas_call(
        _wout_kernel,
        grid=(n // TM_OUT,),
        in_specs=[
            pl.BlockSpec((TM_OUT, D_LRU), lambda i: (i, 0)),
            pl.BlockSpec((TM_OUT, D_CONV), lambda i: (i, 0)),
            pl.BlockSpec((TM_OUT, D_MODEL), lambda i: (i, 0)),
            pl.BlockSpec((D_LRU, D_MODEL), lambda i: (0, 0)),
            pl.BlockSpec((D_CONV, D_MODEL), lambda i: (1, 0)),
            full((1, D_MODEL)), full((1, D_MODEL)),
            full((D_MODEL, 2 * ROUTE_W)), full((1, ROUTE_W)),
        ],
        out_specs=[
            pl.BlockSpec((TM_OUT, D_MODEL), lambda i: (i, 0)),
            pl.BlockSpec((TM_OUT, RT, LANES), lambda i: (i, 0, 0)),
            pl.BlockSpec((TM_OUT, ROUTE_W), lambda i: (i, 0)),
        ],
        out_shape=[
            jax.ShapeDtypeStruct((n, D_MODEL), F32),
            jax.ShapeDtypeStruct((n, RT, LANES), BF16),
            jax.ShapeDtypeStruct((n, ROUTE_W), F32),
        ],
        scratch_shapes=[
            pltpu.VMEM((TM_OUT // 2, D_MODEL), F32),
            pltpu.VMEM((TM_OUT // 2, D_MODEL), F32),
            pltpu.VMEM((2 * TM_OUT, D_MODEL), BF16),
            pltpu.VMEM((64 * PITCH, LANES), F32),
        ],
        compiler_params=pltpu.CompilerParams(
            dimension_semantics=("arbitrary",),
            vmem_limit_bytes=VMEM_LIMIT),
        name="wout_router",
    )(a, b, h, wo, wo, g1, b1, wr_cat, br)


def _route_kernel(lg_ref, info_ref, cnt_ref, run_ref, tri_ref):
    t = pl.program_id(0)
    tm = TM_ROUTE
    l = lg_ref[...]
    lane = lax.broadcasted_iota(jnp.int32, (tm, ROUTE_W), 1)
    neg = jnp.float32(-jnp.inf)
    big = jnp.int32(1 << 20)

    gmask = lane < N_GROUPS
    gmax = jnp.max(jnp.where(gmask, l, neg), axis=-1, keepdims=True)
    gsel = jnp.min(jnp.where(gmask & (l == gmax), lane, big), axis=-1, keepdims=True)
    gsum = jnp.sum(jnp.where(gmask, jnp.exp(l - gmax), 0.0), axis=-1, keepdims=True)
    pg_top = 1.0 / gsum

    lo = N_GROUPS + EXPERTS_PER_GROUP * gsel
    emask = (lane >= lo) & (lane < lo + EXPERTS_PER_GROUP)
    v1 = jnp.max(jnp.where(emask, l, neg), axis=-1, keepdims=True)
    i1 = jnp.min(jnp.where(emask & (l == v1), lane, big), axis=-1, keepdims=True)
    emask2 = emask & (lane != i1)
    v2 = jnp.max(jnp.where(emask2, l, neg), axis=-1, keepdims=True)
    i2 = jnp.min(jnp.where(emask2 & (l == v2), lane, big), axis=-1, keepdims=True)
    e21 = jnp.exp(v2 - v1)
    q1 = pg_top / (1.0 + e21)
    q2 = pg_top * e21 / (1.0 + e21)

    oh1 = (lane == i1).astype(F32)
    oh2 = (lane == i2).astype(F32)
    ohs = oh1 + oh2

    @pl.when(t == 0)
    def _():
        run_ref[...] = jnp.zeros_like(run_ref)
        r_i = lax.broadcasted_iota(jnp.int32, (tm, tm), 0)
        c_i = lax.broadcasted_iota(jnp.int32, (tm, tm), 1)
        tri_ref[...] = (c_i < r_i).astype(BF16)

    cum = jnp.dot(tri_ref[...], ohs.astype(BF16), preferred_element_type=F32)
    basev = run_ref[0:1, :] + cum
    r1 = jnp.sum(oh1 * basev, axis=-1, keepdims=True)
    r2 = jnp.sum(oh2 * basev, axis=-1, keepdims=True)
    run_ref[...] = run_ref[...] + jnp.sum(ohs, axis=0, keepdims=True)
    info = jnp.where(lane == 0, r1, 0.0)
    info = jnp.where(lane == 1, r2, info)
    info = jnp.where(lane == 2, q1, info)
    info = jnp.where(lane == 3, q2, info)
    info = jnp.where(lane == 4, (i1 - N_GROUPS).astype(F32), info)
    info = jnp.where(lane == 5, (i2 - N_GROUPS).astype(F32), info)
    info_ref[...] = info
    cnt_ref[...] = run_ref[...]


def _route(logits):
    n = logits.shape[0]
    nt = n // TM_ROUTE
    return pl.pallas_call(
        _route_kernel,
        grid=(nt,),
        in_specs=[pl.BlockSpec((TM_ROUTE, ROUTE_W), lambda t: (t, 0))],
        out_specs=[
            pl.BlockSpec((TM_ROUTE, ROUTE_W), lambda t: (t, 0)),
            pl.BlockSpec((SUBLANES, ROUTE_W), lambda t: (0, 0)),
        ],
        out_shape=[
            jax.ShapeDtypeStruct((n, ROUTE_W), F32),
            jax.ShapeDtypeStruct((SUBLANES, ROUTE_W), F32),
        ],
        scratch_shapes=[
            pltpu.VMEM((SUBLANES, ROUTE_W), F32),
            pltpu.VMEM((TM_ROUTE, TM_ROUTE), BF16),
        ],
        compiler_params=pltpu.CompilerParams(
            dimension_semantics=("arbitrary",),
            vmem_limit_bytes=VMEM_LIMIT),
        name="route",
    )(logits)


def _plan_kernel(cnt_ref, te_ref, nu_ref, nx_ref, sl_ref, st_ref, ps_ref, pl_ref):
    n_tiles = te_ref.shape[0]
    shift = TM_EXP.bit_length() - 1

    def forward(e, carry):
        tile, parity = carry
        c = cnt_ref[N_GROUPS + e]
        tp = (c + (TM_EXP - 1)) >> shift
        st_ref[e] = tile * TM_EXP
        ps_ref[e] = tile * TM_EXP + c
        pl_ref[e] = tp * TM_EXP - c

        def mark(k, _):
            te_ref[tile + k] = e
            real = jnp.minimum(c - k * TM_EXP, TM_EXP)
            pieces = (real + (EXP_PATH_ROWS - 1)) >> (EXP_PATH_ROWS.bit_length() - 1)
            sl_ref[tile + k] = parity + 2 * (pieces - 1)
            return 0

        lax.fori_loop(0, tp, mark, 0)
        return tile + tp, jnp.where(tp > 0, 1 - parity, parity)

    used, _ = lax.fori_loop(0, N_EXPERTS, forward, (jnp.int32(0), jnp.int32(0)))
    nu_ref[0] = used

    def backward(j, nxt):
        e = N_EXPERTS - 1 - j
        c = cnt_ref[N_GROUPS + e]
        tp = (c + (TM_EXP - 1)) >> shift
        first = st_ref[e] >> shift

        def mark(k, _):
            nx_ref[first + k] = nxt
            return 0

        lax.fori_loop(0, tp, mark, 0)
        return jnp.where(tp > 0, e, nxt)

    lax.fori_loop(0, N_EXPERTS, backward, jnp.int32(-1))

    last_e = te_ref[jnp.maximum(used - 1, 0)]
    last_s = sl_ref[jnp.maximum(used - 1, 0)]

    def tail(i, _):
        te_ref[i] = last_e
        sl_ref[i] = last_s
        nx_ref[i] = -1
        return 0

    lax.fori_loop(used, n_tiles, tail, 0)


def _plan(counts_i32, n_tiles):
    smem = lambda: pl.BlockSpec(memory_space=pltpu.SMEM)
    i32 = lambda k: jax.ShapeDtypeStruct((k,), jnp.int32)
    return pl.pallas_call(
        _plan_kernel,
        in_specs=[smem()],
        out_specs=[smem() for _ in range(7)],
        out_shape=[i32(n_tiles), i32(1), i32(n_tiles), i32(n_tiles),
                   i32(N_EXPERTS), i32(N_EXPERTS), i32(N_EXPERTS)],
        name="plan",
    )(counts_i32)


def _prow_copy(src_ref, src_row, dst_ref, dst_row, sem, rows=1):
    return pltpu.make_async_copy(src_ref.at[pl.ds(src_row, rows)], dst_ref.at[pl.ds(dst_row, rows)], sem)


ZERO_ROWS = TM_EXP // 2
DMA_UNROLL = 16


def _pad_fill(e, ps_ref, pl_ref, zbuf_ref, xs_ref, zsem, wait):
    ln = pl_ref[e]
    st = ps_ref[e]
    b = 1
    while b <= ZERO_ROWS:
        @pl.when((ln & b) != 0)
        def _(b=b):
            cp = _prow_copy(zbuf_ref, 0, xs_ref, st + (ln & (b - 1)), zsem, rows=b)
            if wait:
                cp.wait()
            else:
                cp.start()
        b *= 2


def _dispatch_kernel(pos0_ref, pos1_ref, ps_ref, pl_ref, h1p_ref, xs_ref, zbuf_ref, sem, zsem):
    i = pl.program_id(0)
    tm = TM_DISP

    @pl.when(i == 0)
    def _():
        zbuf_ref[...] = jnp.zeros_like(zbuf_ref)

        def fill_start(e, _):
            _pad_fill(e, ps_ref, pl_ref, zbuf_ref, xs_ref, zsem, False)
            return 0

        lax.fori_loop(0, N_EXPERTS, fill_start, 0)

    def issue(k, _):
        for u in range(DMA_UNROLL):
            r = k * DMA_UNROLL + u
            tok = i * tm + r
            _prow_copy(h1p_ref, r, xs_ref, pos0_ref[tok], sem).start(priority=0)
            _prow_copy(h1p_ref, r, xs_ref, pos1_ref[tok], sem).start(priority=1)
        return 0

    lax.fori_loop(0, tm // DMA_UNROLL, issue, 0)

    def drain(k, _):
        for u in range(DMA_UNROLL):
            _prow_copy(h1p_ref, 0, xs_ref, 0, sem).wait()
            _prow_copy(h1p_ref, 0, xs_ref, 0, sem).wait()
        return 0

    lax.fori_loop(0, tm // DMA_UNROLL, drain, 0)

    @pl.when(i == pl.num_programs(0) - 1)
    def _():
        def fill_wait(e, _):
            _pad_fill(e, ps_ref, pl_ref, zbuf_ref, xs_ref, zsem, True)
            return 0

        lax.fori_loop(0, N_EXPERTS, fill_wait, 0)


def _dispatch(pos0, pos1, pad_start, pad_len, h1p, n_rows):
    n = h1p.shape[0]
    return pl.pallas_call(
        _dispatch_kernel,
        grid_spec=pltpu.PrefetchScalarGridSpec(
            num_scalar_prefetch=4,
            grid=(n // TM_DISP,),
            in_specs=[pl.BlockSpec((TM_DISP, RT, LANES), lambda i, *_: (i, 0, 0))],
            out_specs=pl.BlockSpec(memory_space=pl.ANY),
            scratch_shapes=[
                pltpu.VMEM((ZERO_ROWS, RT, LANES), BF16),
                pltpu.SemaphoreType.DMA(()),
                pltpu.SemaphoreType.DMA(()),
            ],
        ),
        out_shape=jax.ShapeDtypeStruct((n_rows, RT, LANES), BF16),
        compiler_params=pltpu.CompilerParams(
            dimension_semantics=("arbitrary",),
            vmem_limit_bytes=VMEM_LIMIT),
        name="dispatch",
    )(pos0, pos1, pad_start, pad_len, h1p)


def _expert_kernel(te_ref, nu_ref, nx_ref, sl_ref, xs_ref, wg_hbm, wu_hbm, wd_hbm, ys_ref,
                   wgf_ref, wuf_ref, wdf_ref, wgb_ref, wub_ref, wdb_ref, stage_ref, wsem):
    i = pl.program_id(0)
    used = i < nu_ref[0]
    e = te_ref[i]
    s = sl_ref[i] & 1
    pieces = (sl_ref[i] >> 1) + 1
    fresh = (i == 0) | (e != te_ref[jnp.maximum(i - 1, 0)])

    def weight_copies(expert, slot):
        return (pltpu.make_async_copy(wg_hbm.at[expert], wgf_ref.at[slot], wsem.at[slot, 0]),
                pltpu.make_async_copy(wu_hbm.at[expert], wuf_ref.at[slot], wsem.at[slot, 1]),
                pltpu.make_async_copy(wd_hbm.at[expert], wdf_ref.at[slot], wsem.at[slot, 2]))

    @pl.when(i == 0)
    def _():
        for cp in weight_copies(e, s):
            cp.start(priority=1)

    @pl.when(used & fresh)
    def _():
        for cp in weight_copies(e, s):
            cp.wait()

        @pl.when(nx_ref[i] >= 0)
        def _():
            for cp in weight_copies(nx_ref[i], 1 - s):
                cp.start(priority=1)

        wgb_ref[...] = wgf_ref[s].astype(BF16)
        wub_ref[...] = wuf_ref[s].astype(BF16)
        wdb_ref[...] = wdf_ref[s].astype(BF16)

    def ffn(rows):
        x = jnp.concatenate([p.astype(BF16) for p in _load_rows(xs_ref, 0, rows, stage_ref)],
                            axis=1)
        g = jnp.dot(x, wgb_ref[...], preferred_element_type=F32)
        u = jnp.dot(x, wub_ref[...], preferred_element_type=F32)
        h = (g * _sigmoid(g) * u).astype(BF16)
        y = jnp.dot(h, wdb_ref[...], preferred_element_type=F32)
        _store_rows(ys_ref, 0, rows, y, stage_ref)

    for p in range(1, TM_EXP // EXP_PATH_ROWS + 1):
        @pl.when(used & (pieces == p))
        def _(p=p):
            ffn(p * EXP_PATH_ROWS)


def _expert_ffn(tile_expert, n_used, next_expert, slot, xs, wg, wu, wd):
    n_rows = xs.shape[0]
    n_tiles = n_rows // TM_EXP

    def row_map(i, te, nu, nx, sl):
        return (jnp.minimum(i, nu[0] - 1), 0, 0)

    return pl.pallas_call(
        _expert_kernel,
        grid_spec=pltpu.PrefetchScalarGridSpec(
            num_scalar_prefetch=4,
            grid=(n_tiles,),
            in_specs=[
                pl.BlockSpec((TM_EXP, RT, LANES), row_map),
                pl.BlockSpec(memory_space=pl.ANY),
                pl.BlockSpec(memory_space=pl.ANY),
                pl.BlockSpec(memory_space=pl.ANY),
            ],
            out_specs=pl.BlockSpec((TM_EXP, RT, LANES), row_map),
            scratch_shapes=[
                pltpu.VMEM((2, D_MODEL, D_EXPERT), F32),
                pltpu.VMEM((2, D_MODEL, D_EXPERT), F32),
                pltpu.VMEM((2, D_EXPERT, D_MODEL), F32),
                pltpu.VMEM((D_MODEL, D_EXPERT), BF16),
                pltpu.VMEM((D_MODEL, D_EXPERT), BF16),
                pltpu.VMEM((D_EXPERT, D_MODEL), BF16),
                pltpu.VMEM((TM_EXP * PITCH, LANES), F32),
                pltpu.SemaphoreType.DMA((2, 3)),
            ],
        ),
        out_shape=jax.ShapeDtypeStruct((n_rows, RT, LANES), BF16),
        compiler_params=pltpu.CompilerParams(
            dimension_semantics=("arbitrary",),
            vmem_limit_bytes=VMEM_LIMIT),
        name="expert_ffn",
    )(tile_expert, n_used, next_expert, slot, xs, wg, wu, wd)


def _combine_kernel(pos0_ref, pos1_ref, h1_ref, info_ref, g2_ref, b2_ref, ys_ref, o_ref,
                    ybuf_ref, stage_ref, sem):
    i = pl.program_id(0)
    tm = TM_COMB
    par = lax.rem(i, 2)

    def issue(step, parity):
        def body(k, _):
            for u in range(DMA_UNROLL):
                r = k * DMA_UNROLL + u
                tok = step * tm + r
                _prow_copy(ys_ref, pos0_ref[tok], ybuf_ref.at[parity, 0], r,
                           sem.at[parity]).start(priority=0)
                _prow_copy(ys_ref, pos1_ref[tok], ybuf_ref.at[parity, 1], r,
                           sem.at[parity]).start(priority=1)
            return 0
        lax.fori_loop(0, tm // DMA_UNROLL, body, 0)

    def drain(parity):
        def body(k, _):
            for u in range(DMA_UNROLL):
                _prow_copy(ys_ref, 0, ybuf_ref.at[parity, 0], 0, sem.at[parity]).wait()
                _prow_copy(ys_ref, 0, ybuf_ref.at[parity, 1], 0, sem.at[parity]).wait()
            return 0
        lax.fori_loop(0, tm // DMA_UNROLL, body, 0)

    @pl.when(i == 0)
    def _():
        issue(0, 0)

    @pl.when(i + 1 < pl.num_programs(0))
    def _():
        issue(i + 1, 1 - par)

    drain(par)

    rows = 64
    for c in range(tm // rows):
        rs = slice(c * rows, (c + 1) * rows)
        q1 = info_ref[rs, 2:3]
        q2 = info_ref[rs, 3:4]
        y0 = _load_rows(ybuf_ref.at[par, 0], c * rows, rows, stage_ref)
        y1 = _load_rows(ybuf_ref.at[par, 1], c * rows, rows, stage_ref)
        ffn = jnp.concatenate([q1 * a + q2 * b for a, b in zip(y0, y1)], axis=1)
        o_ref[rs, :] = _ln_rows(DN_ALPHA * h1_ref[rs, :] + ffn, g2_ref[...], b2_ref[...])


def _combine(pos0, pos1, h1, info, g2, b2, ys):
    n = h1.shape[0]
    return pl.pallas_call(
        _combine_kernel,
        grid_spec=pltpu.PrefetchScalarGridSpec(
            num_scalar_prefetch=2,
            grid=(n // TM_COMB,),
            in_specs=[
                pl.BlockSpec((TM_COMB, D_MODEL), lambda i, *_: (i, 0)),
                pl.BlockSpec((TM_COMB, ROUTE_W), lambda i, *_: (i, 0)),
                pl.BlockSpec((1, D_MODEL), lambda i, *_: (0, 0)),
                pl.BlockSpec((1, D_MODEL), lambda i, *_: (0, 0)),
                pl.BlockSpec(memory_space=pl.ANY),
            ],
            out_specs=pl.BlockSpec((TM_COMB, D_MODEL), lambda i, *_: (i, 0)),
            scratch_shapes=[
                pltpu.VMEM((2, 2, TM_COMB, RT, LANES), BF16),
                pltpu.VMEM((64 * PITCH, LANES), F32),
                pltpu.SemaphoreType.DMA((2,)),
            ],
        ),
        out_shape=jax.ShapeDtypeStruct((n, D_MODEL), F32),
        compiler_params=pltpu.CompilerParams(
            dimension_semantics=("arbitrary",),
            vmem_limit_bytes=VMEM_LIMIT),
        name="combine",
    )(pos0, pos1, h1, info, g2, b2, ys)


def _block_diag(w, per):
    h, hd, _ = w.shape
    wg = w.reshape(h // per, per, hd, hd)
    eye = jnp.eye(per, dtype=w.dtype)
    return jnp.einsum("gpij,pq->gpiqj", wg, eye).reshape(h // per, per * hd, per * hd)


def kernel(x, ln_in_g, ln_in_b, w_in, lru_conv_w, lru_conv_b, lru_w_a, lru_b_a, lru_w_x, lru_b_x,
           lru_lambda, conf_conv_w, conf_conv_b, conf_ln_g, conf_ln_b, w_out, ln1_g, ln1_b,
           router_group_w, router_group_b, router_expert_w, router_expert_b, exp_w_gate, exp_w_up,
           exp_w_down, ln2_g, ln2_b):
    bsz, seq, d = x.shape
    n = bsz * seq
    x2 = x.reshape(n, d)
    row = lambda v: v.reshape(1, -1).astype(F32)
    l = 0

    z, h0 = _ln_win(x2, row(ln_in_g), row(ln_in_b), w_in[l])

    per = CB_LRU // LRU_HEAD_DIM
    wcat = jnp.concatenate([_block_diag(lru_w_a[l], per), _block_diag(lru_w_x[l], per)],
                           axis=-1).astype(BF16)
    a_out = _lru_mixer(z, lru_conv_w[l], row(lru_conv_b[l]), wcat, row(lru_b_a[l]),
                       row(lru_b_x[l]), row(lru_lambda[l]), bsz, seq)

    nlb = D_CONV // LANES
    w3 = jnp.pad(conf_conv_w[l], ((0, 32 - CONF_CONV_W), (0, 0)))
    w3 = w3.reshape(32, nlb, LANES).transpose(1, 0, 2)
    cb3 = conf_conv_b[l].reshape(nlb, 1, LANES)
    b_out, wo = _conf_mixer(z, w3, cb3, row(conf_ln_g[l]), row(conf_ln_b[l]), w_out[l], bsz, seq)

    wr = jnp.concatenate([router_group_w[l], router_expert_w[l]], axis=1)
    wr = jnp.pad(wr, ((0, 0), (0, ROUTE_W - wr.shape[1])))
    wr_hi = wr.astype(BF16)
    wr_lo = (wr - wr_hi.astype(F32)).astype(BF16)
    wr_cat = jnp.concatenate([wr_hi, wr_lo], axis=1)
    br = jnp.concatenate([router_group_b[l], router_expert_b[l]])
    br = jnp.pad(br, (0, ROUTE_W - br.shape[0])).reshape(1, ROUTE_W)
    h1, h1p, logits = _wout_router(a_out, b_out, h0, wo, row(ln1_g[l]), row(ln1_b[l]), wr_cat, br)

    info, counts = _route(logits)
    idx = info[:, 0:6].astype(jnp.int32)
    r0, r1, e0, e1 = idx[:, 0], idx[:, 1], idx[:, 4], idx[:, 5]

    n_tiles = (n * 2) // TM_EXP + N_EXPERTS
    tile_expert, n_used, next_expert, slot, starts, pad_start, pad_len = _plan(
        counts[0].astype(jnp.int32), n_tiles)

    expert_ids = jnp.arange(N_EXPERTS, dtype=jnp.int32)[None, :]
    start_of = lambda e: jnp.sum(jnp.where(e[:, None] == expert_ids, starts[None, :], 0), axis=1)
    pos0 = (start_of(e0) + r0).astype(jnp.int32)
    pos1 = (start_of(e1) + r1).astype(jnp.int32)

    xs = _dispatch(pos0, pos1, pad_start, pad_len, h1p, n_tiles * TM_EXP)
    shp = (N_EXPERTS, D_MODEL, D_EXPERT)
    ys = _expert_ffn(tile_expert, n_used, next_expert, slot, xs, exp_w_gate[l].reshape(shp),
                     exp_w_up[l].reshape(shp), exp_w_down[l].reshape(N_EXPERTS, D_EXPERT, D_MODEL))
    out = _combine(pos0, pos1, h1, info, row(ln2_g[l]), row(ln2_b[l]), ys)
    return out.reshape(bsz, seq, d)
```
